```python
import math
import jax
import jax.numpy as jnp
from jax import lax
import numpy as np

D_MODEL = 1024
BATCH = 16
SEQ = 4096
DEPTH = 4

GRID_W = 64
CTX_LEN = 256
EPS = 1e-6
D_FF = 2816
FFN_RES_W = 0.5
N_MOD = 9
HY_W = 256
SSD_HEADS = 8
SSD_HEAD_DIM = 64
SSD_W = SSD_HEADS * SSD_HEAD_DIM
SSD_GROUPS = 2
SSD_STATE = 128
SSD_CHUNK = 64
SC_W = 256
D_MIX = HY_W + SSD_W + SC_W
SHORT_K = 3
HY_ORDER = 2
HY_EMB = 33
HY_FH = 64
HY_FOUT = HY_ORDER * 2 * HY_W
HY_FAST = 0.3
HY_SLOW = 1.5
HY_TARGET = 1e-2
HY_IN = 3 * HY_W
SSD_XBC = SSD_W + 2 * SSD_GROUPS * SSD_STATE
SC_IN = 3 * SC_W
IN_SPLITS = (HY_IN, HY_IN + SSD_W, HY_IN + SSD_W + SSD_XBC, HY_IN + SSD_W + SSD_XBC + 2 * SSD_HEADS)
D_IN = IN_SPLITS[-1] + SC_IN

kernel_name = 'hybrid_hyena_ssd_shortconv_dit'


def rmsnorm(x, g):
    xf = x.astype(jnp.float32)
    y = xf * lax.rsqrt(jnp.mean(xf * xf, axis=-1, keepdims=True) + EPS)
    return (y * g.astype(jnp.float32)).astype(x.dtype)


def sub_in(x, g_pre, mod, i):
    return rmsnorm(x, g_pre) * (1.0 + mod[:, :, 3 * i + 1]) + mod[:, :, 3 * i]


def sub_out(x, y, g_post, mod, i, res_w):
    return x + res_w * mod[:, :, 3 * i + 2] * rmsnorm(y, g_post)


def swiglu(h, w_in, w_out):
    g, u = jnp.split(h @ w_in, 2, axis=-1)
    return (jax.nn.silu(g) * u) @ w_out


def dwconv(u, w, b, n_seg):
    bsz, L, C = u.shape
    K = w.shape[0]
    p = K // 2
    seg = L // n_seg
    up = jnp.pad(u.reshape(bsz, n_seg, seg, C), ((0, 0), (0, 0), (p, p), (0, 0)))
    y = up[:, :, 0:seg] * w[0]
    for j in range(1, K):
        y = y + up[:, :, j:j + seg] * w[j]
    if b is not None:
        y = y + b
    return y.reshape(bsz, L, C)


def hyena_filters(L, fw1, fb1, fw2, fb2, fw3, fb3, fw4, freq):
    f32 = jnp.float32
    t = jnp.linspace(0.0, 1.0, L, dtype=f32)[:, None]
    bands = (HY_EMB - 1) // 2
    w = 2.0 * math.pi * jnp.arange(L, dtype=f32)[:, None] / L
    f = jnp.linspace(1e-4, bands - 1, bands, dtype=f32)[None, :]
    z = jnp.concatenate([t, jnp.cos(f * w), -jnp.sin(f * w)], axis=-1)
    fr = freq.astype(f32)
    h = jnp.sin(fr * (z @ fw1.astype(f32) + fb1.astype(f32)))
    h = jnp.sin(fr * (h @ fw2.astype(f32) + fb2.astype(f32)))
    h = jnp.sin(fr * (h @ fw3.astype(f32) + fb3.astype(f32)))
    h = h @ fw4.astype(f32)
    deltas = jnp.linspace(math.log(HY_TARGET) / HY_SLOW, math.log(HY_TARGET) / HY_FAST, HY_W, dtype=f32)
    decay = jnp.exp(-t * jnp.abs(deltas))
    h = h.reshape(L, HY_ORDER, 2, HY_W) * decay[:, None, None, :]
    k = jnp.concatenate([h[:, :, 0], jnp.zeros((1, HY_ORDER, HY_W), f32), h[:0:-1, :, 1]], axis=0)
    return jnp.fft.rfft(k, axis=0)


def hyena_long_conv(u, kf, bias):
    L = u.shape[1]
    v, x1, x2 = jnp.split(u.astype(jnp.float32), 3, axis=-1)
    bias = bias.astype(jnp.float32)
    z = v
    for o, gate in enumerate((x1, x2)):
        zf = jnp.fft.rfft(z, n=2 * L, axis=1)
        z = gate * (jnp.fft.irfft(zf * kf[:, o], n=2 * L, axis=1)[:, :L] + bias[o] * z)
    return z


def ssd_scan(xdt, dA, Bm, Cm, h0, want_y):
    b, l, nh, p = xdt.shape
    g = Bm.shape[2]
    r = nh // g
    n = Bm.shape[-1]
    q = SSD_CHUNK
    c = l // q
    X = xdt.reshape(b, c, q, g, r, p)
    A = dA.reshape(b, c, q, g, r)
    Bc = Bm.reshape(b, c, q, g, n)
    Cc = Cm.reshape(b, c, q, g, n)
    Acs = jnp.cumsum(A, axis=2)
    A_tot = Acs[:, :, -1]
    decay_to_end = jnp.exp(A_tot[:, :, None] - Acs)
    chunk_states = jnp.einsum('bcqgn,bcqgr,bcqgrp->bcgrpn', Bc, decay_to_end, X)

    def step(h_prev, inp):
        s, a = inp
        return jnp.exp(a)[..., None, None] * h_prev + s, h_prev

    h_final, h_starts = lax.scan(step, h0, (jnp.moveaxis(chunk_states, 1, 0), jnp.moveaxis(A_tot, 1, 0)))
    if not want_y:
        return h_final
    h_starts = jnp.moveaxis(h_starts, 0, 1)
    seg = Acs[:, :, :, None] - Acs[:, :, None, :]
    mask = jnp.tril(jnp.ones((q, q), dtype=bool))[:, :, None, None]
    Lmat = jnp.exp(jnp.where(mask, seg, -jnp.inf))
    scores = jnp.einsum('bclgn,bcsgn->bclsg', Cc, Bc)
    y_diag = jnp.einsum('bclsg,bclsgr,bcsgrp->bclgrp', scores, Lmat, X)
    y_off = jnp.einsum('bclgn,bcgrpn,bclgr->bclgrp', Cc, h_starts, jnp.exp(Acs))
    return (y_diag + y_off).reshape(b, l, nh, p), h_final


def token_mixers(p, n_seg, kf, h0_f, h0_b, want_out, hy_conv_w, hy_conv_b, hy_bias, ssd_conv_w, ssd_conv_b,
                 ssd_a_log, ssd_dt_bias, ssd_d, sc_conv_w, mix_gain):
    f32 = jnp.float32
    bsz, L, _ = p.shape
    hy, z, xbc, dt_raw, scp = jnp.split(p, IN_SPLITS, axis=-1)
    xbc = jax.nn.silu(dwconv(xbc, ssd_conv_w, ssd_conv_b, n_seg)).astype(f32)
    xs, Bm, Cm = jnp.split(xbc, [SSD_W, SSD_W + SSD_GROUPS * SSD_STATE], axis=-1)
    xs = xs.reshape(bsz, L, SSD_HEADS, SSD_HEAD_DIM)
    Bm = Bm.reshape(bsz, L, SSD_GROUPS, SSD_STATE)
    Cm = Cm.reshape(bsz, L, SSD_GROUPS, SSD_STATE)
    dt = jax.nn.softplus(dt_raw.astype(f32).reshape(bsz, L, 2, SSD_HEADS) + ssd_dt_bias.astype(f32))
    a = -jnp.exp(ssd_a_log.astype(f32))
    dt_f = dt[:, :, 0]
    res_f = ssd_scan(xs * dt_f[..., None], dt_f * a[0], Bm, Cm, h0_f, want_out)
    rev = lambda t: jnp.flip(t, axis=1)
    dt_b = rev(dt[:, :, 1])
    res_b = ssd_scan(rev(xs) * dt_b[..., None], dt_b * a[1], rev(Bm), rev(Cm), h0_b, want_out)
    if not want_out:
        return None, res_f, res_b
    (y_f, h_f), (y_b, h_b) = res_f, res_b
    y = y_f + rev(y_b) + ssd_d.astype(f32)[:, None] * xs
    y_ssd = rmsnorm(y.reshape(bsz, L, SSD_W) * jax.nn.silu(z.astype(f32)), mix_gain[HY_W:HY_W + SSD_W])
    y_hy = rmsnorm(hyena_long_conv(dwconv(hy, hy_conv_w, hy_conv_b, n_seg), kf, hy_bias), mix_gain[:HY_W])
    gb, gc, hx = jnp.split(scp, 3, axis=-1)
    y_sc = rmsnorm(gb * dwconv(gc * hx, sc_conv_w, None, n_seg), mix_gain[HY_W + SSD_W:])
    out = jnp.concatenate([y_hy.astype(p.dtype), y_ssd.astype(p.dtype), y_sc], axis=-1)
    return out, h_f, h_b


def setup_inputs(seed: int = 0) -> dict:
    key = jax.random.key(seed)
    ks = iter(jax.random.split(key, 40))
    f32 = jnp.float32
    nrm = lambda shape, scale: scale * jax.random.normal(next(ks), shape, f32)
    gain = lambda shape: 1.0 + 0.02 * jax.random.normal(next(ks), shape, f32)
    x = nrm((BATCH, SEQ, D_MODEL), 1.0)
    c = nrm((BATCH, D_MODEL), 1.0)
    ctx = nrm((BATCH, CTX_LEN, D_MODEL), 1.0)
    c_ctx = nrm((D_MODEL,), 1.0)
    w_mod = nrm((DEPTH, D_MODEL, N_MOD * D_MODEL), 0.5 * D_MODEL ** -0.5)
    b_mod = nrm((DEPTH, N_MOD * D_MODEL), 0.02)
    norm_g = gain((DEPTH, 6, D_MODEL))
    ffn_w_in = nrm((DEPTH, 2, D_MODEL, 2 * D_FF), D_MODEL ** -0.5)
    ffn_w_out = nrm((DEPTH, 2, D_FF, D_MODEL), D_FF ** -0.5)
    w_in = nrm((DEPTH, D_MODEL, D_IN), D_MODEL ** -0.5)
    w_out = nrm((DEPTH, D_MIX, D_MODEL), D_MIX ** -0.5)
    hy_conv_w = nrm((DEPTH, SHORT_K, HY_IN), SHORT_K ** -0.5)
    hy_conv_b = nrm((DEPTH, HY_IN), 0.02)
    hy_fw1 = nrm((DEPTH, HY_EMB, HY_FH), HY_EMB ** -0.5)
    hy_fb1 = nrm((DEPTH, HY_FH), 0.02)
    hy_fw2 = nrm((DEPTH, HY_FH, HY_FH), HY_FH ** -0.5)
    hy_fb2 = nrm((DEPTH, HY_FH), 0.02)
    hy_fw3 = nrm((DEPTH, HY_FH, HY_FH), HY_FH ** -0.5)
    hy_fb3 = nrm((DEPTH, HY_FH), 0.02)
    hy_fw4 = nrm((DEPTH, HY_FH, HY_FOUT), HY_FH ** -0.5)
    hy_freq = gain((DEPTH, HY_FH))
    hy_bias = nrm((DEPTH, HY_ORDER, HY_W), 1.0)
    ssd_conv_w = nrm((DEPTH, SHORT_K, SSD_XBC), SHORT_K ** -0.5)
    ssd_conv_b = nrm((DEPTH, SSD_XBC), 0.02)
    ssd_a_log = jnp.log(jax.random.uniform(next(ks), (DEPTH, 2, SSD_HEADS), f32, minval=1.0, maxval=16.0))
    u = jax.random.uniform(next(ks), (DEPTH, 2, SSD_HEADS), f32)
    dt0 = jnp.exp(u * (math.log(0.1) - math.log(1e-3)) + math.log(1e-3))
    ssd_dt_bias = dt0 + jnp.log(-jnp.expm1(-dt0))
    ssd_d = gain((DEPTH, SSD_HEADS))
    sc_conv_w = nrm((DEPTH, SHORT_K, SC_W), SHORT_K ** -0.5)
    mix_gain = gain((DEPTH, D_MIX))
    return {'x': x, 'c': c, 'ctx': ctx, 'c_ctx': c_ctx, 'w_mod': w_mod, 'b_mod': b_mod, 'norm_g': norm_g,
            'ffn_w_in': ffn_w_in, 'ffn_w_out': ffn_w_out, 'w_in': w_in, 'w_out': w_out,
            'hy_conv_w': hy_conv_w, 'hy_conv_b': hy_conv_b, 'hy_fw1': hy_fw1, 'hy_fb1': hy_fb1,
            'hy_fw2': hy_fw2, 'hy_fb2': hy_fb2, 'hy_fw3': hy_fw3, 'hy_fb3': hy_fb3, 'hy_fw4': hy_fw4,
            'hy_freq': hy_freq, 'hy_bias': hy_bias, 'ssd_conv_w': ssd_conv_w, 'ssd_conv_b': ssd_conv_b,
            'ssd_a_log': ssd_a_log, 'ssd_dt_bias': ssd_dt_bias, 'ssd_d': ssd_d, 'sc_conv_w': sc_conv_w,
            'mix_gain': mix_gain}


def reference(x, c, ctx, c_ctx, w_mod, b_mod, norm_g, ffn_w_in, ffn_w_out, w_in, w_out, hy_conv_w, hy_conv_b,
              hy_fw1, hy_fb1, hy_fw2, hy_fb2, hy_fw3, hy_fb3, hy_fw4, hy_freq, hy_bias, ssd_conv_w, ssd_conv_b,
              ssd_a_log, ssd_dt_bias, ssd_d, sc_conv_w, mix_gain):
    bsz, n_lat, _ = x.shape
    rows = n_lat // GRID_W
    cbsz, ctx_len, _ = ctx.shape
    silu_c = jax.nn.silu(c)
    silu_cc = jax.nn.silu(c_ctx)
    xc = ctx
    h_zero = jnp.zeros((cbsz, SSD_GROUPS, SSD_HEADS // SSD_GROUPS, SSD_HEAD_DIM, SSD_STATE), jnp.float32)
    for l in range(DEPTH):
        last = l == DEPTH - 1
        mod_x = (silu_c @ w_mod[l] + b_mod[l]).reshape(bsz, 1, N_MOD, D_MODEL)
        mod_c = (silu_cc @ w_mod[l] + b_mod[l]).reshape(1, 1, N_MOD, D_MODEL)
        g = norm_g[l]
        mp = (hy_conv_w[l], hy_conv_b[l], hy_bias[l], ssd_conv_w[l], ssd_conv_b[l], ssd_a_log[l],
              ssd_dt_bias[l], ssd_d[l], sc_conv_w[l], mix_gain[l])
        fp = (hy_fw1[l], hy_fb1[l], hy_fw2[l], hy_fb2[l], hy_fw3[l], hy_fb3[l], hy_fw4[l], hy_freq[l])
        x = sub_out(x, swiglu(sub_in(x, g[0], mod_x, 0), ffn_w_in[l, 0], ffn_w_out[l, 0]), g[1], mod_x, 0, FFN_RES_W)
        xc = sub_out(xc, swiglu(sub_in(xc, g[0], mod_c, 0), ffn_w_in[l, 0], ffn_w_out[l, 0]), g[1], mod_c, 0, FFN_RES_W)
        pc = sub_in(xc, g[2], mod_c, 1) @ w_in[l]
        kf_ctx = None if last else hyena_filters(ctx_len, *fp)
        y_c, h_f, h_b = token_mixers(pc, 1, kf_ctx, h_zero, h_zero, not last, *mp)
        pl = sub_in(x, g[2], mod_x, 1) @ w_in[l]
        y_l, _, _ = token_mixers(pl, rows, hyena_filters(n_lat, *fp), h_f, h_b, True, *mp)
        x = sub_out(x, y_l @ w_out[l], g[3], mod_x, 1, 1.0)
        x = sub_out(x, swiglu(sub_in(x, g[4], mod_x, 2), ffn_w_in[l, 1], ffn_w_out[l, 1]), g[5], mod_x, 2, FFN_RES_W)
        if not last:
            xc = sub_out(xc, y_c @ w_out[l], g[3], mod_c, 1, 1.0)
            xc = sub_out(xc, swiglu(sub_in(xc, g[4], mod_c, 2), ffn_w_in[l, 1], ffn_w_out[l, 1]), g[5], mod_c, 2, FFN_RES_W)
    return x
```

```python
import functools
import math

import jax
import jax.numpy as jnp
import numpy as np
from jax import lax
from jax.experimental import pallas as pl
from jax.experimental.pallas import tpu as pltpu

GRID_W = 64
EPS = 1e-6
FFN_RES_W = 0.5
N_MOD = 9
HY_W = 256
SSD_HEADS = 8
SSD_HEAD_DIM = 64
SSD_GROUPS = 2
SSD_STATE = 128
SC_W = 256
SHORT_K = 3
HY_ORDER = 2
HY_EMB = 33
HY_FAST = 0.3
HY_SLOW = 1.5
HY_TARGET = 1e-2

SSD_W = SSD_HEADS * SSD_HEAD_DIM
HY_IN = 3 * HY_W
SSD_XBC = SSD_W + 2 * SSD_GROUPS * SSD_STATE
SC_IN = 3 * SC_W
N_DT = 2 * SSD_HEADS

LANE = 128
VMEM_LIMIT = 56 * 1024 * 1024

F32 = jnp.float32
BF16 = jnp.bfloat16


def _cparams(sem):
    return pltpu.CompilerParams(dimension_semantics=sem, vmem_limit_bytes=VMEM_LIMIT)


def _const_spec(shape):
    nd = len(shape)
    return pl.BlockSpec(shape, lambda *_: (0,) * nd, pipeline_mode=pl.Buffered(1))


def _dot(a, b):
    return jnp.dot(a, b, preferred_element_type=F32)


def _split2(a):
    hi = a.astype(BF16)
    lo = (a - hi.astype(F32)).astype(BF16)
    return hi, lo


def _split3(a):
    hi = a.astype(BF16)
    r = a - hi.astype(F32)
    mid = r.astype(BF16)
    lo = (r - mid.astype(F32)).astype(BF16)
    return hi, mid, lo


def _dot_hp(a, b):
    ah, al = _split2(a)
    bh, bl = _split2(b)
    return _dot(ah, bh) + (_dot(ah, bl) + _dot(al, bh))


def _silu(x):
    return x / (1.0 + jnp.exp(-x))


def _rms(x, g):
    ms = jnp.mean(x * x, axis=-1, keepdims=True)
    return x * lax.rsqrt(ms + EPS) * g


def _mod_kernel(c_ref, w_ref, b_ref, o_ref):
    o_ref[0] = _dot_hp(_silu(c_ref[...]), w_ref[0]) + b_ref[0]


def _mod_table(cvec, w_mod, b_mod):
    depth, d, nm = w_mod.shape
    r = cvec.shape[0]
    tn = 1024 if nm % 1024 == 0 else d
    return pl.pallas_call(
        _mod_kernel,
        grid=(depth, nm // tn),
        in_specs=[
            pl.BlockSpec((r, d), lambda l, n: (0, 0)),
            pl.BlockSpec((1, d, tn), lambda l, n: (l, 0, n)),
            pl.BlockSpec((1, 1, tn), lambda l, n: (l, 0, n)),
        ],
        out_specs=pl.BlockSpec((1, r, tn), lambda l, n: (l, 0, n)),
        out_shape=jax.ShapeDtypeStruct((depth, r, nm), F32),
        compiler_params=_cparams(("parallel", "parallel")),
        name="mod_table",
    )(cvec, w_mod, b_mod.reshape(depth, 1, nm))


def _ffn_kernel(x_ref, mod_ref, g_ref, wg_ref, wu_ref, wo_ref, o_ref, *, sub, n_chunk):
    x = x_ref[0]
    mod = mod_ref[0]
    g = g_ref[...]
    shift, scale, gate = mod[3 * sub:3 * sub + 1], mod[3 * sub + 1:3 * sub + 2], mod[3 * sub + 2:3 * sub + 3]
    g_pre, g_post = g[2 * sub:2 * sub + 1], g[2 * sub + 1:2 * sub + 2]
    h = (_rms(x, g_pre) * (1.0 + scale) + shift).astype(BF16)
    dff = wg_ref.shape[1]
    ck = dff // n_chunk
    acc = None
    for c in range(n_chunk):
        gg = _dot(h, wg_ref[:, c * ck:(c + 1) * ck])
        uu = _dot(h, wu_ref[:, c * ck:(c + 1) * ck])
        a = (_silu(gg) * uu).astype(BF16)
        part = _dot(a, wo_ref[c * ck:(c + 1) * ck, :])
        acc = part if acc is None else acc + part
    o_ref[0] = x + FFN_RES_W * gate * _rms(acc, g_post)


def _ffn(x, mod, g, wg, wu, wo, sub, tm):
    bsz, length, d = x.shape
    dff = wg.shape[1]
    n_chunk = 2 if dff % (2 * LANE) == 0 else 1
    per_batch = mod.shape[0] != 1
    return pl.pallas_call(
        functools.partial(_ffn_kernel, sub=sub, n_chunk=n_chunk),
        grid=(bsz, length // tm),
        in_specs=[
            pl.BlockSpec((1, tm, d), lambda b, t: (b, t, 0)),
            pl.BlockSpec((1, N_MOD, d), (lambda b, t: (b, 0, 0)) if per_batch else (lambda b, t: (0, 0, 0))),
            _const_spec(g.shape),
            _const_spec(wg.shape),
            _const_spec(wu.shape),
            _const_spec(wo.shape),
        ],
        out_specs=pl.BlockSpec((1, tm, d), lambda b, t: (b, t, 0)),
        out_shape=jax.ShapeDtypeStruct(x.shape, F32),
        compiler_params=_cparams(("parallel", "parallel")),
        name="ffn",
    )(x, mod, g, wg, wu, wo)


def _conv3(a, w, seg):
    n = a.shape[0]
    row = lax.broadcasted_iota(jnp.int32, a.shape, 0) % seg
    prev = jnp.where(row == 0, 0.0, pltpu.roll(a, 1, 0))
    nxt = jnp.where(row == seg - 1, 0.0, pltpu.roll(a, n - 1, 0))
    return prev * w[0:1] + a * w[1:2] + nxt * w[2:3]


def _inproj_kernel(x_ref, mod_ref, g_ref, w_ref, wdt_ref, wdtt_ref, hyw_ref, hyb_ref, sw_ref, sb_ref, scw_ref,
                   scg_ref, dtb_ref, dtbt_ref, hy_ref, z_ref, xbc_ref, sc_ref, dt_ref, dtt_ref, *, seg):
    x = x_ref[0]
    mod = mod_ref[0]
    h = (_rms(x, g_ref[2:3]) * (1.0 + mod[4:5]) + mod[3:4]).astype(BF16)
    o_z, o_xbc, o_sc = HY_IN, HY_IN + SSD_W, HY_IN + SSD_W + SSD_XBC
    hy_ref[0] = _conv3(_dot(h, w_ref[:, :o_z]), hyw_ref[...], seg) + hyb_ref[...]
    z_ref[0] = _dot(h, w_ref[:, o_z:o_xbc]).astype(BF16)
    xbc = _conv3(_dot(h, w_ref[:, o_xbc:o_sc]), sw_ref[...], seg) + sb_ref[...]
    xbc_ref[0] = _silu(xbc).astype(BF16)
    gb = _dot(h, w_ref[:, o_sc:o_sc + SC_W])
    gc = _dot(h, w_ref[:, o_sc + SC_W:o_sc + 2 * SC_W])
    hx = _dot(h, w_ref[:, o_sc + 2 * SC_W:])
    sc_ref[0] = _rms(gb * _conv3(gc * hx, scw_ref[...], seg), scg_ref[...]).astype(BF16)
    dt_ref[0] = jax.nn.softplus(_dot(h, wdt_ref[...]) + dtb_ref[...])
    dtt = lax.dot_general(wdtt_ref[...], h, (((1,), (1,)), ((), ())), preferred_element_type=F32)
    dtt_ref[0] = jax.nn.softplus(dtt + dtbt_ref[...])


def _inproj(x, mod, g, lw, seg, tm):
    bsz, length, d = x.shape
    per_batch = mod.shape[0] != 1
    tok = lambda w: pl.BlockSpec((1, tm, w), lambda b, t: (b, t, 0))
    consts = (lw["w_in"], lw["w_dt"], lw["w_dtt"], lw["hy_conv_w"], lw["hy_conv_b"], lw["ssd_conv_w"],
              lw["ssd_conv_b"], lw["sc_conv_w"], lw["sc_gain"], lw["dt_bias"], lw["dt_bias_t"])
    return pl.pallas_call(
        functools.partial(_inproj_kernel, seg=seg),
        grid=(bsz, length // tm),
        in_specs=[
            tok(d),
            pl.BlockSpec((1, N_MOD, d), (lambda b, t: (b, 0, 0)) if per_batch else (lambda b, t: (0, 0, 0))),
            _const_spec(g.shape),
        ] + [_const_spec(a.shape) for a in consts],
        out_specs=[tok(HY_IN), tok(SSD_W), tok(SSD_XBC), tok(SC_W), tok(N_DT),
                   pl.BlockSpec((1, N_DT, tm), lambda b, t: (b, 0, t))],
        out_shape=[
            jax.ShapeDtypeStruct((bsz, length, HY_IN), F32),
            jax.ShapeDtypeStruct((bsz, length, SSD_W), BF16),
            jax.ShapeDtypeStruct((bsz, length, SSD_XBC), BF16),
            jax.ShapeDtypeStruct((bsz, length, SC_W), BF16),
            jax.ShapeDtypeStruct((bsz, length, N_DT), F32),
            jax.ShapeDtypeStruct((bsz, N_DT, length), F32),
        ],
        compiler_params=_cparams(("parallel", "parallel")),
        name="inproj",
    )(x, mod, g, *consts)


def _dft_mats(p):
    f = np.arange(p, dtype=np.int64)[:, None]
    t = np.arange(p, dtype=np.int64)[None, :]
    ang = (np.pi / (2 * p)) * (((2 * f + 1) * t) % (4 * p)).astype(np.float64)
    fwd = np.concatenate([np.cos(ang), -np.sin(ang)], axis=0)
    return jnp.asarray(fwd, F32), jnp.asarray(fwd.T / p, F32)


def _filter_kernel(fb_ref, dl_ref, w1t_ref, w1c_ref, w1s_ref, b1_ref, w2_ref, b2_ref, w3_ref, b3_ref, w4_ref,
                   fr_ref, k_ref, *, length, tr):
    lag = pl.program_id(0) * tr + lax.broadcasted_iota(jnp.int32, (tr, 1), 0) - length
    n = jnp.abs(lag).astype(F32)
    t = n * (1.0 / (length - 1))
    w = n * (2.0 * math.pi / length)
    ang = fb_ref[...] * w
    fr = fr_ref[...]
    h = t * w1t_ref[...] + _dot_hp(jnp.cos(ang), w1c_ref[...]) - _dot_hp(jnp.sin(ang), w1s_ref[...])
    h = jnp.sin(fr * (h + b1_ref[...]))
    h = jnp.sin(fr * (_dot_hp(h, w2_ref[...]) + b2_ref[...]))
    h = jnp.sin(fr * (_dot_hp(h, w3_ref[...]) + b3_ref[...]))
    decay = jnp.exp(-t * jnp.abs(dl_ref[...]))
    for o in range(HY_ORDER):
        base = o * 2 * HY_W
        kf = _dot_hp(h, w4_ref[:, base:base + HY_W]) * decay
        kb = _dot_hp(h, w4_ref[:, base + HY_W:base + 2 * HY_W]) * decay
        k_ref[:, o * HY_W:(o + 1) * HY_W] = jnp.where(lag >= 0, kf, jnp.where(lag == -length, 0.0, kb))


def _hyena_filter(fw, length, tr):
    bands = (HY_EMB - 1) // 2
    fbands = jnp.linspace(1e-4, bands - 1, bands, dtype=F32)[None, :]
    deltas = jnp.linspace(math.log(HY_TARGET) / HY_SLOW, math.log(HY_TARGET) / HY_FAST, HY_W, dtype=F32)[None, :]
    fw1 = fw["fw1"]
    consts = (fbands, deltas, fw1[0:1], fw1[1:1 + bands], fw1[1 + bands:], fw["fb1"][None], fw["fw2"],
              fw["fb2"][None], fw["fw3"], fw["fb3"][None], fw["fw4"], fw["freq"][None])
    return pl.pallas_call(
        functools.partial(_filter_kernel, length=length, tr=tr),
        grid=(2 * length // tr,),
        in_specs=[pl.BlockSpec(a.shape, lambda i: (0, 0)) for a in consts],
        out_specs=pl.BlockSpec((tr, HY_ORDER * HY_W), lambda i: (i, 0)),
        out_shape=jax.ShapeDtypeStruct((2 * length, HY_ORDER * HY_W), F32),
        compiler_params=_cparams(("parallel",)),
        name="hyena_filter",
    )(*consts)


def _kspec_kernel(f_ref, k0_ref, k1_ref, o_ref, *, p):
    fwd = f_ref[...]
    a = _dot_hp(fwd, k1_ref[...])
    a1 = _dot_hp(fwd, k0_ref[...])
    a1_re = a1[:p] - k0_ref[0:1, :]
    a1_im = a1[p:]
    sign = 1.0 - 2.0 * (lax.broadcasted_iota(jnp.int32, (p, 1), 0) % 2).astype(F32)
    o_ref[0, :p, :] = a[:p] - sign * a1_im
    o_ref[0, p:, :] = a[p:] + sign * a1_re


def _hyena_spectra(kfull, fwd, p):
    nb = kfull.shape[0] // (2 * p)
    cols = kfull.shape[1]
    return pl.pallas_call(
        functools.partial(_kspec_kernel, p=p),
        grid=(2 * nb - 1,),
        in_specs=[
            pl.BlockSpec(fwd.shape, lambda i: (0, 0)),
            pl.BlockSpec((p, cols), lambda i: (i, 0)),
            pl.BlockSpec((p, cols), lambda i: (i + 1, 0)),
        ],
        out_specs=pl.BlockSpec((1, 2 * p, cols), lambda i: (i, 0, 0)),
        out_shape=jax.ShapeDtypeStruct((2 * nb - 1, 2 * p, cols), F32),
        compiler_params=_cparams(("parallel",)),
        name="hyena_spectra",
    )(fwd, kfull, kfull)


HY_CT = 128
HY_NB = 2
HY_RT = 32


def _hyconv_kernel(zin_ref, gate_ref, k_ref, bias_ref, fwd_ref, inv_ref, o_ref, zf_ref, y_ref, *, nb, p):
    fwd = fwd_ref[...]
    inv = inv_ref[...]
    for j in range(nb):
        zb = jnp.concatenate([zin_ref[s, j * p:(j + 1) * p, :] for s in range(HY_NB)], axis=1).astype(BF16)
        zf_ref[j] = _dot(fwd, zb)

    def block(i, carry):
        def tile(rt, carry2):
            r0 = pl.multiple_of(rt * HY_RT, HY_RT)
            acc_re = [jnp.zeros((HY_RT, HY_CT), F32) for _ in range(HY_NB)]
            acc_im = [jnp.zeros((HY_RT, HY_CT), F32) for _ in range(HY_NB)]
            for j in range(nb):
                d = i - j + (nb - 1)
                kre = k_ref[d, pl.ds(r0, HY_RT), :]
                kim = k_ref[d, pl.ds(p + r0, HY_RT), :]
                zre = zf_ref[j, pl.ds(r0, HY_RT), :]
                zim = zf_ref[j, pl.ds(p + r0, HY_RT), :]
                for s in range(HY_NB):
                    zr = zre[:, s * HY_CT:(s + 1) * HY_CT]
                    zi = zim[:, s * HY_CT:(s + 1) * HY_CT]
                    acc_re[s] = acc_re[s] + (kre * zr - kim * zi)
                    acc_im[s] = acc_im[s] + (kre * zi + kim * zr)
            y_ref[pl.ds(r0, HY_RT), :] = jnp.concatenate(acc_re, axis=1).astype(BF16)
            y_ref[pl.ds(p + r0, HY_RT), :] = jnp.concatenate(acc_im, axis=1).astype(BF16)
            return carry2

        lax.fori_loop(0, p // HY_RT, tile, 0)
        y = _dot(inv, y_ref[...])
        rows = pl.ds(pl.multiple_of(i * p, p), p)
        for s in range(HY_NB):
            zs = zin_ref[s, rows, :]
            o_ref[s, rows, :] = gate_ref[s, rows, :] * (y[:, s * HY_CT:(s + 1) * HY_CT] + bias_ref[...] * zs)
        return carry

    lax.fori_loop(0, nb, block, 0)


def _hyconv(zin, zin_col, gate, gate_col, kspec, k_col, bias, fwd, inv, p):
    bsz, length, _ = zin.shape
    nb = length // p
    nct = HY_W // HY_CT
    nd = kspec.shape[0]
    return pl.pallas_call(
        functools.partial(_hyconv_kernel, nb=nb, p=p),
        grid=(nct, bsz // HY_NB),
        in_specs=[
            pl.BlockSpec((HY_NB, length, HY_CT), lambda c, b: (b, 0, zin_col + c)),
            pl.BlockSpec((HY_NB, length, HY_CT), lambda c, b: (b, 0, gate_col + c)),
            pl.BlockSpec((nd, 2 * p, HY_CT), lambda c, b: (0, 0, k_col + c), pipeline_mode=pl.Buffered(1)),
            pl.BlockSpec((1, HY_CT), lambda c, b: (0, c)),
            _const_spec(fwd.shape),
            _const_spec(inv.shape),
        ],
        out_specs=pl.BlockSpec((HY_NB, length, HY_CT), lambda c, b: (b, 0, c)),
        out_shape=jax.ShapeDtypeStruct((bsz, length, HY_W), F32),
        scratch_shapes=[
            pltpu.VMEM((nb, 2 * p, HY_NB * HY_CT), F32),
            pltpu.VMEM((2 * p, HY_NB * HY_CT), BF16),
        ],
        compiler_params=_cparams(("arbitrary", "arbitrary")),
        name="hyconv",
    )(zin, gate, kspec, bias, fwd, inv)


def _ssd_dir(xbc_ref, dt_ref, dtt_ref, a_row, a_col, h_ref, y_ref, *, q, rev, first):
    n, hp, hd = SSD_STATE, SSD_HEADS // SSD_GROUPS, SSD_HEAD_DIM
    c0 = SSD_HEADS if rev else 0
    dt_col = dt_ref[0, :, c0:c0 + SSD_HEADS]
    dt_row = dtt_ref[0, c0:c0 + SSD_HEADS, :]
    li = lax.broadcasted_iota(jnp.int32, (q, q), 0)
    si = lax.broadcasted_iota(jnp.int32, (q, q), 1)
    keep = (li <= si) if rev else (li >= si)
    tri = jnp.where(keep, 1.0, 0.0).astype(BF16)
    tri_t = jnp.where((li >= si) if rev else (li <= si), 1.0, 0.0).astype(BF16)
    acs_col = sum(_dot(tri, part) for part in _split3(dt_col * a_row))
    acs_row = sum(_dot(part, tri_t) for part in _split3(dt_row * a_col))
    end = 0 if rev else q - 1
    tot_row = acs_col[end:end + 1, :]
    w_col = jnp.exp(tot_row - acs_col) * dt_col
    e_col = jnp.exp(acs_col)
    e_tot = jnp.exp(tot_row)
    for g in range(SSD_GROUPS):
        xg = xbc_ref[0, :, g * hp * hd:(g + 1) * hp * hd]
        bg = xbc_ref[0, :, SSD_W + g * n:SSD_W + (g + 1) * n]
        cg = xbc_ref[0, :, SSD_W + SSD_GROUPS * n + g * n:SSD_W + SSD_GROUPS * n + (g + 1) * n]
        scores = lax.dot_general(cg, bg, (((1,), (1,)), ((), ())), preferred_element_type=F32)
        hg = h_ref[g]
        y_off = _dot(cg, hg.astype(BF16))
        ys, xw, dec = [], [], []
        for r in range(hp):
            hh = g * hp + r
            seg = acs_col[:, hh:hh + 1] - acs_row[hh:hh + 1, :]
            m = jnp.exp(jnp.where(keep, seg, -1e30)) * scores * dt_row[hh:hh + 1, :]
            xh = xg[:, r * hd:(r + 1) * hd]
            yd = _dot(m.astype(BF16), xh)
            ys.append(yd + e_col[:, hh:hh + 1] * y_off[:, r * hd:(r + 1) * hd])
            xw.append((xh.astype(F32) * w_col[:, hh:hh + 1]).astype(BF16))
            dec.append(jnp.broadcast_to(e_tot[:, hh:hh + 1], (1, hd)))
        y = jnp.concatenate(ys, axis=1)
        cols = slice(g * hp * hd, (g + 1) * hp * hd)
        if first:
            y_ref[0, :, cols] = y
        else:
            y_ref[0, :, cols] = y_ref[0, :, cols] + y
        upd = lax.dot_general(bg, jnp.concatenate(xw, axis=1), (((0,), (0,)), ((), ())),
                              preferred_element_type=F32)
        h_ref[g] = hg * jnp.concatenate(dec, axis=1) + upd


def _ssd_kernel(xf_ref, dtf_ref, dttf_ref, xb_ref, dtb_ref, dttb_ref, a_row_ref, a_col_ref, h0f_ref, h0b_ref,
                yf_ref, yb_ref, hf_out, hb_out, hf_ref, hb_ref, *, q):
    k = pl.program_id(1)

    @pl.when(k == 0)
    def _():
        hf_ref[...] = h0f_ref[0]
        hb_ref[...] = h0b_ref[0]

    a_row = a_row_ref[...]
    a_col = a_col_ref[...]
    _ssd_dir(xf_ref, dtf_ref, dttf_ref, a_row[:, :SSD_HEADS], a_col[:SSD_HEADS], hf_ref, yf_ref,
             q=q, rev=False, first=True)
    _ssd_dir(xb_ref, dtb_ref, dttb_ref, a_row[:, SSD_HEADS:], a_col[SSD_HEADS:], hb_ref, yb_ref,
             q=q, rev=True, first=True)

    @pl.when(k == pl.num_programs(1) - 1)
    def _():
        hf_out[0] = hf_ref[...]
        hb_out[0] = hb_ref[...]


def _ssd(xbc, dt, dtt, a, h0f, h0b, q):
    bsz, length, _ = xbc.shape
    nc = length // q
    a_row = a.reshape(1, N_DT)
    a_col = a.reshape(N_DT, 1)
    hshape = (SSD_GROUPS, SSD_STATE, SSD_W // SSD_GROUPS)
    fw = lambda b, k: (b, k, 0)
    bw = lambda b, k: (b, nc - 1 - k, 0)
    st = pl.BlockSpec((1,) + hshape, lambda b, k: (b, 0, 0, 0))
    return pl.pallas_call(
        functools.partial(_ssd_kernel, q=q),
        grid=(bsz, nc),
        in_specs=[
            pl.BlockSpec((1, q, SSD_XBC), fw),
            pl.BlockSpec((1, q, N_DT), fw),
            pl.BlockSpec((1, N_DT, q), lambda b, k: (b, 0, k)),
            pl.BlockSpec((1, q, SSD_XBC), bw),
            pl.BlockSpec((1, q, N_DT), bw),
            pl.BlockSpec((1, N_DT, q), lambda b, k: (b, 0, nc - 1 - k)),
            pl.BlockSpec((1, N_DT), lambda b, k: (0, 0)),
            pl.BlockSpec((N_DT, 1), lambda b, k: (0, 0)),
            st, st,
        ],
        out_specs=[pl.BlockSpec((1, q, SSD_W), fw), pl.BlockSpec((1, q, SSD_W), bw), st, st],
        out_shape=[
            jax.ShapeDtypeStruct((bsz, length, SSD_W), F32),
            jax.ShapeDtypeStruct((bsz, length, SSD_W), F32),
            jax.ShapeDtypeStruct((bsz,) + hshape, F32),
            jax.ShapeDtypeStruct((bsz,) + hshape, F32),
        ],
        scratch_shapes=[pltpu.VMEM(hshape, F32), pltpu.VMEM(hshape, F32)],
        compiler_params=_cparams(("parallel", "arbitrary")),
        name="ssd",
    )(xbc, dt, dtt, xbc, dt, dtt, a_row, a_col, h0f, h0b)


def _outproj_kernel(x_ref, mod_ref, g_ref, hy_ref, yf_ref, yb_ref, xs_ref, z_ref, sc_ref, dexp_ref, mg_ref,
                    why_ref, wssd_ref, wsc_ref, o_ref):
    x = x_ref[0]
    mod = mod_ref[0]
    mg = mg_ref[...]
    y_hy = _rms(hy_ref[0], mg[:, :HY_W]).astype(BF16)
    y = yf_ref[0] + yb_ref[0] + dexp_ref[...] * xs_ref[0].astype(F32)
    y_ssd = _rms(y * _silu(z_ref[0].astype(F32)), mg[:, HY_W:HY_W + SSD_W]).astype(BF16)
    o = _dot(y_hy, why_ref[...]) + _dot(y_ssd, wssd_ref[...]) + _dot(sc_ref[0], wsc_ref[...])
    o_ref[0] = x + mod[5:6] * _rms(o, g_ref[3:4])


def _outproj(x, mod, g, hy, yf, yb, xbc, z, sc, lw, tm):
    bsz, length, d = x.shape
    per_batch = mod.shape[0] != 1
    tok = lambda w: pl.BlockSpec((1, tm, w), lambda b, t: (b, t, 0))
    consts = (lw["d_exp"], lw["mix_gain"], lw["w_out_hy"], lw["w_out_ssd"], lw["w_out_sc"])
    return pl.pallas_call(
        _outproj_kernel,
        grid=(bsz, length // tm),
        in_specs=[
            tok(d),
            pl.BlockSpec((1, N_MOD, d), (lambda b, t: (b, 0, 0)) if per_batch else (lambda b, t: (0, 0, 0))),
            _const_spec(g.shape),
            tok(HY_W), tok(SSD_W), tok(SSD_W), tok(SSD_W), tok(SSD_W), tok(SC_W),
        ] + [_const_spec(a.shape) for a in consts],
        out_specs=tok(d),
        out_shape=jax.ShapeDtypeStruct(x.shape, F32),
        compiler_params=_cparams(("parallel", "parallel")),
        name="outproj",
    )(x, mod, g, hy, yf, yb, xbc, z, sc, *consts)


def _mixer(xs, mod, g, lw, kspec, dft, seg, tm, p, q, h0f, h0b, batch_shape, want_out):
    bsz, length = batch_shape
    hy, z, xbc, sc, dt, dtt = _inproj(xs, mod, g, lw, seg, tm)
    if xs.shape[0] != bsz:
        dtt = dtt.reshape(N_DT, bsz, length).transpose(1, 0, 2)
    seq = lambda a: a.reshape(bsz, length, a.shape[-1])
    yf, yb, hf, hb = _ssd(seq(xbc), seq(dt), dtt, lw["ssd_a"], h0f, h0b, q)
    if not want_out:
        return None, hf, hb
    fwd, inv = dft
    nct = HY_W // HY_CT
    hy_s = seq(hy)
    fwd_b, inv_b = fwd.astype(BF16), inv.astype(BF16)
    z1 = _hyconv(hy_s, 0, hy_s, nct, kspec, 0, lw["hy_bias"][0:1], fwd_b, inv_b, p)
    z2 = _hyconv(z1, 0, hy_s, 2 * nct, kspec, nct, lw["hy_bias"][1:2], fwd_b, inv_b, p)
    tok = lambda a: a.reshape(xs.shape[0], xs.shape[1], a.shape[-1])
    out = _outproj(xs, mod, g, tok(z2), tok(yf), tok(yb), xbc, z, sc, lw, tm)
    return out, hf, hb


def _layer_weights(l, w_in, w_out, hy_conv_w, hy_conv_b, hy_bias, ssd_conv_w, ssd_conv_b, ssd_a_log, ssd_dt_bias,
                   ssd_d, sc_conv_w, mix_gain):
    o_dt = HY_IN + SSD_W + SSD_XBC
    wl = w_in[l]
    w_dt = wl[:, o_dt:o_dt + N_DT].astype(BF16)
    dt_bias = ssd_dt_bias[l].reshape(1, N_DT)
    return {
        "w_in": jnp.concatenate([wl[:, :o_dt], wl[:, o_dt + N_DT:]], axis=1).astype(BF16),
        "w_dt": w_dt,
        "w_dtt": w_dt.T,
        "hy_conv_w": hy_conv_w[l],
        "hy_conv_b": hy_conv_b[l][None],
        "ssd_conv_w": ssd_conv_w[l],
        "ssd_conv_b": ssd_conv_b[l][None],
        "sc_conv_w": sc_conv_w[l],
        "sc_gain": mix_gain[l][None, HY_W + SSD_W:],
        "dt_bias": dt_bias,
        "dt_bias_t": dt_bias.reshape(N_DT, 1),
        "hy_bias": hy_bias[l],
        "ssd_a": -jnp.exp(ssd_a_log[l]),
        "d_exp": jnp.repeat(ssd_d[l], SSD_HEAD_DIM)[None],
        "mix_gain": mix_gain[l][None],
        "w_out_hy": w_out[l, :HY_W].astype(BF16),
        "w_out_ssd": w_out[l, HY_W:HY_W + SSD_W].astype(BF16),
        "w_out_sc": w_out[l, HY_W + SSD_W:].astype(BF16),
    }


def _pick(n, prefs):
    for v in prefs:
        if n % v == 0:
            return v
    return n


def kernel(x, c, ctx, c_ctx, w_mod, b_mod, norm_g, ffn_w_in, ffn_w_out, w_in, w_out, hy_conv_w, hy_conv_b, hy_fw1,
           hy_fb1, hy_fw2, hy_fb2, hy_fw3, hy_fb3, hy_fw4, hy_freq, hy_bias, ssd_conv_w, ssd_conv_b, ssd_a_log,
           ssd_dt_bias, ssd_d, sc_conv_w, mix_gain):
    bsz, n_lat, d = x.shape
    _, ctx_len, _ = ctx.shape
    depth = w_mod.shape[0]
    dff = ffn_w_out.shape[2]

    rows = -(-(bsz + 1) // 8) * 8
    cvec = jnp.zeros((rows, d), F32).at[:bsz].set(c).at[bsz].set(c_ctx)
    mod_all = _mod_table(cvec, w_mod, b_mod).reshape(depth, rows, N_MOD, d)

    tm = _pick(n_lat, (512, 256, 128, GRID_W))
    p_lat = _pick(n_lat, (512, 256, 128))
    q_lat = _pick(n_lat, (256, 128))
    xc = ctx.reshape(1, bsz * ctx_len, d)
    tm_c = ctx_len * _pick(bsz, (2, 1)) if ctx_len < 512 else _pick(ctx_len, (512, 256))
    p_ctx = _pick(ctx_len, (512, 256, 128))
    q_ctx = _pick(ctx_len, (256, 128))
    dft_lat = _dft_mats(p_lat)
    dft_ctx = _dft_mats(p_ctx)

    h_zero = jnp.zeros((bsz, SSD_GROUPS, SSD_STATE, SSD_W // SSD_GROUPS), F32)
    for l in range(depth):
        last = l == depth - 1
        mod_x = mod_all[l, :bsz]
        mod_c = mod_all[l, bsz:bsz + 1]
        g = norm_g[l]
        lw = _layer_weights(l, w_in, w_out, hy_conv_w, hy_conv_b, hy_bias, ssd_conv_w, ssd_conv_b, ssd_a_log,
                            ssd_dt_bias, ssd_d, sc_conv_w, mix_gain)
        fw = {"fw1": hy_fw1[l], "fb1": hy_fb1[l], "fw2": hy_fw2[l], "fb2": hy_fb2[l], "fw3": hy_fw3[l],
              "fb3": hy_fb3[l], "fw4": hy_fw4[l], "freq": hy_freq[l]}
        ffn_w = [(ffn_w_in[l, i, :, :dff].astype(BF16), ffn_w_in[l, i, :, dff:].astype(BF16),
                  ffn_w_out[l, i].astype(BF16)) for i in range(2)]

        x = _ffn(x, mod_x, g, *ffn_w[0], 0, tm)
        xc = _ffn(xc, mod_c, g, *ffn_w[0], 0, tm_c)

        ks_ctx = None
        if not last:
            ks_ctx = _hyena_spectra(_hyena_filter(fw, ctx_len, p_ctx), dft_ctx[0], p_ctx)
        y_c, h_f, h_b = _mixer(xc, mod_c, g, lw, ks_ctx, dft_ctx, ctx_len, tm_c, p_ctx, q_ctx, h_zero, h_zero,
                               (bsz, ctx_len), not last)
        ks_lat = _hyena_spectra(_hyena_filter(fw, n_lat, p_lat), dft_lat[0], p_lat)
        x, _, _ = _mixer(x, mod_x, g, lw, ks_lat, dft_lat, GRID_W, tm, p_lat, q_lat, h_f, h_b, (bsz, n_lat), True)
        x = _ffn(x, mod_x, g, *ffn_w[1], 2, tm)
        if not last:
            xc = _ffn(y_c, mod_c, g, *ffn_w[1], 2, tm_c)
    return x
```

```python
import functools
import math

import jax
import jax.numpy as jnp
import numpy as np
from jax import lax
from jax.experimental import pallas as pl
from jax.experimental.pallas import tpu as pltpu

GRID_W = 64
EPS = 1e-6
FFN_RES_W = 0.5
N_MOD = 9
HY_W = 256
SSD_HEADS = 8
SSD_HEAD_DIM = 64
SSD_GROUPS = 2
SSD_STATE = 128
SC_W = 256
SHORT_K = 3
HY_ORDER = 2
HY_EMB = 33
HY_FAST = 0.3
HY_SLOW = 1.5
HY_TARGET = 1e-2

SSD_W = SSD_HEADS * SSD_HEAD_DIM
HY_IN = 3 * HY_W
SSD_XBC = SSD_W + 2 * SSD_GROUPS * SSD_STATE
SC_IN = 3 * SC_W
N_DT = 2 * SSD_HEADS

LANE = 128
VMEM_LIMIT = 56 * 1024 * 1024

F32 = jnp.float32
BF16 = jnp.bfloat16


def _cparams(sem):
    return pltpu.CompilerParams(dimension_semantics=sem, vmem_limit_bytes=VMEM_LIMIT)


def _const_spec(shape):
    nd = len(shape)
    return pl.BlockSpec(shape, lambda *_: (0,) * nd, pipeline_mode=pl.Buffered(1))


def _dot(a, b):
    return jnp.dot(a, b, preferred_element_type=F32)


def _split2(a):
    hi = a.astype(BF16)
    lo = (a - hi.astype(F32)).astype(BF16)
    return hi, lo


def _split3(a):
    hi = a.astype(BF16)
    r = a - hi.astype(F32)
    mid = r.astype(BF16)
    lo = (r - mid.astype(F32)).astype(BF16)
    return hi, mid, lo


def _dot_hp(a, b):
    ah, al = _split2(a)
    bh, bl = _split2(b)
    return _dot(ah, bh) + (_dot(ah, bl) + _dot(al, bh))


def _silu(x):
    return x / (1.0 + jnp.exp(-x))


def _rms(x, g):
    ms = jnp.mean(x * x, axis=-1, keepdims=True)
    return x * lax.rsqrt(ms + EPS) * g


def _mod_kernel(c_ref, w_ref, b_ref, o_ref):
    o_ref[0] = _dot_hp(_silu(c_ref[...]), w_ref[0]) + b_ref[0]


def _mod_table(cvec, w_mod, b_mod):
    depth, d, nm = w_mod.shape
    r = cvec.shape[0]
    tn = 1024 if nm % 1024 == 0 else d
    return pl.pallas_call(
        _mod_kernel,
        grid=(depth, nm // tn),
        in_specs=[
            pl.BlockSpec((r, d), lambda l, n: (0, 0)),
            pl.BlockSpec((1, d, tn), lambda l, n: (l, 0, n)),
            pl.BlockSpec((1, 1, tn), lambda l, n: (l, 0, n)),
        ],
        out_specs=pl.BlockSpec((1, r, tn), lambda l, n: (l, 0, n)),
        out_shape=jax.ShapeDtypeStruct((depth, r, nm), F32),
        compiler_params=_cparams(("parallel", "parallel")),
        name="mod_table",
    )(cvec, w_mod, b_mod.reshape(depth, 1, nm))


def _ffn_kernel(x_ref, mod_ref, g_ref, wg_ref, wu_ref, wo_ref, o_ref, *, sub, n_chunk):
    x = x_ref[0]
    mod = mod_ref[0]
    g = g_ref[...]
    shift, scale, gate = mod[3 * sub:3 * sub + 1], mod[3 * sub + 1:3 * sub + 2], mod[3 * sub + 2:3 * sub + 3]
    g_pre, g_post = g[2 * sub:2 * sub + 1], g[2 * sub + 1:2 * sub + 2]
    h = (_rms(x, g_pre) * (1.0 + scale) + shift).astype(BF16)
    dff = wg_ref.shape[1]
    ck = dff // n_chunk
    acc = None
    for c in range(n_chunk):
        gg = _dot(h, wg_ref[:, c * ck:(c + 1) * ck])
        uu = _dot(h, wu_ref[:, c * ck:(c + 1) * ck])
        a = (_silu(gg) * uu).astype(BF16)
        part = _dot(a, wo_ref[c * ck:(c + 1) * ck, :])
        acc = part if acc is None else acc + part
    o_ref[0] = x + FFN_RES_W * gate * _rms(acc, g_post)


def _ffn(x, mod, g, wg, wu, wo, sub, tm):
    bsz, length, d = x.shape
    dff = wg.shape[1]
    n_chunk = 2 if dff % (2 * LANE) == 0 else 1
    per_batch = mod.shape[0] != 1
    return pl.pallas_call(
        functools.partial(_ffn_kernel, sub=sub, n_chunk=n_chunk),
        grid=(bsz, length // tm),
        in_specs=[
            pl.BlockSpec((1, tm, d), lambda b, t: (b, t, 0)),
            pl.BlockSpec((1, N_MOD, d), (lambda b, t: (b, 0, 0)) if per_batch else (lambda b, t: (0, 0, 0))),
            _const_spec(g.shape),
            _const_spec(wg.shape),
            _const_spec(wu.shape),
            _const_spec(wo.shape),
        ],
        out_specs=pl.BlockSpec((1, tm, d), lambda b, t: (b, t, 0)),
        out_shape=jax.ShapeDtypeStruct(x.shape, F32),
        compiler_params=_cparams(("parallel", "parallel")),
        name="ffn",
    )(x, mod, g, wg, wu, wo)


def _conv3(a, w, seg):
    n = a.shape[0]
    row = lax.broadcasted_iota(jnp.int32, a.shape, 0) % seg
    prev = jnp.where(row == 0, 0.0, pltpu.roll(a, 1, 0))
    nxt = jnp.where(row == seg - 1, 0.0, pltpu.roll(a, n - 1, 0))
    return prev * w[0:1] + a * w[1:2] + nxt * w[2:3]


def _chunk_scan(da, q, axis):
    n = da.shape[axis]
    pos = lax.broadcasted_iota(jnp.int32, da.shape, axis) % q
    head = lax.broadcasted_iota(jnp.int32, da.shape, 1 - axis)
    up, down = da, da
    s = 1
    while s < q:
        up = up + jnp.where(pos >= s, pltpu.roll(up, s, axis), 0.0)
        down = down + jnp.where(pos < q - s, pltpu.roll(down, n - s, axis), 0.0)
        s *= 2
    return jnp.where(head < SSD_HEADS, up, down)


def _inproj_kernel(x_ref, mod_ref, g_ref, w_ref, wdt_ref, wdtt_ref, hyw_ref, hyb_ref, sw_ref, sb_ref, scw_ref,
                   scg_ref, dtb_ref, dtbt_ref, a2r_ref, a2c_ref, hy_ref, z_ref, xbc_ref, sc_ref, dta_ref, dtat_ref,
                   *, seg, q):
    x = x_ref[0]
    mod = mod_ref[0]
    h = (_rms(x, g_ref[2:3]) * (1.0 + mod[4:5]) + mod[3:4]).astype(BF16)
    o_z, o_xbc, o_sc = HY_IN, HY_IN + SSD_W, HY_IN + SSD_W + SSD_XBC
    hy_ref[0] = _conv3(_dot(h, w_ref[:, :o_z]), hyw_ref[...], seg) + hyb_ref[...]
    z_ref[0] = _dot(h, w_ref[:, o_z:o_xbc]).astype(BF16)
    xbc = _conv3(_dot(h, w_ref[:, o_xbc:o_sc]), sw_ref[...], seg) + sb_ref[...]
    xbc_ref[0] = _silu(xbc).astype(BF16)
    gb = _dot(h, w_ref[:, o_sc:o_sc + SC_W])
    gc = _dot(h, w_ref[:, o_sc + SC_W:o_sc + 2 * SC_W])
    hx = _dot(h, w_ref[:, o_sc + 2 * SC_W:])
    sc_ref[0] = _rms(gb * _conv3(gc * hx, scw_ref[...], seg), scg_ref[...]).astype(BF16)
    dt = jax.nn.softplus(_dot(h, wdt_ref[...]) + dtb_ref[...])
    dtt = lax.dot_general(wdtt_ref[...], h, (((1,), (1,)), ((), ())), preferred_element_type=F32)
    dtt = jax.nn.softplus(dtt + dtbt_ref[...])
    dta_ref[0, :, :N_DT] = dt
    dta_ref[0, :, N_DT:] = _chunk_scan(dt * a2r_ref[...], q, 0)
    dtat_ref[0, :N_DT, :] = dtt
    dtat_ref[0, N_DT:, :] = _chunk_scan(dtt * a2c_ref[...], q, 1)


def _inproj(x, mod, g, lw, seg, tm, q):
    bsz, length, d = x.shape
    per_batch = mod.shape[0] != 1
    tok = lambda w: pl.BlockSpec((1, tm, w), lambda b, t: (b, t, 0))
    consts = (lw["w_in"], lw["w_dt"], lw["w_dtt"], lw["hy_conv_w"], lw["hy_conv_b"], lw["ssd_conv_w"],
              lw["ssd_conv_b"], lw["sc_conv_w"], lw["sc_gain"], lw["dt_bias"], lw["dt_bias_t"], lw["a2_row"],
              lw["a2_col"])
    return pl.pallas_call(
        functools.partial(_inproj_kernel, seg=seg, q=q),
        grid=(bsz, length // tm),
        in_specs=[
            tok(d),
            pl.BlockSpec((1, N_MOD, d), (lambda b, t: (b, 0, 0)) if per_batch else (lambda b, t: (0, 0, 0))),
            _const_spec(g.shape),
        ] + [_const_spec(a.shape) for a in consts],
        out_specs=[tok(HY_IN), tok(SSD_W), tok(SSD_XBC), tok(SC_W), tok(2 * N_DT),
                   pl.BlockSpec((1, 2 * N_DT, tm), lambda b, t: (b, 0, t))],
        out_shape=[
            jax.ShapeDtypeStruct((bsz, length, HY_IN), F32),
            jax.ShapeDtypeStruct((bsz, length, SSD_W), BF16),
            jax.ShapeDtypeStruct((bsz, length, SSD_XBC), BF16),
            jax.ShapeDtypeStruct((bsz, length, SC_W), BF16),
            jax.ShapeDtypeStruct((bsz, length, 2 * N_DT), F32),
            jax.ShapeDtypeStruct((bsz, 2 * N_DT, length), F32),
        ],
        compiler_params=_cparams(("parallel", "parallel")),
        name="inproj",
    )(x, mod, g, *consts)


def _dft_mats(p):
    f = np.arange(p, dtype=np.int64)[:, None]
    t = np.arange(p, dtype=np.int64)[None, :]
    ang = (np.pi / (2 * p)) * (((2 * f + 1) * t) % (4 * p)).astype(np.float64)
    fwd = np.concatenate([np.cos(ang), -np.sin(ang)], axis=0)
    return jnp.asarray(fwd, F32), jnp.asarray(fwd.T / p, F32)


def _filter_kernel(fb_ref, dl_ref, w1t_ref, w1c_ref, w1s_ref, b1_ref, w2_ref, b2_ref, w3_ref, b3_ref, w4_ref,
                   fr_ref, k_ref, *, length, tr):
    lag = pl.program_id(0) * tr + lax.broadcasted_iota(jnp.int32, (tr, 1), 0) - length
    n = jnp.abs(lag).astype(F32)
    t = n * (1.0 / (length - 1))
    w = n * (2.0 * math.pi / length)
    ang = fb_ref[...] * w
    fr = fr_ref[...]
    h = t * w1t_ref[...] + _dot_hp(jnp.cos(ang), w1c_ref[...]) - _dot_hp(jnp.sin(ang), w1s_ref[...])
    h = jnp.sin(fr * (h + b1_ref[...]))
    h = jnp.sin(fr * (_dot_hp(h, w2_ref[...]) + b2_ref[...]))
    h = jnp.sin(fr * (_dot_hp(h, w3_ref[...]) + b3_ref[...]))
    decay = jnp.exp(-t * jnp.abs(dl_ref[...]))
    for o in range(HY_ORDER):
        base = o * 2 * HY_W
        kf = _dot_hp(h, w4_ref[:, base:base + HY_W]) * decay
        kb = _dot_hp(h, w4_ref[:, base + HY_W:base + 2 * HY_W]) * decay
        k_ref[:, o * HY_W:(o + 1) * HY_W] = jnp.where(lag >= 0, kf, jnp.where(lag == -length, 0.0, kb))


def _hyena_filter(fw, length, tr):
    bands = (HY_EMB - 1) // 2
    fbands = jnp.linspace(1e-4, bands - 1, bands, dtype=F32)[None, :]
    deltas = jnp.linspace(math.log(HY_TARGET) / HY_SLOW, math.log(HY_TARGET) / HY_FAST, HY_W, dtype=F32)[None, :]
    fw1 = fw["fw1"]
    consts = (fbands, deltas, fw1[0:1], fw1[1:1 + bands], fw1[1 + bands:], fw["fb1"][None], fw["fw2"],
              fw["fb2"][None], fw["fw3"], fw["fb3"][None], fw["fw4"], fw["freq"][None])
    return pl.pallas_call(
        functools.partial(_filter_kernel, length=length, tr=tr),
        grid=(2 * length // tr,),
        in_specs=[pl.BlockSpec(a.shape, lambda i: (0, 0)) for a in consts],
        out_specs=pl.BlockSpec((tr, HY_ORDER * HY_W), lambda i: (i, 0)),
        out_shape=jax.ShapeDtypeStruct((2 * length, HY_ORDER * HY_W), F32),
        compiler_params=_cparams(("parallel",)),
        name="hyena_filter",
    )(*consts)


def _kspec_kernel(f_ref, k0_ref, k1_ref, o_ref, *, p):
    fwd = f_ref[...]
    a = _dot_hp(fwd, k1_ref[...])
    a1 = _dot_hp(fwd, k0_ref[...])
    a1_re = a1[:p] - k0_ref[0:1, :]
    a1_im = a1[p:]
    sign = 1.0 - 2.0 * (lax.broadcasted_iota(jnp.int32, (p, 1), 0) % 2).astype(F32)
    o_ref[0, :p, :] = a[:p] - sign * a1_im
    o_ref[0, p:, :] = a[p:] + sign * a1_re


def _hyena_spectra(kfull, fwd, p):
    nb = kfull.shape[0] // (2 * p)
    cols = kfull.shape[1]
    return pl.pallas_call(
        functools.partial(_kspec_kernel, p=p),
        grid=(2 * nb - 1,),
        in_specs=[
            pl.BlockSpec(fwd.shape, lambda i: (0, 0)),
            pl.BlockSpec((p, cols), lambda i: (i, 0)),
            pl.BlockSpec((p, cols), lambda i: (i + 1, 0)),
        ],
        out_specs=pl.BlockSpec((1, 2 * p, cols), lambda i: (i, 0, 0)),
        out_shape=jax.ShapeDtypeStruct((2 * nb - 1, 2 * p, cols), F32),
        compiler_params=_cparams(("parallel",)),
        name="hyena_spectra",
    )(fwd, kfull, kfull)


HY_CT = 128
HY_NB = 2
HY_RT = 32


def _hyconv_kernel(zin_ref, gate_ref, k_ref, bias_ref, fwd_ref, inv_ref, o_ref, zf_ref, y_ref, *, nb, p):
    fwd = fwd_ref[...]
    inv = inv_ref[...]
    for j in range(nb):
        zb = jnp.concatenate([zin_ref[s, j * p:(j + 1) * p, :] for s in range(HY_NB)], axis=1).astype(BF16)
        zf_ref[j] = _dot(fwd, zb)

    def block(i, carry):
        def tile(rt, carry2):
            r0 = pl.multiple_of(rt * HY_RT, HY_RT)
            acc_re = [jnp.zeros((HY_RT, HY_CT), F32) for _ in range(HY_NB)]
            acc_im = [jnp.zeros((HY_RT, HY_CT), F32) for _ in range(HY_NB)]
            for j in range(nb):
                d = i - j + (nb - 1)
                kre = k_ref[d, pl.ds(r0, HY_RT), :]
                kim = k_ref[d, pl.ds(p + r0, HY_RT), :]
                zre = zf_ref[j, pl.ds(r0, HY_RT), :]
                zim = zf_ref[j, pl.ds(p + r0, HY_RT), :]
                for s in range(HY_NB):
                    zr = zre[:, s * HY_CT:(s + 1) * HY_CT]
                    zi = zim[:, s * HY_CT:(s + 1) * HY_CT]
                    acc_re[s] = acc_re[s] + (kre * zr - kim * zi)
                    acc_im[s] = acc_im[s] + (kre * zi + kim * zr)
            y_ref[pl.ds(r0, HY_RT), :] = jnp.concatenate(acc_re, axis=1).astype(BF16)
            y_ref[pl.ds(p + r0, HY_RT), :] = jnp.concatenate(acc_im, axis=1).astype(BF16)
            return carry2

        lax.fori_loop(0, p // HY_RT, tile, 0)
        y = _dot(inv, y_ref[...])
        rows = pl.ds(pl.multiple_of(i * p, p), p)
        for s in range(HY_NB):
            zs = zin_ref[s, rows, :]
            o_ref[s, rows, :] = gate_ref[s, rows, :] * (y[:, s * HY_CT:(s + 1) * HY_CT] + bias_ref[...] * zs)
        return carry

    lax.fori_loop(0, nb, block, 0)


def _hyconv(zin, zin_col, gate, gate_col, kspec, k_col, bias, fwd, inv, p):
    bsz, length, _ = zin.shape
    nb = length // p
    nct = HY_W // HY_CT
    nd = kspec.shape[0]
    return pl.pallas_call(
        functools.partial(_hyconv_kernel, nb=nb, p=p),
        grid=(nct, bsz // HY_NB),
        in_specs=[
            pl.BlockSpec((HY_NB, length, HY_CT), lambda c, b: (b, 0, zin_col + c)),
            pl.BlockSpec((HY_NB, length, HY_CT), lambda c, b: (b, 0, gate_col + c)),
            pl.BlockSpec((nd, 2 * p, HY_CT), lambda c, b: (0, 0, k_col + c), pipeline_mode=pl.Buffered(1)),
            pl.BlockSpec((1, HY_CT), lambda c, b: (0, c)),
            _const_spec(fwd.shape),
            _const_spec(inv.shape),
        ],
        out_specs=pl.BlockSpec((HY_NB, length, HY_CT), lambda c, b: (b, 0, c)),
        out_shape=jax.ShapeDtypeStruct((bsz, length, HY_W), F32),
        scratch_shapes=[
            pltpu.VMEM((nb, 2 * p, HY_NB * HY_CT), F32),
            pltpu.VMEM((2 * p, HY_NB * HY_CT), BF16),
        ],
        compiler_params=_cparams(("arbitrary", "arbitrary")),
        name="hyconv",
    )(zin, gate, kspec, bias, fwd, inv)


def _ssd_dir(xbc_ref, dta_ref, dtat_ref, h_ref, y_ref, *, q, rev):
    n, hp, hd = SSD_STATE, SSD_HEADS // SSD_GROUPS, SSD_HEAD_DIM
    c0 = SSD_HEADS if rev else 0
    end = 0 if rev else q - 1
    li = lax.broadcasted_iota(jnp.int32, (q, q), 0)
    si = lax.broadcasted_iota(jnp.int32, (q, q), 1)
    keep = (li <= si) if rev else (li >= si)
    low = lax.broadcasted_iota(jnp.int32, (q, LANE), 1) < hd
    pairs = hp * hd // LANE
    for g in range(SSD_GROUPS):
        bg = xbc_ref[0, :, SSD_W + g * n:SSD_W + (g + 1) * n]
        cg = xbc_ref[0, :, SSD_W + SSD_GROUPS * n + g * n:SSD_W + SSD_GROUPS * n + (g + 1) * n]
        scores = lax.dot_general(cg, bg, (((1,), (1,)), ((), ())), preferred_element_type=F32)
        hg = h_ref[g]
        y_off = _dot(cg, hg.astype(BF16))
        ys, xws, tots = [], [], []
        for t in range(pairs):
            cols = slice(g * hp * hd + t * LANE, g * hp * hd + (t + 1) * LANE)
            acs_b, dt_b = [], []
            for r in (2 * t, 2 * t + 1):
                c = c0 + g * hp + r
                acs_b.append(jnp.broadcast_to(dta_ref[0, :, N_DT + c:N_DT + c + 1], (q, LANE)))
                dt_b.append(jnp.broadcast_to(dta_ref[0, :, c:c + 1], (q, LANE)))
            acs2 = jnp.where(low, acs_b[0], acs_b[1])
            dt2 = jnp.where(low, dt_b[0], dt_b[1])
            xdt = xbc_ref[0, :, cols].astype(F32) * dt2
            yd = None
            for i, r in enumerate((2 * t, 2 * t + 1)):
                c = c0 + g * hp + r
                seg = jnp.concatenate([acs_b[i]] * (q // LANE), axis=1) - dtat_ref[0, N_DT + c:N_DT + c + 1, :]
                m = (jnp.exp2(jnp.where(keep, seg, -1e30)) * scores).astype(BF16)
                xh = jnp.where(low if i == 0 else jnp.logical_not(low), xdt, 0.0).astype(BF16)
                part = _dot(m, xh)
                yd = part if yd is None else yd + part
            ys.append(yd + jnp.exp2(acs2) * y_off[:, t * LANE:(t + 1) * LANE])
            tot = acs2[end:end + 1, :]
            xws.append((xdt * jnp.exp2(tot - acs2)).astype(BF16))
            tots.append(tot)
        y_ref[0, :, g * hp * hd:(g + 1) * hp * hd] = jnp.concatenate(ys, axis=1).astype(y_ref.dtype)
        upd = lax.dot_general(bg, jnp.concatenate(xws, axis=1), (((0,), (0,)), ((), ())),
                              preferred_element_type=F32)
        h_ref[g] = hg * jnp.exp2(jnp.concatenate(tots, axis=1)) + upd


def _ssd_kernel(xf_ref, dtaf_ref, dtatf_ref, xb_ref, dtab_ref, dtatb_ref, h0f_ref, h0b_ref,
                yf_ref, yb_ref, hf_out, hb_out, hf_ref, hb_ref, *, q):
    k = pl.program_id(1)

    @pl.when(k == 0)
    def _():
        hf_ref[...] = h0f_ref[0]
        hb_ref[...] = h0b_ref[0]

    _ssd_dir(xf_ref, dtaf_ref, dtatf_ref, hf_ref, yf_ref, q=q, rev=False)
    _ssd_dir(xb_ref, dtab_ref, dtatb_ref, hb_ref, yb_ref, q=q, rev=True)

    @pl.when(k == pl.num_programs(1) - 1)
    def _():
        hf_out[0] = hf_ref[...]
        hb_out[0] = hb_ref[...]


def _ssd(xbc, dta, dtat, h0f, h0b, q):
    bsz, length, _ = xbc.shape
    nc = length // q
    hshape = (SSD_GROUPS, SSD_STATE, SSD_W // SSD_GROUPS)
    fw = lambda b, k: (b, k, 0)
    bw = lambda b, k: (b, nc - 1 - k, 0)
    st = pl.BlockSpec((1,) + hshape, lambda b, k: (b, 0, 0, 0))
    return pl.pallas_call(
        functools.partial(_ssd_kernel, q=q),
        grid=(bsz, nc),
        in_specs=[
            pl.BlockSpec((1, q, SSD_XBC), fw),
            pl.BlockSpec((1, q, 2 * N_DT), fw),
            pl.BlockSpec((1, 2 * N_DT, q), lambda b, k: (b, 0, k)),
            pl.BlockSpec((1, q, SSD_XBC), bw),
            pl.BlockSpec((1, q, 2 * N_DT), bw),
            pl.BlockSpec((1, 2 * N_DT, q), lambda b, k: (b, 0, nc - 1 - k)),
            st, st,
        ],
        out_specs=[pl.BlockSpec((1, q, SSD_W), fw), pl.BlockSpec((1, q, SSD_W), bw), st, st],
        out_shape=[
            jax.ShapeDtypeStruct((bsz, length, SSD_W), BF16),
            jax.ShapeDtypeStruct((bsz, length, SSD_W), BF16),
            jax.ShapeDtypeStruct((bsz,) + hshape, F32),
            jax.ShapeDtypeStruct((bsz,) + hshape, F32),
        ],
        scratch_shapes=[pltpu.VMEM(hshape, F32), pltpu.VMEM(hshape, F32)],
        compiler_params=_cparams(("parallel", "arbitrary")),
        name="ssd",
    )(xbc, dta, dtat, xbc, dta, dtat, h0f, h0b)


def _outproj_kernel(x_ref, mod_ref, g_ref, hy_ref, yf_ref, yb_ref, xs_ref, z_ref, sc_ref, dexp_ref, mg_ref,
                    why_ref, wssd_ref, wsc_ref, o_ref):
    x = x_ref[0]
    mod = mod_ref[0]
    mg = mg_ref[...]
    y_hy = _rms(hy_ref[0], mg[:, :HY_W]).astype(BF16)
    y = yf_ref[0].astype(F32) + yb_ref[0].astype(F32) + dexp_ref[...] * xs_ref[0].astype(F32)
    y_ssd = _rms(y * _silu(z_ref[0].astype(F32)), mg[:, HY_W:HY_W + SSD_W]).astype(BF16)
    o = _dot(y_hy, why_ref[...]) + _dot(y_ssd, wssd_ref[...]) + _dot(sc_ref[0], wsc_ref[...])
    o_ref[0] = x + mod[5:6] * _rms(o, g_ref[3:4])


def _outproj(x, mod, g, hy, yf, yb, xbc, z, sc, lw, tm):
    bsz, length, d = x.shape
    per_batch = mod.shape[0] != 1
    tok = lambda w: pl.BlockSpec((1, tm, w), lambda b, t: (b, t, 0))
    consts = (lw["d_exp"], lw["mix_gain"], lw["w_out_hy"], lw["w_out_ssd"], lw["w_out_sc"])
    return pl.pallas_call(
        _outproj_kernel,
        grid=(bsz, length // tm),
        in_specs=[
            tok(d),
            pl.BlockSpec((1, N_MOD, d), (lambda b, t: (b, 0, 0)) if per_batch else (lambda b, t: (0, 0, 0))),
            _const_spec(g.shape),
            tok(HY_W), tok(SSD_W), tok(SSD_W), tok(SSD_W), tok(SSD_W), tok(SC_W),
        ] + [_const_spec(a.shape) for a in consts],
        out_specs=tok(d),
        out_shape=jax.ShapeDtypeStruct(x.shape, F32),
        compiler_params=_cparams(("parallel", "parallel")),
        name="outproj",
    )(x, mod, g, hy, yf, yb, xbc, z, sc, *consts)


def _mixer(xs, mod, g, lw, kspec, dft, seg, tm, p, q, h0f, h0b, batch_shape, want_out):
    bsz, length = batch_shape
    hy, z, xbc, sc, dta, dtat = _inproj(xs, mod, g, lw, seg, tm, q)
    if xs.shape[0] != bsz:
        dtat = dtat.reshape(2 * N_DT, bsz, length).transpose(1, 0, 2)
    seq = lambda a: a.reshape(bsz, length, a.shape[-1])
    yf, yb, hf, hb = _ssd(seq(xbc), seq(dta), dtat, h0f, h0b, q)
    if not want_out:
        return None, hf, hb
    fwd, inv = dft
    nct = HY_W // HY_CT
    hy_s = seq(hy)
    fwd_b, inv_b = fwd.astype(BF16), inv.astype(BF16)
    z1 = _hyconv(hy_s, 0, hy_s, nct, kspec, 0, lw["hy_bias"][0:1], fwd_b, inv_b, p)
    z2 = _hyconv(z1, 0, hy_s, 2 * nct, kspec, nct, lw["hy_bias"][1:2], fwd_b, inv_b, p)
    tok = lambda a: a.reshape(xs.shape[0], xs.shape[1], a.shape[-1])
    out = _outproj(xs, mod, g, tok(z2), tok(yf), tok(yb), xbc, z, sc, lw, tm)
    return out, hf, hb


def _layer_weights(l, w_in, w_out, hy_conv_w, hy_conv_b, hy_bias, ssd_conv_w, ssd_conv_b, ssd_a_log, ssd_dt_bias,
                   ssd_d, sc_conv_w, mix_gain):
    o_dt = HY_IN + SSD_W + SSD_XBC
    wl = w_in[l]
    w_dt = wl[:, o_dt:o_dt + N_DT].astype(BF16)
    dt_bias = ssd_dt_bias[l].reshape(1, N_DT)
    return {
        "w_in": jnp.concatenate([wl[:, :o_dt], wl[:, o_dt + N_DT:]], axis=1).astype(BF16),
        "w_dt": w_dt,
        "w_dtt": w_dt.T,
        "hy_conv_w": hy_conv_w[l],
        "hy_conv_b": hy_conv_b[l][None],
        "ssd_conv_w": ssd_conv_w[l],
        "ssd_conv_b": ssd_conv_b[l][None],
        "sc_conv_w": sc_conv_w[l],
        "sc_gain": mix_gain[l][None, HY_W + SSD_W:],
        "dt_bias": dt_bias,
        "dt_bias_t": dt_bias.reshape(N_DT, 1),
        "hy_bias": hy_bias[l],
        "a2_row": (-jnp.exp(ssd_a_log[l]) * math.log2(math.e)).reshape(1, N_DT),
        "a2_col": (-jnp.exp(ssd_a_log[l]) * math.log2(math.e)).reshape(N_DT, 1),
        "d_exp": jnp.repeat(ssd_d[l], SSD_HEAD_DIM)[None],
        "mix_gain": mix_gain[l][None],
        "w_out_hy": w_out[l, :HY_W].astype(BF16),
        "w_out_ssd": w_out[l, HY_W:HY_W + SSD_W].astype(BF16),
        "w_out_sc": w_out[l, HY_W + SSD_W:].astype(BF16),
    }


def _pick(n, prefs):
    for v in prefs:
        if n % v == 0:
            return v
    return n


def kernel(x, c, ctx, c_ctx, w_mod, b_mod, norm_g, ffn_w_in, ffn_w_out, w_in, w_out, hy_conv_w, hy_conv_b, hy_fw1,
           hy_fb1, hy_fw2, hy_fb2, hy_fw3, hy_fb3, hy_fw4, hy_freq, hy_bias, ssd_conv_w, ssd_conv_b, ssd_a_log,
           ssd_dt_bias, ssd_d, sc_conv_w, mix_gain):
    bsz, n_lat, d = x.shape
    _, ctx_len, _ = ctx.shape
    depth = w_mod.shape[0]
    dff = ffn_w_out.shape[2]

    rows = -(-(bsz + 1) // 8) * 8
    cvec = jnp.zeros((rows, d), F32).at[:bsz].set(c).at[bsz].set(c_ctx)
    mod_all = _mod_table(cvec, w_mod, b_mod).reshape(depth, rows, N_MOD, d)

    tm = _pick(n_lat, (512, 256, 128, GRID_W))
    p_lat = _pick(n_lat, (512, 256, 128))
    q_lat = _pick(n_lat, (256, 128))
    xc = ctx.reshape(1, bsz * ctx_len, d)
    tm_c = ctx_len * _pick(bsz, (2, 1)) if ctx_len < 512 else _pick(ctx_len, (512, 256))
    p_ctx = _pick(ctx_len, (512, 256, 128))
    q_ctx = _pick(ctx_len, (256, 128))
    dft_lat = _dft_mats(p_lat)
    dft_ctx = _dft_mats(p_ctx)

    h_zero = jnp.zeros((bsz, SSD_GROUPS, SSD_STATE, SSD_W // SSD_GROUPS), F32)
    for l in range(depth):
        last = l == depth - 1
        mod_x = mod_all[l, :bsz]
        mod_c = mod_all[l, bsz:bsz + 1]
        g = norm_g[l]
        lw = _layer_weights(l, w_in, w_out, hy_conv_w, hy_conv_b, hy_bias, ssd_conv_w, ssd_conv_b, ssd_a_log,
                            ssd_dt_bias, ssd_d, sc_conv_w, mix_gain)
        fw = {"fw1": hy_fw1[l], "fb1": hy_fb1[l], "fw2": hy_fw2[l], "fb2": hy_fb2[l], "fw3": hy_fw3[l],
              "fb3": hy_fb3[l], "fw4": hy_fw4[l], "freq": hy_freq[l]}
        ffn_w = [(ffn_w_in[l, i, :, :dff].astype(BF16), ffn_w_in[l, i, :, dff:].astype(BF16),
                  ffn_w_out[l, i].astype(BF16)) for i in range(2)]

        x = _ffn(x, mod_x, g, *ffn_w[0], 0, tm)
        xc = _ffn(xc, mod_c, g, *ffn_w[0], 0, tm_c)

        ks_ctx = None
        if not last:
            ks_ctx = _hyena_spectra(_hyena_filter(fw, ctx_len, p_ctx), dft_ctx[0], p_ctx)
        y_c, h_f, h_b = _mixer(xc, mod_c, g, lw, ks_ctx, dft_ctx, ctx_len, tm_c, p_ctx, q_ctx, h_zero, h_zero,
                               (bsz, ctx_len), not last)
        ks_lat = _hyena_spectra(_hyena_filter(fw, n_lat, p_lat), dft_lat[0], p_lat)
        x, _, _ = _mixer(x, mod_x, g, lw, ks_lat, dft_lat, GRID_W, tm, p_lat, q_lat, h_f, h_b, (bsz, n_lat), True)
        x = _ffn(x, mod_x, g, *ffn_w[1], 2, tm)
        if not last:
            xc = _ffn(y_c, mod_c, g, *ffn_w[1], 2, tm_c)
    return x
```

```python
import functools
import math

import jax
import jax.numpy as jnp
import numpy as np
from jax import lax
from jax.experimental import pallas as pl
from jax.experimental.pallas import tpu as pltpu

GRID_W = 64
EPS = 1e-6
FFN_RES_W = 0.5
N_MOD = 9
HY_W = 256
SSD_HEADS = 8
SSD_HEAD_DIM = 64
SSD_GROUPS = 2
SSD_STATE = 128
SC_W = 256
SHORT_K = 3
HY_ORDER = 2
HY_EMB = 33
HY_FAST = 0.3
HY_SLOW = 1.5
HY_TARGET = 1e-2

SSD_W = SSD_HEADS * SSD_HEAD_DIM
HY_IN = 3 * HY_W
SSD_XBC = SSD_W + 2 * SSD_GROUPS * SSD_STATE
SC_IN = 3 * SC_W
N_DT = 2 * SSD_HEADS

LANE = 128
SUBLANE = 8
MXU_TILE = 256
FFN_CHUNK = 3 * MXU_TILE
INPROJ_CHUNK = 2 * MXU_TILE
VMEM_LIMIT = 56 * 1024 * 1024

F32 = jnp.float32
BF16 = jnp.bfloat16


def _cparams(sem):
    return pltpu.CompilerParams(dimension_semantics=sem, vmem_limit_bytes=VMEM_LIMIT)


def _const_spec(shape):
    nd = len(shape)
    return pl.BlockSpec(shape, lambda *_: (0,) * nd, pipeline_mode=pl.Buffered(1))


def _dot(a, b):
    return jnp.dot(a, b, preferred_element_type=F32)


def _split2(a):
    hi = a.astype(BF16)
    lo = (a - hi.astype(F32)).astype(BF16)
    return hi, lo


def _split3(a):
    hi = a.astype(BF16)
    r = a - hi.astype(F32)
    mid = r.astype(BF16)
    lo = (r - mid.astype(F32)).astype(BF16)
    return hi, mid, lo


def _dot_hp(a, b):
    ah, al = _split2(a)
    bh, bl = _split2(b)
    return _dot(ah, bh) + (_dot(ah, bl) + _dot(al, bh))


def _silu(x):
    hx = 0.5 * x
    return hx + hx * jnp.tanh(hx)


def _rms(x, g):
    ms = jnp.mean(x * x, axis=-1, keepdims=True)
    return x * (lax.rsqrt(ms + EPS) * g)


def _mod_kernel(c_ref, w_ref, b_ref, o_ref):
    o_ref[0] = _dot_hp(_silu(c_ref[...]), w_ref[0]) + b_ref[0]


def _mod_table(cvec, w_mod, b_mod):
    depth, d, nm = w_mod.shape
    r = cvec.shape[0]
    tn = 1024 if nm % 1024 == 0 else d
    return pl.pallas_call(
        _mod_kernel,
        grid=(depth, nm // tn),
        in_specs=[
            pl.BlockSpec((r, d), lambda l, n: (0, 0)),
            pl.BlockSpec((1, d, tn), lambda l, n: (l, 0, n)),
            pl.BlockSpec((1, 1, tn), lambda l, n: (l, 0, n)),
        ],
        out_specs=pl.BlockSpec((1, r, tn), lambda l, n: (l, 0, n)),
        out_shape=jax.ShapeDtypeStruct((depth, r, nm), F32),
        compiler_params=_cparams(("parallel", "parallel")),
        name="mod_table",
    )(cvec, w_mod, b_mod.reshape(depth, 1, nm))


def _ffn_kernel(x_ref, mod_ref, g_ref, wg_ref, wu_ref, wo_ref, o_ref, *, sub, bounds):
    x = x_ref[0]
    mod = mod_ref[0]
    g = g_ref[...]
    shift, scale, gate = mod[3 * sub:3 * sub + 1], mod[3 * sub + 1:3 * sub + 2], mod[3 * sub + 2:3 * sub + 3]
    g_pre, g_post = g[2 * sub:2 * sub + 1], g[2 * sub + 1:2 * sub + 2]
    h = (_rms(x, g_pre * (1.0 + scale)) + shift).astype(BF16)
    acc = None
    for lo, hi in zip(bounds[:-1], bounds[1:]):
        gg = _dot(h, wg_ref[:, lo:hi])
        uu = _dot(h, wu_ref[:, lo:hi])
        a = (_silu(gg) * uu).astype(BF16)
        part = _dot(a, wo_ref[lo:hi, :])
        acc = part if acc is None else acc + part
    o_ref[0] = x + _rms(acc, FFN_RES_W * gate * g_post)


def _ffn_bounds(dff, width):
    return tuple(range(0, dff, width)) + (dff,)


def _ffn(x, mod, g, wg, wu, wo, sub, tm, width=FFN_CHUNK):
    bsz, length, d = x.shape
    dff = wg.shape[1]
    per_batch = mod.shape[0] != 1
    return pl.pallas_call(
        functools.partial(_ffn_kernel, sub=sub, bounds=_ffn_bounds(dff, width)),
        grid=(bsz, length // tm),
        in_specs=[
            pl.BlockSpec((1, tm, d), lambda b, t: (b, t, 0)),
            pl.BlockSpec((1, N_MOD, d), (lambda b, t: (b, 0, 0)) if per_batch else (lambda b, t: (0, 0, 0))),
            _const_spec(g.shape),
            _const_spec(wg.shape),
            _const_spec(wu.shape),
            _const_spec(wo.shape),
        ],
        out_specs=pl.BlockSpec((1, tm, d), lambda b, t: (b, t, 0)),
        out_shape=jax.ShapeDtypeStruct(x.shape, F32),
        compiler_params=_cparams(("parallel", "parallel")),
        name="ffn",
    )(x, mod, g, wg, wu, wo)


def _conv3(a, w, seg):
    n, c = a.shape
    a4 = a.reshape(n // seg, seg // SUBLANE, SUBLANE, c)
    sub = lax.broadcasted_iota(jnp.int32, a4.shape, 2)
    edge = jnp.zeros((n // seg, 1, SUBLANE, c), F32)
    down = pltpu.roll(a4, 1, 2)
    prev = jnp.where(sub == 0, jnp.concatenate([edge, down[:, :-1]], axis=1), down)
    up = pltpu.roll(a4, SUBLANE - 1, 2)
    nxt = jnp.where(sub == SUBLANE - 1, jnp.concatenate([up[:, 1:], edge], axis=1), up)
    return (prev.reshape(n, c) * w[0:1] + a * w[1:2]) + nxt.reshape(n, c) * w[2:3]


def _chunk_scan(da, q, axis):
    n = da.shape[axis]
    pos = lax.broadcasted_iota(jnp.int32, da.shape, axis) % q
    head = lax.broadcasted_iota(jnp.int32, da.shape, 1 - axis)
    up, down = da, da
    s = 1
    while s < q:
        up = up + jnp.where(pos >= s, pltpu.roll(up, s, axis), 0.0)
        down = down + jnp.where(pos < q - s, pltpu.roll(down, n - s, axis), 0.0)
        s *= 2
    return jnp.where(head < SSD_HEADS, up, down)


def _inproj_kernel(x_ref, mod_ref, g_ref, w_ref, wdt_ref, wdtt_ref, hyw_ref, hyb_ref, sw_ref, sb_ref, scw_ref,
                   scg_ref, dtb_ref, dtbt_ref, a2r_ref, a2c_ref, hy_ref, z_ref, xbc_ref, sc_ref, dta_ref, dtat_ref,
                   *, seg, q):
    x = x_ref[0]
    mod = mod_ref[0]
    h = (_rms(x, g_ref[2:3] * (1.0 + mod[4:5])) + mod[3:4]).astype(BF16)
    o_z, o_xbc, o_sc = HY_IN, HY_IN + SSD_W, HY_IN + SSD_W + SSD_XBC

    def post_hy(p, lo, hi):
        hy_ref[0, :, lo:hi] = _conv3(p, hyw_ref[:, lo:hi], seg) + hyb_ref[:, lo:hi]

    def post_z(p, lo, hi):
        z_ref[0, :, lo - o_z:hi - o_z] = p.astype(BF16)

    def post_xbc(p, lo, hi):
        lo, hi = lo - o_xbc, hi - o_xbc
        xbc_ref[0, :, lo:hi] = _silu(_conv3(p, sw_ref[:, lo:hi], seg) + sb_ref[:, lo:hi]).astype(BF16)

    def post_sc(p, lo, hi):
        gb, gc, hx = p[:, :SC_W], p[:, SC_W:2 * SC_W], p[:, 2 * SC_W:]
        sc_ref[0] = _rms(gb * _conv3(gc * hx, scw_ref[...], seg), scg_ref[...]).astype(BF16)

    stages = []
    for post, lo, hi in ((post_hy, 0, o_z), (post_z, o_z, o_xbc), (post_xbc, o_xbc, o_sc)):
        stages += [(post, c, min(c + INPROJ_CHUNK, hi)) for c in range(lo, hi, INPROJ_CHUNK)]
    stages.append((post_sc, o_sc, o_sc + SC_IN))
    pending = None
    for post, lo, hi in stages:
        p = _dot(h, w_ref[:, lo:hi])
        if pending is not None:
            pending[0](*pending[1:])
        pending = (post, p, lo, hi)
    pending[0](*pending[1:])
    dt = jax.nn.softplus(_dot(h, wdt_ref[...]) + dtb_ref[...])
    dtt = lax.dot_general(wdtt_ref[...], h, (((1,), (1,)), ((), ())), preferred_element_type=F32)
    dtt = jax.nn.softplus(dtt + dtbt_ref[...])
    dta_ref[0, :, :N_DT] = dt
    dta_ref[0, :, N_DT:] = _chunk_scan(dt * a2r_ref[...], q, 0)
    dtat_ref[0, :N_DT, :] = dtt
    dtat_ref[0, N_DT:, :] = _chunk_scan(dtt * a2c_ref[...], q, 1)


def _inproj(x, mod, g, lw, seg, tm, q):
    bsz, length, d = x.shape
    per_batch = mod.shape[0] != 1
    tok = lambda w: pl.BlockSpec((1, tm, w), lambda b, t: (b, t, 0))
    consts = (lw["w_in"], lw["w_dt"], lw["w_dtt"], lw["hy_conv_w"], lw["hy_conv_b"], lw["ssd_conv_w"],
              lw["ssd_conv_b"], lw["sc_conv_w"], lw["sc_gain"], lw["dt_bias"], lw["dt_bias_t"], lw["a2_row"],
              lw["a2_col"])
    return pl.pallas_call(
        functools.partial(_inproj_kernel, seg=seg, q=q),
        grid=(bsz, length // tm),
        in_specs=[
            tok(d),
            pl.BlockSpec((1, N_MOD, d), (lambda b, t: (b, 0, 0)) if per_batch else (lambda b, t: (0, 0, 0))),
            _const_spec(g.shape),
        ] + [_const_spec(a.shape) for a in consts],
        out_specs=[tok(HY_IN), tok(SSD_W), tok(SSD_XBC), tok(SC_W), tok(2 * N_DT),
                   pl.BlockSpec((1, 2 * N_DT, tm), lambda b, t: (b, 0, t))],
        out_shape=[
            jax.ShapeDtypeStruct((bsz, length, HY_IN), F32),
            jax.ShapeDtypeStruct((bsz, length, SSD_W), BF16),
            jax.ShapeDtypeStruct((bsz, length, SSD_XBC), BF16),
            jax.ShapeDtypeStruct((bsz, length, SC_W), BF16),
            jax.ShapeDtypeStruct((bsz, length, 2 * N_DT), F32),
            jax.ShapeDtypeStruct((bsz, 2 * N_DT, length), F32),
        ],
        compiler_params=_cparams(("parallel", "parallel")),
        name="inproj",
    )(x, mod, g, *consts)


def _dft_mats(p):
    f = np.arange(p, dtype=np.int64)[:, None]
    t = np.arange(p, dtype=np.int64)[None, :]
    ang = (np.pi / (2 * p)) * (((2 * f + 1) * t) % (4 * p)).astype(np.float64)
    fwd = np.concatenate([np.cos(ang), -np.sin(ang)], axis=0)
    return jnp.asarray(fwd, F32), jnp.asarray(fwd.T / p, F32)


def _filter_kernel(fb_ref, dl_ref, w1t_ref, w1c_ref, w1s_ref, b1_ref, w2_ref, b2_ref, w3_ref, b3_ref, w4_ref,
                   fr_ref, k_ref, *, length, tr):
    lag = pl.program_id(0) * tr + lax.broadcasted_iota(jnp.int32, (tr, 1), 0) - length
    n = jnp.abs(lag).astype(F32)
    t = n * (1.0 / (length - 1))
    w = n * (2.0 * math.pi / length)
    ang = fb_ref[...] * w
    fr = fr_ref[...]
    h = t * w1t_ref[...] + _dot_hp(jnp.cos(ang), w1c_ref[...]) - _dot_hp(jnp.sin(ang), w1s_ref[...])
    h = jnp.sin(fr * (h + b1_ref[...]))
    h = jnp.sin(fr * (_dot_hp(h, w2_ref[...]) + b2_ref[...]))
    h = jnp.sin(fr * (_dot_hp(h, w3_ref[...]) + b3_ref[...]))
    decay = jnp.exp(-t * jnp.abs(dl_ref[...]))
    for o in range(HY_ORDER):
        base = o * 2 * HY_W
        kf = _dot_hp(h, w4_ref[:, base:base + HY_W]) * decay
        kb = _dot_hp(h, w4_ref[:, base + HY_W:base + 2 * HY_W]) * decay
        k_ref[:, o * HY_W:(o + 1) * HY_W] = jnp.where(lag >= 0, kf, jnp.where(lag == -length, 0.0, kb))


def _hyena_filter(fw, length, tr):
    bands = (HY_EMB - 1) // 2
    fbands = jnp.linspace(1e-4, bands - 1, bands, dtype=F32)[None, :]
    deltas = jnp.linspace(math.log(HY_TARGET) / HY_SLOW, math.log(HY_TARGET) / HY_FAST, HY_W, dtype=F32)[None, :]
    fw1 = fw["fw1"]
    consts = (fbands, deltas, fw1[0:1], fw1[1:1 + bands], fw1[1 + bands:], fw["fb1"][None], fw["fw2"],
              fw["fb2"][None], fw["fw3"], fw["fb3"][None], fw["fw4"], fw["freq"][None])
    return pl.pallas_call(
        functools.partial(_filter_kernel, length=length, tr=tr),
        grid=(2 * length // tr,),
        in_specs=[pl.BlockSpec(a.shape, lambda i: (0, 0)) for a in consts],
        out_specs=pl.BlockSpec((tr, HY_ORDER * HY_W), lambda i: (i, 0)),
        out_shape=jax.ShapeDtypeStruct((2 * length, HY_ORDER * HY_W), F32),
        compiler_params=_cparams(("parallel",)),
        name="hyena_filter",
    )(*consts)


def _kspec_kernel(f_ref, k0_ref, k1_ref, o_ref, *, p):
    fwd = f_ref[...]
    a = _dot_hp(fwd, k1_ref[...])
    a1 = _dot_hp(fwd, k0_ref[...])
    a1_re = a1[:p] - k0_ref[0:1, :]
    a1_im = a1[p:]
    sign = 1.0 - 2.0 * (lax.broadcasted_iota(jnp.int32, (p, 1), 0) % 2).astype(F32)
    o_ref[0, :p, :] = a[:p] - sign * a1_im
    o_ref[0, p:, :] = a[p:] + sign * a1_re


def _hyena_spectra(kfull, fwd, p):
    nb = kfull.shape[0] // (2 * p)
    cols = kfull.shape[1]
    return pl.pallas_call(
        functools.partial(_kspec_kernel, p=p),
        grid=(2 * nb - 1,),
        in_specs=[
            pl.BlockSpec(fwd.shape, lambda i: (0, 0)),
            pl.BlockSpec((p, cols), lambda i: (i, 0)),
            pl.BlockSpec((p, cols), lambda i: (i + 1, 0)),
        ],
        out_specs=pl.BlockSpec((1, 2 * p, cols), lambda i: (i, 0, 0)),
        out_shape=jax.ShapeDtypeStruct((2 * nb - 1, 2 * p, cols), F32),
        compiler_params=_cparams(("parallel",)),
        name="hyena_spectra",
    )(fwd, kfull, kfull)


HY_CT = 128
HY_NB = 2
HY_RT = 32


def _hyconv_kernel(zin_ref, gate_ref, k_ref, bias_ref, fwd_ref, inv_ref, o_ref, zf_ref, ya_ref, yb_ref, *, nb, p):
    fwd = fwd_ref[...]
    for j in range(nb):
        zb = jnp.concatenate([zin_ref[s, j * p:(j + 1) * p, :] for s in range(HY_NB)], axis=1).astype(BF16)
        zf_ref[j] = _dot(fwd, zb)

    def mac(i, y_ref):
        for rt in range(p // HY_RT):
            r0 = rt * HY_RT
            acc_re = [None] * HY_NB
            acc_im = [None] * HY_NB
            for j in range(nb):
                d = i - j + (nb - 1)
                kre = k_ref[d, r0:r0 + HY_RT, :]
                kim = k_ref[d, p + r0:p + r0 + HY_RT, :]
                zre = zf_ref[j, r0:r0 + HY_RT, :]
                zim = zf_ref[j, p + r0:p + r0 + HY_RT, :]
                for s in range(HY_NB):
                    zr = zre[:, s * HY_CT:(s + 1) * HY_CT]
                    zi = zim[:, s * HY_CT:(s + 1) * HY_CT]
                    re = kre * zr - kim * zi
                    im = kre * zi + kim * zr
                    acc_re[s] = re if j == 0 else acc_re[s] + re
                    acc_im[s] = im if j == 0 else acc_im[s] + im
            y_ref[r0:r0 + HY_RT, :] = jnp.concatenate(acc_re, axis=1).astype(BF16)
            y_ref[p + r0:p + r0 + HY_RT, :] = jnp.concatenate(acc_im, axis=1).astype(BF16)

    def finish(i, y_ref):
        y = _dot(inv_ref[...], y_ref[...])
        rows = pl.ds(pl.multiple_of(i * p, p), p)
        for s in range(HY_NB):
            zs = zin_ref[s, rows, :]
            o_ref[s, rows, :] = gate_ref[s, rows, :] * (y[:, s * HY_CT:(s + 1) * HY_CT] + bias_ref[...] * zs)

    mac(0, ya_ref)
    for i in range(1, nb):
        mac(i, yb_ref if i % 2 else ya_ref)
        finish(i - 1, ya_ref if i % 2 else yb_ref)
    finish(nb - 1, ya_ref if nb % 2 else yb_ref)


def _hyconv(zin, zin_col, gate, gate_col, kspec, k_col, bias, fwd, inv, p):
    bsz, length, _ = zin.shape
    nb = length // p
    nct = HY_W // HY_CT
    nd = kspec.shape[0]
    return pl.pallas_call(
        functools.partial(_hyconv_kernel, nb=nb, p=p),
        grid=(nct, bsz // HY_NB),
        in_specs=[
            pl.BlockSpec((HY_NB, length, HY_CT), lambda c, b: (b, 0, zin_col + c)),
            pl.BlockSpec((HY_NB, length, HY_CT), lambda c, b: (b, 0, gate_col + c)),
            pl.BlockSpec((nd, 2 * p, HY_CT), lambda c, b: (0, 0, k_col + c), pipeline_mode=pl.Buffered(1)),
            pl.BlockSpec((1, HY_CT), lambda c, b: (0, c)),
            _const_spec(fwd.shape),
            _const_spec(inv.shape),
        ],
        out_specs=pl.BlockSpec((HY_NB, length, HY_CT), lambda c, b: (b, 0, c)),
        out_shape=jax.ShapeDtypeStruct((bsz, length, HY_W), F32),
        scratch_shapes=[
            pltpu.VMEM((nb, 2 * p, HY_NB * HY_CT), F32),
            pltpu.VMEM((2 * p, HY_NB * HY_CT), BF16),
            pltpu.VMEM((2 * p, HY_NB * HY_CT), BF16),
        ],
        compiler_params=_cparams(("arbitrary", "arbitrary")),
        name="hyconv",
    )(zin, gate, kspec, bias, fwd, inv)


def _ssd_dir(xbc_ref, dta_ref, dtat_ref, h_ref, y_ref, *, q, rev):
    n, hp, hd = SSD_STATE, SSD_HEADS // SSD_GROUPS, SSD_HEAD_DIM
    c0 = SSD_HEADS if rev else 0
    end = 0 if rev else q - 1
    li = lax.broadcasted_iota(jnp.int32, (q, q), 0)
    si = lax.broadcasted_iota(jnp.int32, (q, q), 1)
    keep = (li <= si) if rev else (li >= si)
    low = lax.broadcasted_iota(jnp.int32, (q, LANE), 1) < hd
    pairs = hp * hd // LANE
    for g in range(SSD_GROUPS):
        bg = xbc_ref[0, :, SSD_W + g * n:SSD_W + (g + 1) * n]
        cg = xbc_ref[0, :, SSD_W + SSD_GROUPS * n + g * n:SSD_W + SSD_GROUPS * n + (g + 1) * n]
        scores = lax.dot_general(cg, bg, (((1,), (1,)), ((), ())), preferred_element_type=F32)
        hg = h_ref[g]
        y_off = _dot(cg, hg.astype(BF16))
        ys, xws, tots = [], [], []
        for t in range(pairs):
            cols = slice(g * hp * hd + t * LANE, g * hp * hd + (t + 1) * LANE)
            acs_b, dt_b = [], []
            for r in (2 * t, 2 * t + 1):
                c = c0 + g * hp + r
                acs_b.append(jnp.broadcast_to(dta_ref[0, :, N_DT + c:N_DT + c + 1], (q, LANE)))
                dt_b.append(jnp.broadcast_to(dta_ref[0, :, c:c + 1], (q, LANE)))
            acs2 = jnp.where(low, acs_b[0], acs_b[1])
            dt2 = jnp.where(low, dt_b[0], dt_b[1])
            xdt = xbc_ref[0, :, cols].astype(F32) * dt2
            yd = None
            for i, r in enumerate((2 * t, 2 * t + 1)):
                c = c0 + g * hp + r
                seg = jnp.concatenate([acs_b[i]] * (q // LANE), axis=1) - dtat_ref[0, N_DT + c:N_DT + c + 1, :]
                m = (jnp.exp2(jnp.where(keep, seg, -1e30)) * scores).astype(BF16)
                xh = jnp.where(low if i == 0 else jnp.logical_not(low), xdt, 0.0).astype(BF16)
                part = _dot(m, xh)
                yd = part if yd is None else yd + part
            ys.append(yd + jnp.exp2(acs2) * y_off[:, t * LANE:(t + 1) * LANE])
            tot = acs2[end:end + 1, :]
            xws.append((xdt * jnp.exp2(tot - acs2)).astype(BF16))
            tots.append(tot)
        y_ref[0, :, g * hp * hd:(g + 1) * hp * hd] = jnp.concatenate(ys, axis=1).astype(y_ref.dtype)
        upd = lax.dot_general(bg, jnp.concatenate(xws, axis=1), (((0,), (0,)), ((), ())),
                              preferred_element_type=F32)
        h_ref[g] = hg * jnp.exp2(jnp.concatenate(tots, axis=1)) + upd


def _ssd_kernel(xf_ref, dtaf_ref, dtatf_ref, xb_ref, dtab_ref, dtatb_ref, h0f_ref, h0b_ref,
                yf_ref, yb_ref, hf_out, hb_out, hf_ref, hb_ref, *, q):
    k = pl.program_id(1)

    @pl.when(k == 0)
    def _():
        hf_ref[...] = h0f_ref[0]
        hb_ref[...] = h0b_ref[0]

    _ssd_dir(xf_ref, dtaf_ref, dtatf_ref, hf_ref, yf_ref, q=q, rev=False)
    _ssd_dir(xb_ref, dtab_ref, dtatb_ref, hb_ref, yb_ref, q=q, rev=True)

    @pl.when(k == pl.num_programs(1) - 1)
    def _():
        hf_out[0] = hf_ref[...]
        hb_out[0] = hb_ref[...]


def _ssd(xbc, dta, dtat, h0f, h0b, q):
    bsz, length, _ = xbc.shape
    nc = length // q
    hshape = (SSD_GROUPS, SSD_STATE, SSD_W // SSD_GROUPS)
    fw = lambda b, k: (b, k, 0)
    bw = lambda b, k: (b, nc - 1 - k, 0)
    st = pl.BlockSpec((1,) + hshape, lambda b, k: (b, 0, 0, 0))
    return pl.pallas_call(
        functools.partial(_ssd_kernel, q=q),
        grid=(bsz, nc),
        in_specs=[
            pl.BlockSpec((1, q, SSD_XBC), fw),
            pl.BlockSpec((1, q, 2 * N_DT), fw),
            pl.BlockSpec((1, 2 * N_DT, q), lambda b, k: (b, 0, k)),
            pl.BlockSpec((1, q, SSD_XBC), bw),
            pl.BlockSpec((1, q, 2 * N_DT), bw),
            pl.BlockSpec((1, 2 * N_DT, q), lambda b, k: (b, 0, nc - 1 - k)),
            st, st,
        ],
        out_specs=[pl.BlockSpec((1, q, SSD_W), fw), pl.BlockSpec((1, q, SSD_W), bw), st, st],
        out_shape=[
            jax.ShapeDtypeStruct((bsz, length, SSD_W), BF16),
            jax.ShapeDtypeStruct((bsz, length, SSD_W), BF16),
            jax.ShapeDtypeStruct((bsz,) + hshape, F32),
            jax.ShapeDtypeStruct((bsz,) + hshape, F32),
        ],
        scratch_shapes=[pltpu.VMEM(hshape, F32), pltpu.VMEM(hshape, F32)],
        compiler_params=_cparams(("parallel", "arbitrary")),
        name="ssd",
    )(xbc, dta, dtat, xbc, dta, dtat, h0f, h0b)


def _outproj_kernel(x_ref, mod_ref, g_ref, hy_ref, yf_ref, yb_ref, xs_ref, z_ref, sc_ref, dexp_ref, mg_ref,
                    why_ref, wssd_ref, wsc_ref, o_ref):
    x = x_ref[0]
    mod = mod_ref[0]
    mg = mg_ref[...]
    y_hy = _rms(hy_ref[0], mg[:, :HY_W]).astype(BF16)
    y = yf_ref[0].astype(F32) + yb_ref[0].astype(F32) + dexp_ref[...] * xs_ref[0].astype(F32)
    y_ssd = _rms(y * _silu(z_ref[0].astype(F32)), mg[:, HY_W:HY_W + SSD_W]).astype(BF16)
    o = _dot(y_hy, why_ref[...]) + _dot(y_ssd, wssd_ref[...]) + _dot(sc_ref[0], wsc_ref[...])
    o_ref[0] = x + _rms(o, mod[5:6] * g_ref[3:4])


def _outproj(x, mod, g, hy, yf, yb, xbc, z, sc, lw, tm):
    bsz, length, d = x.shape
    per_batch = mod.shape[0] != 1
    tok = lambda w: pl.BlockSpec((1, tm, w), lambda b, t: (b, t, 0))
    consts = (lw["d_exp"], lw["mix_gain"], lw["w_out_hy"], lw["w_out_ssd"], lw["w_out_sc"])
    return pl.pallas_call(
        _outproj_kernel,
        grid=(bsz, length // tm),
        in_specs=[
            tok(d),
            pl.BlockSpec((1, N_MOD, d), (lambda b, t: (b, 0, 0)) if per_batch else (lambda b, t: (0, 0, 0))),
            _const_spec(g.shape),
            tok(HY_W), tok(SSD_W), tok(SSD_W), tok(SSD_W), tok(SSD_W), tok(SC_W),
        ] + [_const_spec(a.shape) for a in consts],
        out_specs=tok(d),
        out_shape=jax.ShapeDtypeStruct(x.shape, F32),
        compiler_params=_cparams(("parallel", "parallel")),
        name="outproj",
    )(x, mod, g, hy, yf, yb, xbc, z, sc, *consts)


def _mixer(xs, mod, g, lw, kspec, dft, seg, tm, p, q, h0f, h0b, batch_shape, want_out):
    bsz, length = batch_shape
    hy, z, xbc, sc, dta, dtat = _inproj(xs, mod, g, lw, seg, tm, q)
    if xs.shape[0] != bsz:
        dtat = dtat.reshape(2 * N_DT, bsz, length).transpose(1, 0, 2)
    seq = lambda a: a.reshape(bsz, length, a.shape[-1])
    yf, yb, hf, hb = _ssd(seq(xbc), seq(dta), dtat, h0f, h0b, q)
    if not want_out:
        return None, hf, hb
    fwd, inv = dft
    nct = HY_W // HY_CT
    hy_s = seq(hy)
    fwd_b, inv_b = fwd.astype(BF16), inv.astype(BF16)
    z1 = _hyconv(hy_s, 0, hy_s, nct, kspec, 0, lw["hy_bias"][0:1], fwd_b, inv_b, p)
    z2 = _hyconv(z1, 0, hy_s, 2 * nct, kspec, nct, lw["hy_bias"][1:2], fwd_b, inv_b, p)
    tok = lambda a: a.reshape(xs.shape[0], xs.shape[1], a.shape[-1])
    out = _outproj(xs, mod, g, tok(z2), tok(yf), tok(yb), xbc, z, sc, lw, tm)
    return out, hf, hb


def _layer_weights(l, w_in, w_out, hy_conv_w, hy_conv_b, hy_bias, ssd_conv_w, ssd_conv_b, ssd_a_log, ssd_dt_bias,
                   ssd_d, sc_conv_w, mix_gain):
    o_dt = HY_IN + SSD_W + SSD_XBC
    wl = w_in[l]
    w_dt = wl[:, o_dt:o_dt + N_DT].astype(BF16)
    dt_bias = ssd_dt_bias[l].reshape(1, N_DT)
    return {
        "w_in": jnp.concatenate([wl[:, :o_dt], wl[:, o_dt + N_DT:]], axis=1).astype(BF16),
        "w_dt": w_dt,
        "w_dtt": w_dt.T,
        "hy_conv_w": hy_conv_w[l],
        "hy_conv_b": hy_conv_b[l][None],
        "ssd_conv_w": ssd_conv_w[l],
        "ssd_conv_b": ssd_conv_b[l][None],
        "sc_conv_w": sc_conv_w[l],
        "sc_gain": mix_gain[l][None, HY_W + SSD_W:],
        "dt_bias": dt_bias,
        "dt_bias_t": dt_bias.reshape(N_DT, 1),
        "hy_bias": hy_bias[l],
        "a2_row": (-jnp.exp(ssd_a_log[l]) * math.log2(math.e)).reshape(1, N_DT),
        "a2_col": (-jnp.exp(ssd_a_log[l]) * math.log2(math.e)).reshape(N_DT, 1),
        "d_exp": jnp.repeat(ssd_d[l], SSD_HEAD_DIM)[None],
        "mix_gain": mix_gain[l][None],
        "w_out_hy": w_out[l, :HY_W].astype(BF16),
        "w_out_ssd": w_out[l, HY_W:HY_W + SSD_W].astype(BF16),
        "w_out_sc": w_out[l, HY_W + SSD_W:].astype(BF16),
    }


def _pick(n, prefs):
    for v in prefs:
        if n % v == 0:
            return v
    return n


def kernel(x, c, ctx, c_ctx, w_mod, b_mod, norm_g, ffn_w_in, ffn_w_out, w_in, w_out, hy_conv_w, hy_conv_b, hy_fw1,
           hy_fb1, hy_fw2, hy_fb2, hy_fw3, hy_fb3, hy_fw4, hy_freq, hy_bias, ssd_conv_w, ssd_conv_b, ssd_a_log,
           ssd_dt_bias, ssd_d, sc_conv_w, mix_gain):
    bsz, n_lat, d = x.shape
    _, ctx_len, _ = ctx.shape
    depth = w_mod.shape[0]
    dff = ffn_w_out.shape[2]

    rows = -(-(bsz + 1) // 8) * 8
    cvec = jnp.zeros((rows, d), F32).at[:bsz].set(c).at[bsz].set(c_ctx)
    mod_all = _mod_table(cvec, w_mod, b_mod).reshape(depth, rows, N_MOD, d)

    tm = _pick(n_lat, (512, 256, 128, GRID_W))
    p_lat = _pick(n_lat, (512, 256, 128))
    q_lat = _pick(n_lat, (256, 128))
    xc = ctx.reshape(1, bsz * ctx_len, d)
    tm_c = ctx_len * _pick(bsz, (2, 1)) if ctx_len < 512 else _pick(ctx_len, (512, 256))
    p_ctx = _pick(ctx_len, (512, 256, 128))
    q_ctx = _pick(ctx_len, (256, 128))
    dft_lat = _dft_mats(p_lat)
    dft_ctx = _dft_mats(p_ctx)

    h_zero = jnp.zeros((bsz, SSD_GROUPS, SSD_STATE, SSD_W // SSD_GROUPS), F32)
    for l in range(depth):
        last = l == depth - 1
        mod_x = mod_all[l, :bsz]
        mod_c = mod_all[l, bsz:bsz + 1]
        g = norm_g[l]
        lw = _layer_weights(l, w_in, w_out, hy_conv_w, hy_conv_b, hy_bias, ssd_conv_w, ssd_conv_b, ssd_a_log,
                            ssd_dt_bias, ssd_d, sc_conv_w, mix_gain)
        fw = {"fw1": hy_fw1[l], "fb1": hy_fb1[l], "fw2": hy_fw2[l], "fb2": hy_fb2[l], "fw3": hy_fw3[l],
              "fb3": hy_fb3[l], "fw4": hy_fw4[l], "freq": hy_freq[l]}
        ffn_w = [(ffn_w_in[l, i, :, :dff].astype(BF16), ffn_w_in[l, i, :, dff:].astype(BF16),
                  ffn_w_out[l, i].astype(BF16)) for i in range(2)]

        x = _ffn(x, mod_x, g, *ffn_w[0], 0, tm)
        xc = _ffn(xc, mod_c, g, *ffn_w[0], 0, tm_c)

        ks_ctx = None
        if not last:
            ks_ctx = _hyena_spectra(_hyena_filter(fw, ctx_len, p_ctx), dft_ctx[0], p_ctx)
        y_c, h_f, h_b = _mixer(xc, mod_c, g, lw, ks_ctx, dft_ctx, ctx_len, tm_c, p_ctx, q_ctx, h_zero, h_zero,
                               (bsz, ctx_len), not last)
        ks_lat = _hyena_spectra(_hyena_filter(fw, n_lat, p_lat), dft_lat[0], p_lat)
        x, _, _ = _mixer(x, mod_x, g, lw, ks_lat, dft_lat, GRID_W, tm, p_lat, q_lat, h_f, h_b, (bsz, n_lat), True)
        x = _ffn(x, mod_x, g, *ffn_w[1], 2, tm)
        if not last:
            xc = _ffn(y_c, mod_c, g, *ffn_w[1], 2, tm_c)
    return x
```

```python
import functools
import math

import jax
import jax.numpy as jnp
import numpy as np
from jax import lax
from jax.experimental import pallas as pl
from jax.experimental.pallas import tpu as pltpu

GRID_W = 64
EPS = 1e-6
FFN_RES_W = 0.5
N_MOD = 9
HY_W = 256
SSD_HEADS = 8
SSD_HEAD_DIM = 64
SSD_GROUPS = 2
SSD_STATE = 128
SC_W = 256
SHORT_K = 3
HY_ORDER = 2
HY_EMB = 33
HY_FAST = 0.3
HY_SLOW = 1.5
HY_TARGET = 1e-2

SSD_W = SSD_HEADS * SSD_HEAD_DIM
HY_IN = 3 * HY_W
SSD_XBC = SSD_W + 2 * SSD_GROUPS * SSD_STATE
SC_IN = 3 * SC_W
N_DT = 2 * SSD_HEADS

LANE = 128
SUBLANE = 8
MXU_TILE = 256
FFN_CHUNK = 3 * MXU_TILE
INPROJ_CHUNK = 2 * MXU_TILE
SSD_CPS = 4
VMEM_LIMIT = 56 * 1024 * 1024

F32 = jnp.float32
BF16 = jnp.bfloat16


def _cparams(sem):
    return pltpu.CompilerParams(dimension_semantics=sem, vmem_limit_bytes=VMEM_LIMIT)


def _const_spec(shape):
    nd = len(shape)
    return pl.BlockSpec(shape, lambda *_: (0,) * nd, pipeline_mode=pl.Buffered(1))


def _dot(a, b):
    return jnp.dot(a, b, preferred_element_type=F32)


def _split2(a):
    hi = a.astype(BF16)
    lo = (a - hi.astype(F32)).astype(BF16)
    return hi, lo


def _split3(a):
    hi = a.astype(BF16)
    r = a - hi.astype(F32)
    mid = r.astype(BF16)
    lo = (r - mid.astype(F32)).astype(BF16)
    return hi, mid, lo


def _dot_hp(a, b):
    ah, al = _split2(a)
    bh, bl = _split2(b)
    return _dot(ah, bh) + (_dot(ah, bl) + _dot(al, bh))


def _silu(x):
    hx = 0.5 * x
    return hx + hx * jnp.tanh(hx)


def _rms(x, g):
    ms = jnp.mean(x * x, axis=-1, keepdims=True)
    return x * (lax.rsqrt(ms + EPS) * g)


def _mod_kernel(c_ref, w_ref, b_ref, o_ref):
    o_ref[0] = _dot_hp(_silu(c_ref[...]), w_ref[0]) + b_ref[0]


def _mod_table(cvec, w_mod, b_mod):
    depth, d, nm = w_mod.shape
    r = cvec.shape[0]
    tn = 1024 if nm % 1024 == 0 else d
    return pl.pallas_call(
        _mod_kernel,
        grid=(depth, nm // tn),
        in_specs=[
            pl.BlockSpec((r, d), lambda l, n: (0, 0)),
            pl.BlockSpec((1, d, tn), lambda l, n: (l, 0, n)),
            pl.BlockSpec((1, 1, tn), lambda l, n: (l, 0, n)),
        ],
        out_specs=pl.BlockSpec((1, r, tn), lambda l, n: (l, 0, n)),
        out_shape=jax.ShapeDtypeStruct((depth, r, nm), F32),
        compiler_params=_cparams(("parallel", "parallel")),
        name="mod_table",
    )(cvec, w_mod, b_mod.reshape(depth, 1, nm))


def _ffn_kernel(x_ref, mod_ref, g_ref, wg_ref, wu_ref, wo_ref, o_ref, *, sub, bounds):
    x = x_ref[0]
    mod = mod_ref[0]
    g = g_ref[...]
    shift, scale, gate = mod[3 * sub:3 * sub + 1], mod[3 * sub + 1:3 * sub + 2], mod[3 * sub + 2:3 * sub + 3]
    g_pre, g_post = g[2 * sub:2 * sub + 1], g[2 * sub + 1:2 * sub + 2]
    h = (_rms(x, g_pre * (1.0 + scale)) + shift).astype(BF16)
    acc = None
    for lo, hi in zip(bounds[:-1], bounds[1:]):
        gg = _dot(h, wg_ref[:, lo:hi])
        uu = _dot(h, wu_ref[:, lo:hi])
        a = (_silu(gg) * uu).astype(BF16)
        part = _dot(a, wo_ref[lo:hi, :])
        acc = part if acc is None else acc + part
    o_ref[0] = x + _rms(acc, FFN_RES_W * gate * g_post)


def _ffn_bounds(dff, width):
    return tuple(range(0, dff, width)) + (dff,)


def _ffn(x, mod, g, wg, wu, wo, sub, tm, width=FFN_CHUNK):
    bsz, length, d = x.shape
    dff = wg.shape[1]
    per_batch = mod.shape[0] != 1
    return pl.pallas_call(
        functools.partial(_ffn_kernel, sub=sub, bounds=_ffn_bounds(dff, width)),
        grid=(bsz, length // tm),
        in_specs=[
            pl.BlockSpec((1, tm, d), lambda b, t: (b, t, 0)),
            pl.BlockSpec((1, N_MOD, d), (lambda b, t: (b, 0, 0)) if per_batch else (lambda b, t: (0, 0, 0))),
            _const_spec(g.shape),
            _const_spec(wg.shape),
            _const_spec(wu.shape),
            _const_spec(wo.shape),
        ],
        out_specs=pl.BlockSpec((1, tm, d), lambda b, t: (b, t, 0)),
        out_shape=jax.ShapeDtypeStruct(x.shape, F32),
        compiler_params=_cparams(("parallel", "parallel")),
        name="ffn",
    )(x, mod, g, wg, wu, wo)


def _conv3(a, w, seg):
    n, c = a.shape
    a4 = a.reshape(n // seg, seg // SUBLANE, SUBLANE, c)
    sub = lax.broadcasted_iota(jnp.int32, a4.shape, 2)
    edge = jnp.zeros((n // seg, 1, SUBLANE, c), F32)
    down = pltpu.roll(a4, 1, 2)
    prev = jnp.where(sub == 0, jnp.concatenate([edge, down[:, :-1]], axis=1), down)
    up = pltpu.roll(a4, SUBLANE - 1, 2)
    nxt = jnp.where(sub == SUBLANE - 1, jnp.concatenate([up[:, 1:], edge], axis=1), up)
    return (prev.reshape(n, c) * w[0:1] + a * w[1:2]) + nxt.reshape(n, c) * w[2:3]


def _chunk_scan(da, q, axis):
    n = da.shape[axis]
    pos = lax.broadcasted_iota(jnp.int32, da.shape, axis) % q
    head = lax.broadcasted_iota(jnp.int32, da.shape, 1 - axis)
    up, down = da, da
    s = 1
    while s < q:
        up = up + jnp.where(pos >= s, pltpu.roll(up, s, axis), 0.0)
        down = down + jnp.where(pos < q - s, pltpu.roll(down, n - s, axis), 0.0)
        s *= 2
    return jnp.where(head < SSD_HEADS, up, down)


def _inproj_kernel(x_ref, mod_ref, g_ref, w_ref, wdtt_ref, hyw_ref, hyb_ref, sw_ref, sb_ref, scw_ref,
                   scg_ref, dtbt_ref, a2c_ref, hy_ref, z_ref, xbc_ref, sc_ref, dta_ref, dtat_ref, *, seg, q):
    x = x_ref[0]
    mod = mod_ref[0]
    h = (_rms(x, g_ref[2:3] * (1.0 + mod[4:5])) + mod[3:4]).astype(BF16)
    dtt = lax.dot_general(wdtt_ref[...], h, (((1,), (1,)), ((), ())), preferred_element_type=F32)
    dtt = jax.nn.softplus(dtt + dtbt_ref[...])
    rowform = jnp.concatenate([dtt, _chunk_scan(dtt * a2c_ref[...], q, 1)], axis=0)
    dtat_ref[0] = rowform
    pad = jnp.zeros((LANE - 2 * N_DT, rowform.shape[1]), F32)
    dta_ref[0] = jnp.concatenate([rowform, pad], axis=0).T[:, :2 * N_DT]
    o_z, o_xbc, o_sc = HY_IN, HY_IN + SSD_W, HY_IN + SSD_W + SSD_XBC

    def post_hy(p, lo, hi):
        hy_ref[0, :, lo:hi] = _conv3(p, hyw_ref[:, lo:hi], seg) + hyb_ref[:, lo:hi]

    def post_z(p, lo, hi):
        z_ref[0, :, lo - o_z:hi - o_z] = p.astype(BF16)

    def post_xbc(p, lo, hi):
        lo, hi = lo - o_xbc, hi - o_xbc
        xbc_ref[0, :, lo:hi] = _silu(_conv3(p, sw_ref[:, lo:hi], seg) + sb_ref[:, lo:hi]).astype(BF16)

    def post_sc(p, lo, hi):
        gb, gc, hx = p[:, :SC_W], p[:, SC_W:2 * SC_W], p[:, 2 * SC_W:]
        sc_ref[0] = _rms(gb * _conv3(gc * hx, scw_ref[...], seg), scg_ref[...]).astype(BF16)

    stages = []
    for post, lo, hi in ((post_hy, 0, o_z), (post_z, o_z, o_xbc), (post_xbc, o_xbc, o_sc)):
        stages += [(post, c, min(c + INPROJ_CHUNK, hi)) for c in range(lo, hi, INPROJ_CHUNK)]
    stages.append((post_sc, o_sc, o_sc + SC_IN))
    pending = None
    for post, lo, hi in stages:
        p = _dot(h, w_ref[:, lo:hi])
        if pending is not None:
            pending[0](*pending[1:])
        pending = (post, p, lo, hi)
    pending[0](*pending[1:])


def _inproj(x, mod, g, lw, seg, tm, q):
    bsz, length, d = x.shape
    per_batch = mod.shape[0] != 1
    tok = lambda w: pl.BlockSpec((1, tm, w), lambda b, t: (b, t, 0))
    consts = (lw["w_in"], lw["w_dtt"], lw["hy_conv_w"], lw["hy_conv_b"], lw["ssd_conv_w"],
              lw["ssd_conv_b"], lw["sc_conv_w"], lw["sc_gain"], lw["dt_bias_t"], lw["a2_col"])
    return pl.pallas_call(
        functools.partial(_inproj_kernel, seg=seg, q=q),
        grid=(bsz, length // tm),
        in_specs=[
            tok(d),
            pl.BlockSpec((1, N_MOD, d), (lambda b, t: (b, 0, 0)) if per_batch else (lambda b, t: (0, 0, 0))),
            _const_spec(g.shape),
        ] + [_const_spec(a.shape) for a in consts],
        out_specs=[tok(HY_IN), tok(SSD_W), tok(SSD_XBC), tok(SC_W), tok(2 * N_DT),
                   pl.BlockSpec((1, 2 * N_DT, tm), lambda b, t: (b, 0, t))],
        out_shape=[
            jax.ShapeDtypeStruct((bsz, length, HY_IN), F32),
            jax.ShapeDtypeStruct((bsz, length, SSD_W), BF16),
            jax.ShapeDtypeStruct((bsz, length, SSD_XBC), BF16),
            jax.ShapeDtypeStruct((bsz, length, SC_W), BF16),
            jax.ShapeDtypeStruct((bsz, length, 2 * N_DT), F32),
            jax.ShapeDtypeStruct((bsz, 2 * N_DT, length), F32),
        ],
        compiler_params=_cparams(("parallel", "parallel")),
        name="inproj",
    )(x, mod, g, *consts)


def _dft_mats(p):
    f = np.arange(p, dtype=np.int64)[:, None]
    t = np.arange(p, dtype=np.int64)[None, :]
    ang = (np.pi / (2 * p)) * (((2 * f + 1) * t) % (4 * p)).astype(np.float64)
    fwd = np.concatenate([np.cos(ang), -np.sin(ang)], axis=0)
    return jnp.asarray(fwd, F32), jnp.asarray(fwd.T / p, F32)


def _filter_kernel(fb_ref, dl_ref, w1t_ref, w1c_ref, w1s_ref, b1_ref, w2_ref, b2_ref, w3_ref, b3_ref, w4_ref,
                   fr_ref, k_ref, *, length, tr):
    lag = pl.program_id(0) * tr + lax.broadcasted_iota(jnp.int32, (tr, 1), 0) - length
    n = jnp.abs(lag).astype(F32)
    t = n * (1.0 / (length - 1))
    w = n * (2.0 * math.pi / length)
    ang = fb_ref[...] * w
    fr = fr_ref[...]
    h = t * w1t_ref[...] + _dot_hp(jnp.cos(ang), w1c_ref[...]) - _dot_hp(jnp.sin(ang), w1s_ref[...])
    h = jnp.sin(fr * (h + b1_ref[...]))
    h = jnp.sin(fr * (_dot_hp(h, w2_ref[...]) + b2_ref[...]))
    h = jnp.sin(fr * (_dot_hp(h, w3_ref[...]) + b3_ref[...]))
    decay = jnp.exp(-t * jnp.abs(dl_ref[...]))
    for o in range(HY_ORDER):
        base = o * 2 * HY_W
        kf = _dot_hp(h, w4_ref[:, base:base + HY_W]) * decay
        kb = _dot_hp(h, w4_ref[:, base + HY_W:base + 2 * HY_W]) * decay
        k_ref[:, o * HY_W:(o + 1) * HY_W] = jnp.where(lag >= 0, kf, jnp.where(lag == -length, 0.0, kb))


def _hyena_filter(fw, length, tr):
    bands = (HY_EMB - 1) // 2
    fbands = jnp.linspace(1e-4, bands - 1, bands, dtype=F32)[None, :]
    deltas = jnp.linspace(math.log(HY_TARGET) / HY_SLOW, math.log(HY_TARGET) / HY_FAST, HY_W, dtype=F32)[None, :]
    fw1 = fw["fw1"]
    consts = (fbands, deltas, fw1[0:1], fw1[1:1 + bands], fw1[1 + bands:], fw["fb1"][None], fw["fw2"],
              fw["fb2"][None], fw["fw3"], fw["fb3"][None], fw["fw4"], fw["freq"][None])
    return pl.pallas_call(
        functools.partial(_filter_kernel, length=length, tr=tr),
        grid=(2 * length // tr,),
        in_specs=[pl.BlockSpec(a.shape, lambda i: (0, 0)) for a in consts],
        out_specs=pl.BlockSpec((tr, HY_ORDER * HY_W), lambda i: (i, 0)),
        out_shape=jax.ShapeDtypeStruct((2 * length, HY_ORDER * HY_W), F32),
        compiler_params=_cparams(("parallel",)),
        name="hyena_filter",
    )(*consts)


def _kspec_kernel(f_ref, k0_ref, k1_ref, o_ref, *, p):
    fwd = f_ref[...]
    a = _dot_hp(fwd, k1_ref[...])
    a1 = _dot_hp(fwd, k0_ref[...])
    a1_re = a1[:p] - k0_ref[0:1, :]
    a1_im = a1[p:]
    sign = 1.0 - 2.0 * (lax.broadcasted_iota(jnp.int32, (p, 1), 0) % 2).astype(F32)
    o_ref[0, :p, :] = a[:p] - sign * a1_im
    o_ref[0, p:, :] = a[p:] + sign * a1_re


def _hyena_spectra(kfull, fwd, p):
    nb = kfull.shape[0] // (2 * p)
    cols = kfull.shape[1]
    return pl.pallas_call(
        functools.partial(_kspec_kernel, p=p),
        grid=(2 * nb - 1,),
        in_specs=[
            pl.BlockSpec(fwd.shape, lambda i: (0, 0)),
            pl.BlockSpec((p, cols), lambda i: (i, 0)),
            pl.BlockSpec((p, cols), lambda i: (i + 1, 0)),
        ],
        out_specs=pl.BlockSpec((1, 2 * p, cols), lambda i: (i, 0, 0)),
        out_shape=jax.ShapeDtypeStruct((2 * nb - 1, 2 * p, cols), F32),
        compiler_params=_cparams(("parallel",)),
        name="hyena_spectra",
    )(fwd, kfull, kfull)


HY_CT = 128
HY_NB = 2
HY_RT = 32


def _hyconv_kernel(zin_ref, gate_ref, k_ref, bias_ref, fwd_ref, inv_ref, o_ref, zf_ref, ya_ref, yb_ref, *, nb, p):
    fwd = fwd_ref[...]
    for j in range(nb):
        zb = jnp.concatenate([zin_ref[s, j * p:(j + 1) * p, :] for s in range(HY_NB)], axis=1).astype(BF16)
        zf_ref[j] = _dot(fwd, zb)

    def mac(i, y_ref):
        for rt in range(p // HY_RT):
            r0 = rt * HY_RT
            acc_re = [None] * HY_NB
            acc_im = [None] * HY_NB
            for j in range(nb):
                d = i - j + (nb - 1)
                kre = k_ref[d, r0:r0 + HY_RT, :]
                kim = k_ref[d, p + r0:p + r0 + HY_RT, :]
                zre = zf_ref[j, r0:r0 + HY_RT, :]
                zim = zf_ref[j, p + r0:p + r0 + HY_RT, :]
                for s in range(HY_NB):
                    zr = zre[:, s * HY_CT:(s + 1) * HY_CT]
                    zi = zim[:, s * HY_CT:(s + 1) * HY_CT]
                    re = kre * zr - kim * zi
                    im = kre * zi + kim * zr
                    acc_re[s] = re if j == 0 else acc_re[s] + re
                    acc_im[s] = im if j == 0 else acc_im[s] + im
            y_ref[r0:r0 + HY_RT, :] = jnp.concatenate(acc_re, axis=1).astype(BF16)
            y_ref[p + r0:p + r0 + HY_RT, :] = jnp.concatenate(acc_im, axis=1).astype(BF16)

    def finish(i, y_ref):
        y = _dot(inv_ref[...], y_ref[...])
        rows = pl.ds(pl.multiple_of(i * p, p), p)
        for s in range(HY_NB):
            zs = zin_ref[s, rows, :]
            o_ref[s, rows, :] = gate_ref[s, rows, :] * (y[:, s * HY_CT:(s + 1) * HY_CT] + bias_ref[...] * zs)

    mac(0, ya_ref)
    for i in range(1, nb):
        mac(i, yb_ref if i % 2 else ya_ref)
        finish(i - 1, ya_ref if i % 2 else yb_ref)
    finish(nb - 1, ya_ref if nb % 2 else yb_ref)


def _hyconv(zin, zin_col, gate, gate_col, kspec, k_col, bias, fwd, inv, p):
    bsz, length, _ = zin.shape
    nb = length // p
    nct = HY_W // HY_CT
    nd = kspec.shape[0]
    return pl.pallas_call(
        functools.partial(_hyconv_kernel, nb=nb, p=p),
        grid=(nct, bsz // HY_NB),
        in_specs=[
            pl.BlockSpec((HY_NB, length, HY_CT), lambda c, b: (b, 0, zin_col + c)),
            pl.BlockSpec((HY_NB, length, HY_CT), lambda c, b: (b, 0, gate_col + c)),
            pl.BlockSpec((nd, 2 * p, HY_CT), lambda c, b: (0, 0, k_col + c), pipeline_mode=pl.Buffered(1)),
            pl.BlockSpec((1, HY_CT), lambda c, b: (0, c)),
            _const_spec(fwd.shape),
            _const_spec(inv.shape),
        ],
        out_specs=pl.BlockSpec((HY_NB, length, HY_CT), lambda c, b: (b, 0, c)),
        out_shape=jax.ShapeDtypeStruct((bsz, length, HY_W), F32),
        scratch_shapes=[
            pltpu.VMEM((nb, 2 * p, HY_NB * HY_CT), F32),
            pltpu.VMEM((2 * p, HY_NB * HY_CT), BF16),
            pltpu.VMEM((2 * p, HY_NB * HY_CT), BF16),
        ],
        compiler_params=_cparams(("arbitrary", "arbitrary")),
        name="hyconv",
    )(zin, gate, kspec, bias, fwd, inv)


def _ssd_dir(xbc_ref, dta_ref, dtat_ref, h_ref, y_ref, *, q, rev, r0):
    rows = slice(r0, r0 + q)
    n, hp, hd = SSD_STATE, SSD_HEADS // SSD_GROUPS, SSD_HEAD_DIM
    c0 = SSD_HEADS if rev else 0
    end = 0 if rev else q - 1
    li = lax.broadcasted_iota(jnp.int32, (q, q), 0)
    si = lax.broadcasted_iota(jnp.int32, (q, q), 1)
    keep = (li <= si) if rev else (li >= si)
    low = lax.broadcasted_iota(jnp.int32, (q, LANE), 1) < hd
    pairs = hp * hd // LANE
    for g in range(SSD_GROUPS):
        bg = xbc_ref[0, rows, SSD_W + g * n:SSD_W + (g + 1) * n]
        cg = xbc_ref[0, rows, SSD_W + SSD_GROUPS * n + g * n:SSD_W + SSD_GROUPS * n + (g + 1) * n]
        scores = lax.dot_general(cg, bg, (((1,), (1,)), ((), ())), preferred_element_type=F32)
        hg = h_ref[g]
        y_off = _dot(cg, hg.astype(BF16))
        ys, xws, tots = [], [], []
        for t in range(pairs):
            cols = slice(g * hp * hd + t * LANE, g * hp * hd + (t + 1) * LANE)
            acs_b, dt_b = [], []
            for r in (2 * t, 2 * t + 1):
                c = c0 + g * hp + r
                acs_b.append(jnp.broadcast_to(dta_ref[0, rows, N_DT + c:N_DT + c + 1], (q, LANE)))
                dt_b.append(jnp.broadcast_to(dta_ref[0, rows, c:c + 1], (q, LANE)))
            acs2 = jnp.where(low, acs_b[0], acs_b[1])
            dt2 = jnp.where(low, dt_b[0], dt_b[1])
            xdt = xbc_ref[0, rows, cols].astype(F32) * dt2
            yd = None
            for i, r in enumerate((2 * t, 2 * t + 1)):
                c = c0 + g * hp + r
                seg = jnp.concatenate([acs_b[i]] * (q // LANE), axis=1) - dtat_ref[0, N_DT + c:N_DT + c + 1, rows]
                m = (jnp.exp2(jnp.where(keep, seg, -1e30)) * scores).astype(BF16)
                xh = jnp.where(low if i == 0 else jnp.logical_not(low), xdt, 0.0).astype(BF16)
                part = _dot(m, xh)
                yd = part if yd is None else yd + part
            ys.append(yd + jnp.exp2(acs2) * y_off[:, t * LANE:(t + 1) * LANE])
            tot = acs2[end:end + 1, :]
            xws.append((xdt * jnp.exp2(tot - acs2)).astype(BF16))
            tots.append(tot)
        y_ref[0, rows, g * hp * hd:(g + 1) * hp * hd] = jnp.concatenate(ys, axis=1).astype(y_ref.dtype)
        upd = lax.dot_general(bg, jnp.concatenate(xws, axis=1), (((0,), (0,)), ((), ())),
                              preferred_element_type=F32)
        h_ref[g] = hg * jnp.exp2(jnp.concatenate(tots, axis=1)) + upd


def _ssd_kernel(xf_ref, dtaf_ref, dtatf_ref, xb_ref, dtab_ref, dtatb_ref, h0f_ref, h0b_ref,
                yf_ref, yb_ref, hf_out, hb_out, hf_ref, hb_ref, *, q):
    k = pl.program_id(1)

    @pl.when(k == 0)
    def _():
        hf_ref[...] = h0f_ref[0]
        hb_ref[...] = h0b_ref[0]

    cps = xf_ref.shape[1] // q
    for j in range(cps):
        _ssd_dir(xf_ref, dtaf_ref, dtatf_ref, hf_ref, yf_ref, q=q, rev=False, r0=j * q)
        _ssd_dir(xb_ref, dtab_ref, dtatb_ref, hb_ref, yb_ref, q=q, rev=True, r0=(cps - 1 - j) * q)

    @pl.when(k == pl.num_programs(1) - 1)
    def _():
        hf_out[0] = hf_ref[...]
        hb_out[0] = hb_ref[...]


def _ssd(xbc, dta, dtat, h0f, h0b, q, cps):
    bsz, length, _ = xbc.shape
    qs = q * cps
    nc = length // qs
    hshape = (SSD_GROUPS, SSD_STATE, SSD_W // SSD_GROUPS)
    fw = lambda b, k: (b, k, 0)
    bw = lambda b, k: (b, nc - 1 - k, 0)
    st = pl.BlockSpec((1,) + hshape, lambda b, k: (b, 0, 0, 0))
    return pl.pallas_call(
        functools.partial(_ssd_kernel, q=q),
        grid=(bsz, nc),
        in_specs=[
            pl.BlockSpec((1, qs, SSD_XBC), fw),
            pl.BlockSpec((1, qs, 2 * N_DT), fw),
            pl.BlockSpec((1, 2 * N_DT, qs), lambda b, k: (b, 0, k)),
            pl.BlockSpec((1, qs, SSD_XBC), bw),
            pl.BlockSpec((1, qs, 2 * N_DT), bw),
            pl.BlockSpec((1, 2 * N_DT, qs), lambda b, k: (b, 0, nc - 1 - k)),
            st, st,
        ],
        out_specs=[pl.BlockSpec((1, qs, SSD_W), fw), pl.BlockSpec((1, qs, SSD_W), bw), st, st],
        out_shape=[
            jax.ShapeDtypeStruct((bsz, length, SSD_W), BF16),
            jax.ShapeDtypeStruct((bsz, length, SSD_W), BF16),
            jax.ShapeDtypeStruct((bsz,) + hshape, F32),
            jax.ShapeDtypeStruct((bsz,) + hshape, F32),
        ],
        scratch_shapes=[pltpu.VMEM(hshape, F32), pltpu.VMEM(hshape, F32)],
        compiler_params=_cparams(("parallel", "arbitrary")),
        name="ssd",
    )(xbc, dta, dtat, xbc, dta, dtat, h0f, h0b)


def _outproj_kernel(x_ref, mod_ref, g_ref, hy_ref, yf_ref, yb_ref, xs_ref, z_ref, sc_ref, dexp_ref, mg_ref,
                    why_ref, wssd_ref, wsc_ref, o_ref):
    x = x_ref[0]
    mod = mod_ref[0]
    mg = mg_ref[...]
    y_hy = _rms(hy_ref[0], mg[:, :HY_W]).astype(BF16)
    y = yf_ref[0].astype(F32) + yb_ref[0].astype(F32) + dexp_ref[...] * xs_ref[0].astype(F32)
    y_ssd = _rms(y * _silu(z_ref[0].astype(F32)), mg[:, HY_W:HY_W + SSD_W]).astype(BF16)
    o = _dot(y_hy, why_ref[...]) + _dot(y_ssd, wssd_ref[...]) + _dot(sc_ref[0], wsc_ref[...])
    o_ref[0] = x + _rms(o, mod[5:6] * g_ref[3:4])


def _outproj(x, mod, g, hy, yf, yb, xbc, z, sc, lw, tm):
    bsz, length, d = x.shape
    per_batch = mod.shape[0] != 1
    tok = lambda w: pl.BlockSpec((1, tm, w), lambda b, t: (b, t, 0))
    consts = (lw["d_exp"], lw["mix_gain"], lw["w_out_hy"], lw["w_out_ssd"], lw["w_out_sc"])
    return pl.pallas_call(
        _outproj_kernel,
        grid=(bsz, length // tm),
        in_specs=[
            tok(d),
            pl.BlockSpec((1, N_MOD, d), (lambda b, t: (b, 0, 0)) if per_batch else (lambda b, t: (0, 0, 0))),
            _const_spec(g.shape),
            tok(HY_W), tok(SSD_W), tok(SSD_W), tok(SSD_W), tok(SSD_W), tok(SC_W),
        ] + [_const_spec(a.shape) for a in consts],
        out_specs=tok(d),
        out_shape=jax.ShapeDtypeStruct(x.shape, F32),
        compiler_params=_cparams(("parallel", "parallel")),
        name="outproj",
    )(x, mod, g, hy, yf, yb, xbc, z, sc, *consts)


def _mixer(xs, mod, g, lw, kspec, dft, seg, tm, p, q, h0f, h0b, batch_shape, want_out):
    bsz, length = batch_shape
    hy, z, xbc, sc, dta, dtat = _inproj(xs, mod, g, lw, seg, tm, q)
    if xs.shape[0] != bsz:
        dtat = dtat.reshape(2 * N_DT, bsz, length).transpose(1, 0, 2)
    seq = lambda a: a.reshape(bsz, length, a.shape[-1])
    yf, yb, hf, hb = _ssd(seq(xbc), seq(dta), dtat, h0f, h0b, q, _pick(length // q, (SSD_CPS, 2, 1)))
    if not want_out:
        return None, hf, hb
    fwd, inv = dft
    nct = HY_W // HY_CT
    hy_s = seq(hy)
    fwd_b, inv_b = fwd.astype(BF16), inv.astype(BF16)
    z1 = _hyconv(hy_s, 0, hy_s, nct, kspec, 0, lw["hy_bias"][0:1], fwd_b, inv_b, p)
    z2 = _hyconv(z1, 0, hy_s, 2 * nct, kspec, nct, lw["hy_bias"][1:2], fwd_b, inv_b, p)
    tok = lambda a: a.reshape(xs.shape[0], xs.shape[1], a.shape[-1])
    out = _outproj(xs, mod, g, tok(z2), tok(yf), tok(yb), xbc, z, sc, lw, tm)
    return out, hf, hb


def _layer_weights(l, w_in, w_out, hy_conv_w, hy_conv_b, hy_bias, ssd_conv_w, ssd_conv_b, ssd_a_log, ssd_dt_bias,
                   ssd_d, sc_conv_w, mix_gain):
    o_dt = HY_IN + SSD_W + SSD_XBC
    wl = w_in[l]
    return {
        "w_in": jnp.concatenate([wl[:, :o_dt], wl[:, o_dt + N_DT:]], axis=1).astype(BF16),
        "w_dtt": wl[:, o_dt:o_dt + N_DT].astype(BF16).T,
        "hy_conv_w": hy_conv_w[l],
        "hy_conv_b": hy_conv_b[l][None],
        "ssd_conv_w": ssd_conv_w[l],
        "ssd_conv_b": ssd_conv_b[l][None],
        "sc_conv_w": sc_conv_w[l],
        "sc_gain": mix_gain[l][None, HY_W + SSD_W:],
        "dt_bias_t": ssd_dt_bias[l].reshape(N_DT, 1),
        "hy_bias": hy_bias[l],
        "a2_col": (-jnp.exp(ssd_a_log[l]) * math.log2(math.e)).reshape(N_DT, 1),
        "d_exp": jnp.repeat(ssd_d[l], SSD_HEAD_DIM)[None],
        "mix_gain": mix_gain[l][None],
        "w_out_hy": w_out[l, :HY_W].astype(BF16),
        "w_out_ssd": w_out[l, HY_W:HY_W + SSD_W].astype(BF16),
        "w_out_sc": w_out[l, HY_W + SSD_W:].astype(BF16),
    }


def _pick(n, prefs):
    for v in prefs:
        if n % v == 0:
            return v
    return n


def kernel(x, c, ctx, c_ctx, w_mod, b_mod, norm_g, ffn_w_in, ffn_w_out, w_in, w_out, hy_conv_w, hy_conv_b, hy_fw1,
           hy_fb1, hy_fw2, hy_fb2, hy_fw3, hy_fb3, hy_fw4, hy_freq, hy_bias, ssd_conv_w, ssd_conv_b, ssd_a_log,
           ssd_dt_bias, ssd_d, sc_conv_w, mix_gain):
    bsz, n_lat, d = x.shape
    _, ctx_len, _ = ctx.shape
    depth = w_mod.shape[0]
    dff = ffn_w_out.shape[2]

    rows = -(-(bsz + 1) // 8) * 8
    cvec = jnp.zeros((rows, d), F32).at[:bsz].set(c).at[bsz].set(c_ctx)
    mod_all = _mod_table(cvec, w_mod, b_mod).reshape(depth, rows, N_MOD, d)

    tm = _pick(n_lat, (512, 256, 128, GRID_W))
    p_lat = _pick(n_lat, (512, 256, 128))
    q_lat = _pick(n_lat, (256, 128))
    xc = ctx.reshape(1, bsz * ctx_len, d)
    tm_c = ctx_len * _pick(bsz, (2, 1)) if ctx_len < 512 else _pick(ctx_len, (512, 256))
    p_ctx = _pick(ctx_len, (512, 256, 128))
    q_ctx = _pick(ctx_len, (256, 128))
    dft_lat = _dft_mats(p_lat)
    dft_ctx = _dft_mats(p_ctx)

    h_zero = jnp.zeros((bsz, SSD_GROUPS, SSD_STATE, SSD_W // SSD_GROUPS), F32)
    for l in range(depth):
        last = l == depth - 1
        mod_x = mod_all[l, :bsz]
        mod_c = mod_all[l, bsz:bsz + 1]
        g = norm_g[l]
        lw = _layer_weights(l, w_in, w_out, hy_conv_w, hy_conv_b, hy_bias, ssd_conv_w, ssd_conv_b, ssd_a_log,
                            ssd_dt_bias, ssd_d, sc_conv_w, mix_gain)
        fw = {"fw1": hy_fw1[l], "fb1": hy_fb1[l], "fw2": hy_fw2[l], "fb2": hy_fb2[l], "fw3": hy_fw3[l],
              "fb3": hy_fb3[l], "fw4": hy_fw4[l], "freq": hy_freq[l]}
        ffn_w = [(ffn_w_in[l, i, :, :dff].astype(BF16), ffn_w_in[l, i, :, dff:].astype(BF16),
                  ffn_w_out[l, i].astype(BF16)) for i in range(2)]

        x = _ffn(x, mod_x, g, *ffn_w[0], 0, tm)
        xc = _ffn(xc, mod_c, g, *ffn_w[0], 0, tm_c)

        ks_ctx = None
        if not last:
            ks_ctx = _hyena_spectra(_hyena_filter(fw, ctx_len, p_ctx), dft_ctx[0], p_ctx)
        y_c, h_f, h_b = _mixer(xc, mod_c, g, lw, ks_ctx, dft_ctx, ctx_len, tm_c, p_ctx, q_ctx, h_zero, h_zero,
                               (bsz, ctx_len), not last)
        ks_lat = _hyena_spectra(_hyena_filter(fw, n_lat, p_lat), dft_lat[0], p_lat)
        x, _, _ = _mixer(x, mod_x, g, lw, ks_lat, dft_lat, GRID_W, tm, p_lat, q_lat, h_f, h_b, (bsz, n_lat), True)
        x = _ffn(x, mod_x, g, *ffn_w[1], 2, tm)
        if not last:
            xc = _ffn(y_c, mod_c, g, *ffn_w[1], 2, tm_c)
    return x
```

```python
import functools
import math

import jax
import jax.numpy as jnp
import numpy as np
from jax import lax
from jax.experimental import pallas as pl
from jax.experimental.pallas import tpu as pltpu

GRID_W = 64
EPS = 1e-6
FFN_RES_W = 0.5
N_MOD = 9
HY_W = 256
SSD_HEADS = 8
SSD_HEAD_DIM = 64
SSD_GROUPS = 2
SSD_STATE = 128
SC_W = 256
SHORT_K = 3
HY_ORDER = 2
HY_EMB = 33
HY_FAST = 0.3
HY_SLOW = 1.5
HY_TARGET = 1e-2

SSD_W = SSD_HEADS * SSD_HEAD_DIM
HY_IN = 3 * HY_W
SSD_XBC = SSD_W + 2 * SSD_GROUPS * SSD_STATE
SC_IN = 3 * SC_W
N_DT = 2 * SSD_HEADS

LANE = 128
SUBLANE = 8
MXU_TILE = 256
FFN_CHUNK = 3 * MXU_TILE
INPROJ_CHUNK = 2 * MXU_TILE
SSD_CPS = 4
VMEM_LIMIT = 56 * 1024 * 1024

F32 = jnp.float32
BF16 = jnp.bfloat16


def _cparams(sem):
    return pltpu.CompilerParams(dimension_semantics=sem, vmem_limit_bytes=VMEM_LIMIT)


def _const_spec(shape):
    nd = len(shape)
    return pl.BlockSpec(shape, lambda *_: (0,) * nd, pipeline_mode=pl.Buffered(1))


def _dot(a, b):
    return jnp.dot(a, b, preferred_element_type=F32)


def _split2(a):
    hi = a.astype(BF16)
    lo = (a - hi.astype(F32)).astype(BF16)
    return hi, lo


def _split3(a):
    hi = a.astype(BF16)
    r = a - hi.astype(F32)
    mid = r.astype(BF16)
    lo = (r - mid.astype(F32)).astype(BF16)
    return hi, mid, lo


def _dot_hp(a, b):
    ah, al = _split2(a)
    bh, bl = _split2(b)
    return _dot(ah, bh) + (_dot(ah, bl) + _dot(al, bh))


def _silu(x):
    hx = 0.5 * x
    return hx + hx * jnp.tanh(hx)


def _rms(x, g):
    ms = jnp.mean(x * x, axis=-1, keepdims=True)
    return x * (lax.rsqrt(ms + EPS) * g)


def _mod_kernel(c_ref, w_ref, b_ref, o_ref):
    o_ref[0] = _dot_hp(_silu(c_ref[...]), w_ref[0]) + b_ref[0]


def _mod_table(cvec, w_mod, b_mod):
    depth, d, nm = w_mod.shape
    r = cvec.shape[0]
    tn = 1024 if nm % 1024 == 0 else d
    return pl.pallas_call(
        _mod_kernel,
        grid=(depth, nm // tn),
        in_specs=[
            pl.BlockSpec((r, d), lambda l, n: (0, 0)),
            pl.BlockSpec((1, d, tn), lambda l, n: (l, 0, n)),
            pl.BlockSpec((1, 1, tn), lambda l, n: (l, 0, n)),
        ],
        out_specs=pl.BlockSpec((1, r, tn), lambda l, n: (l, 0, n)),
        out_shape=jax.ShapeDtypeStruct((depth, r, nm), F32),
        compiler_params=_cparams(("parallel", "parallel")),
        name="mod_table",
    )(cvec, w_mod, b_mod.reshape(depth, 1, nm))


def _ffn_rows(x, mod, g, wg_ref, wu_ref, wo_ref, sub, bounds):
    shift, scale, gate = mod[3 * sub:3 * sub + 1], mod[3 * sub + 1:3 * sub + 2], mod[3 * sub + 2:3 * sub + 3]
    g_pre, g_post = g[2 * sub:2 * sub + 1], g[2 * sub + 1:2 * sub + 2]
    h = (_rms(x, g_pre * (1.0 + scale)) + shift).astype(BF16)
    acc = None
    for lo, hi in zip(bounds[:-1], bounds[1:]):
        gg = _dot(h, wg_ref[:, lo:hi])
        uu = _dot(h, wu_ref[:, lo:hi])
        a = (_silu(gg) * uu).astype(BF16)
        part = _dot(a, wo_ref[lo:hi, :])
        acc = part if acc is None else acc + part
    return x + _rms(acc, FFN_RES_W * gate * g_post)


def _ffn_kernel(x_ref, mod_ref, g_ref, wg_ref, wu_ref, wo_ref, o_ref, *, sub, bounds):
    o_ref[0] = _ffn_rows(x_ref[0], mod_ref[0], g_ref[...], wg_ref, wu_ref, wo_ref, sub, bounds)


def _ffn_bounds(dff, width):
    return tuple(range(0, dff, width)) + (dff,)


def _ffn(x, mod, g, wg, wu, wo, sub, tm, width=FFN_CHUNK):
    bsz, length, d = x.shape
    dff = wg.shape[1]
    per_batch = mod.shape[0] != 1
    return pl.pallas_call(
        functools.partial(_ffn_kernel, sub=sub, bounds=_ffn_bounds(dff, width)),
        grid=(bsz, length // tm),
        in_specs=[
            pl.BlockSpec((1, tm, d), lambda b, t: (b, t, 0)),
            pl.BlockSpec((1, N_MOD, d), (lambda b, t: (b, 0, 0)) if per_batch else (lambda b, t: (0, 0, 0))),
            _const_spec(g.shape),
            _const_spec(wg.shape),
            _const_spec(wu.shape),
            _const_spec(wo.shape),
        ],
        out_specs=pl.BlockSpec((1, tm, d), lambda b, t: (b, t, 0)),
        out_shape=jax.ShapeDtypeStruct(x.shape, F32),
        compiler_params=_cparams(("parallel", "parallel")),
        name="ffn",
    )(x, mod, g, wg, wu, wo)


def _conv3(a, w, seg):
    n, c = a.shape
    a4 = a.reshape(n // seg, seg // SUBLANE, SUBLANE, c)
    sub = lax.broadcasted_iota(jnp.int32, a4.shape, 2)
    edge = jnp.zeros((n // seg, 1, SUBLANE, c), F32)
    down = pltpu.roll(a4, 1, 2)
    prev = jnp.where(sub == 0, jnp.concatenate([edge, down[:, :-1]], axis=1), down)
    up = pltpu.roll(a4, SUBLANE - 1, 2)
    nxt = jnp.where(sub == SUBLANE - 1, jnp.concatenate([up[:, 1:], edge], axis=1), up)
    return (prev.reshape(n, c) * w[0:1] + a * w[1:2]) + nxt.reshape(n, c) * w[2:3]


def _chunk_scan(da, q, axis):
    n = da.shape[axis]
    pos = lax.broadcasted_iota(jnp.int32, da.shape, axis) % q
    head = lax.broadcasted_iota(jnp.int32, da.shape, 1 - axis)
    up, down = da, da
    s = 1
    while s < q:
        up = up + jnp.where(pos >= s, pltpu.roll(up, s, axis), 0.0)
        down = down + jnp.where(pos < q - s, pltpu.roll(down, n - s, axis), 0.0)
        s *= 2
    return jnp.where(head < SSD_HEADS, up, down)


def _inproj_kernel(x_ref, mod_ref, g_ref, w_ref, wdtt_ref, hyw_ref, hyb_ref, sw_ref, sb_ref, scw_ref,
                   scg_ref, dtbt_ref, a2c_ref, hy_ref, z_ref, xbc_ref, sc_ref, dta_ref, dtat_ref, *, seg, q):
    x = x_ref[0]
    mod = mod_ref[0]
    h = (_rms(x, g_ref[2:3] * (1.0 + mod[4:5])) + mod[3:4]).astype(BF16)
    dtt = lax.dot_general(wdtt_ref[...], h, (((1,), (1,)), ((), ())), preferred_element_type=F32)
    dtt = jax.nn.softplus(dtt + dtbt_ref[...])
    rowform = jnp.concatenate([dtt, _chunk_scan(dtt * a2c_ref[...], q, 1)], axis=0)
    dtat_ref[0] = rowform
    pad = jnp.zeros((LANE - 2 * N_DT, rowform.shape[1]), F32)
    dta_ref[0] = jnp.concatenate([rowform, pad], axis=0).T[:, :2 * N_DT]
    o_z, o_xbc, o_sc = HY_IN, HY_IN + SSD_W, HY_IN + SSD_W + SSD_XBC

    def post_hy(p, lo, hi):
        hy_ref[0, :, lo:hi] = _conv3(p, hyw_ref[:, lo:hi], seg) + hyb_ref[:, lo:hi]

    def post_z(p, lo, hi):
        z_ref[0, :, lo - o_z:hi - o_z] = p.astype(BF16)

    def post_xbc(p, lo, hi):
        lo, hi = lo - o_xbc, hi - o_xbc
        xbc_ref[0, :, lo:hi] = _silu(_conv3(p, sw_ref[:, lo:hi], seg) + sb_ref[:, lo:hi]).astype(BF16)

    def post_sc(p, lo, hi):
        gb, gc, hx = p[:, :SC_W], p[:, SC_W:2 * SC_W], p[:, 2 * SC_W:]
        sc_ref[0] = _rms(gb * _conv3(gc * hx, scw_ref[...], seg), scg_ref[...]).astype(BF16)

    stages = []
    for post, lo, hi in ((post_hy, 0, o_z), (post_z, o_z, o_xbc), (post_xbc, o_xbc, o_sc)):
        stages += [(post, c, min(c + INPROJ_CHUNK, hi)) for c in range(lo, hi, INPROJ_CHUNK)]
    stages.append((post_sc, o_sc, o_sc + SC_IN))
    pending = None
    for post, lo, hi in stages:
        p = _dot(h, w_ref[:, lo:hi])
        if pending is not None:
            pending[0](*pending[1:])
        pending = (post, p, lo, hi)
    pending[0](*pending[1:])


def _inproj(x, mod, g, lw, seg, tm, q):
    bsz, length, d = x.shape
    per_batch = mod.shape[0] != 1
    tok = lambda w: pl.BlockSpec((1, tm, w), lambda b, t: (b, t, 0))
    consts = (lw["w_in"], lw["w_dtt"], lw["hy_conv_w"], lw["hy_conv_b"], lw["ssd_conv_w"],
              lw["ssd_conv_b"], lw["sc_conv_w"], lw["sc_gain"], lw["dt_bias_t"], lw["a2_col"])
    return pl.pallas_call(
        functools.partial(_inproj_kernel, seg=seg, q=q),
        grid=(bsz, length // tm),
        in_specs=[
            tok(d),
            pl.BlockSpec((1, N_MOD, d), (lambda b, t: (b, 0, 0)) if per_batch else (lambda b, t: (0, 0, 0))),
            _const_spec(g.shape),
        ] + [_const_spec(a.shape) for a in consts],
        out_specs=[tok(HY_IN), tok(SSD_W), tok(SSD_XBC), tok(SC_W), tok(2 * N_DT),
                   pl.BlockSpec((1, 2 * N_DT, tm), lambda b, t: (b, 0, t))],
        out_shape=[
            jax.ShapeDtypeStruct((bsz, length, HY_IN), F32),
            jax.ShapeDtypeStruct((bsz, length, SSD_W), BF16),
            jax.ShapeDtypeStruct((bsz, length, SSD_XBC), BF16),
            jax.ShapeDtypeStruct((bsz, length, SC_W), BF16),
            jax.ShapeDtypeStruct((bsz, length, 2 * N_DT), F32),
            jax.ShapeDtypeStruct((bsz, 2 * N_DT, length), F32),
        ],
        compiler_params=_cparams(("parallel", "parallel")),
        name="inproj",
    )(x, mod, g, *consts)


def _dft_mats(p):
    f = np.arange(p, dtype=np.int64)[:, None]
    t = np.arange(p, dtype=np.int64)[None, :]
    ang = (np.pi / (2 * p)) * (((2 * f + 1) * t) % (4 * p)).astype(np.float64)
    fwd = np.concatenate([np.cos(ang), -np.sin(ang)], axis=0)
    return jnp.asarray(fwd, F32), jnp.asarray(fwd.T / p, F32)


def _filter_kernel(fb_ref, dl_ref, w1t_ref, w1c_ref, w1s_ref, b1_ref, w2_ref, b2_ref, w3_ref, b3_ref, w4_ref,
                   fr_ref, k_ref, *, length, tr):
    lag = pl.program_id(0) * tr + lax.broadcasted_iota(jnp.int32, (tr, 1), 0) - length
    n = jnp.abs(lag).astype(F32)
    t = n * (1.0 / (length - 1))
    w = n * (2.0 * math.pi / length)
    ang = fb_ref[...] * w
    fr = fr_ref[...]
    h = t * w1t_ref[...] + _dot_hp(jnp.cos(ang), w1c_ref[...]) - _dot_hp(jnp.sin(ang), w1s_ref[...])
    h = jnp.sin(fr * (h + b1_ref[...]))
    h = jnp.sin(fr * (_dot_hp(h, w2_ref[...]) + b2_ref[...]))
    h = jnp.sin(fr * (_dot_hp(h, w3_ref[...]) + b3_ref[...]))
    decay = jnp.exp(-t * jnp.abs(dl_ref[...]))
    for o in range(HY_ORDER):
        base = o * 2 * HY_W
        kf = _dot_hp(h, w4_ref[:, base:base + HY_W]) * decay
        kb = _dot_hp(h, w4_ref[:, base + HY_W:base + 2 * HY_W]) * decay
        k_ref[:, o * HY_W:(o + 1) * HY_W] = jnp.where(lag >= 0, kf, jnp.where(lag == -length, 0.0, kb))


def _hyena_filter(fw, length, tr):
    bands = (HY_EMB - 1) // 2
    fbands = jnp.linspace(1e-4, bands - 1, bands, dtype=F32)[None, :]
    deltas = jnp.linspace(math.log(HY_TARGET) / HY_SLOW, math.log(HY_TARGET) / HY_FAST, HY_W, dtype=F32)[None, :]
    fw1 = fw["fw1"]
    consts = (fbands, deltas, fw1[0:1], fw1[1:1 + bands], fw1[1 + bands:], fw["fb1"][None], fw["fw2"],
              fw["fb2"][None], fw["fw3"], fw["fb3"][None], fw["fw4"], fw["freq"][None])
    return pl.pallas_call(
        functools.partial(_filter_kernel, length=length, tr=tr),
        grid=(2 * length // tr,),
        in_specs=[pl.BlockSpec(a.shape, lambda i: (0, 0)) for a in consts],
        out_specs=pl.BlockSpec((tr, HY_ORDER * HY_W), lambda i: (i, 0)),
        out_shape=jax.ShapeDtypeStruct((2 * length, HY_ORDER * HY_W), F32),
        compiler_params=_cparams(("parallel",)),
        name="hyena_filter",
    )(*consts)


def _kspec_kernel(f_ref, k0_ref, k1_ref, o_ref, *, p):
    fwd = f_ref[...]
    a = _dot_hp(fwd, k1_ref[...])
    a1 = _dot_hp(fwd, k0_ref[...])
    a1_re = a1[:p] - k0_ref[0:1, :]
    a1_im = a1[p:]
    sign = 1.0 - 2.0 * (lax.broadcasted_iota(jnp.int32, (p, 1), 0) % 2).astype(F32)
    o_ref[0, :p, :] = a[:p] - sign * a1_im
    o_ref[0, p:, :] = a[p:] + sign * a1_re


def _hyena_spectra(kfull, fwd, p):
    nb = kfull.shape[0] // (2 * p)
    cols = kfull.shape[1]
    return pl.pallas_call(
        functools.partial(_kspec_kernel, p=p),
        grid=(2 * nb - 1,),
        in_specs=[
            pl.BlockSpec(fwd.shape, lambda i: (0, 0)),
            pl.BlockSpec((p, cols), lambda i: (i, 0)),
            pl.BlockSpec((p, cols), lambda i: (i + 1, 0)),
        ],
        out_specs=pl.BlockSpec((1, 2 * p, cols), lambda i: (i, 0, 0)),
        out_shape=jax.ShapeDtypeStruct((2 * nb - 1, 2 * p, cols), F32),
        compiler_params=_cparams(("parallel",)),
        name="hyena_spectra",
    )(fwd, kfull, kfull)


HY_CT = 128
HY_NB = 2
HY_RT = 32


def _hyconv_kernel(zin_ref, gate_ref, k_ref, bias_ref, fwd_ref, inv_ref, o_ref, zf_ref, ya_ref, yb_ref, *, nb, p):
    fwd = fwd_ref[...]
    for j in range(nb):
        zb = jnp.concatenate([zin_ref[s, j * p:(j + 1) * p, :] for s in range(HY_NB)], axis=1).astype(BF16)
        zf_ref[j] = _dot(fwd, zb)

    def mac(i, y_ref):
        for rt in range(p // HY_RT):
            r0 = rt * HY_RT
            acc_re = [None] * HY_NB
            acc_im = [None] * HY_NB
            for j in range(nb):
                d = i - j + (nb - 1)
                kre = k_ref[d, r0:r0 + HY_RT, :]
                kim = k_ref[d, p + r0:p + r0 + HY_RT, :]
                zre = zf_ref[j, r0:r0 + HY_RT, :]
                zim = zf_ref[j, p + r0:p + r0 + HY_RT, :]
                for s in range(HY_NB):
                    zr = zre[:, s * HY_CT:(s + 1) * HY_CT]
                    zi = zim[:, s * HY_CT:(s + 1) * HY_CT]
                    re = kre * zr - kim * zi
                    im = kre * zi + kim * zr
                    acc_re[s] = re if j == 0 else acc_re[s] + re
                    acc_im[s] = im if j == 0 else acc_im[s] + im
            y_ref[r0:r0 + HY_RT, :] = jnp.concatenate(acc_re, axis=1).astype(BF16)
            y_ref[p + r0:p + r0 + HY_RT, :] = jnp.concatenate(acc_im, axis=1).astype(BF16)

    def finish(i, y_ref):
        y = _dot(inv_ref[...], y_ref[...])
        rows = pl.ds(pl.multiple_of(i * p, p), p)
        for s in range(HY_NB):
            zs = zin_ref[s, rows, :]
            o_ref[s, rows, :] = gate_ref[s, rows, :] * (y[:, s * HY_CT:(s + 1) * HY_CT] + bias_ref[...] * zs)

    mac(0, ya_ref)
    for i in range(1, nb):
        mac(i, yb_ref if i % 2 else ya_ref)
        finish(i - 1, ya_ref if i % 2 else yb_ref)
    finish(nb - 1, ya_ref if nb % 2 else yb_ref)


def _hyconv(zin, zin_col, gate, gate_col, kspec, k_col, bias, fwd, inv, p):
    bsz, length, _ = zin.shape
    nb = length // p
    nct = HY_W // HY_CT
    nd = kspec.shape[0]
    return pl.pallas_call(
        functools.partial(_hyconv_kernel, nb=nb, p=p),
        grid=(nct, bsz // HY_NB),
        in_specs=[
            pl.BlockSpec((HY_NB, length, HY_CT), lambda c, b: (b, 0, zin_col + c)),
            pl.BlockSpec((HY_NB, length, HY_CT), lambda c, b: (b, 0, gate_col + c)),
            pl.BlockSpec((nd, 2 * p, HY_CT), lambda c, b: (0, 0, k_col + c), pipeline_mode=pl.Buffered(1)),
            pl.BlockSpec((1, HY_CT), lambda c, b: (0, c)),
            _const_spec(fwd.shape),
            _const_spec(inv.shape),
        ],
        out_specs=pl.BlockSpec((HY_NB, length, HY_CT), lambda c, b: (b, 0, c)),
        out_shape=jax.ShapeDtypeStruct((bsz, length, HY_W), F32),
        scratch_shapes=[
            pltpu.VMEM((nb, 2 * p, HY_NB * HY_CT), F32),
            pltpu.VMEM((2 * p, HY_NB * HY_CT), BF16),
            pltpu.VMEM((2 * p, HY_NB * HY_CT), BF16),
        ],
        compiler_params=_cparams(("arbitrary", "arbitrary")),
        name="hyconv",
    )(zin, gate, kspec, bias, fwd, inv)


def _ssd_dir(xbc_ref, dta_ref, dtat_ref, h_ref, y_ref, *, q, rev, r0):
    rows = slice(r0, r0 + q)
    n, hp, hd = SSD_STATE, SSD_HEADS // SSD_GROUPS, SSD_HEAD_DIM
    c0 = SSD_HEADS if rev else 0
    end = 0 if rev else q - 1
    li = lax.broadcasted_iota(jnp.int32, (q, q), 0)
    si = lax.broadcasted_iota(jnp.int32, (q, q), 1)
    keep = (li <= si) if rev else (li >= si)
    low = lax.broadcasted_iota(jnp.int32, (q, LANE), 1) < hd
    pairs = hp * hd // LANE
    for g in range(SSD_GROUPS):
        bg = xbc_ref[0, rows, SSD_W + g * n:SSD_W + (g + 1) * n]
        cg = xbc_ref[0, rows, SSD_W + SSD_GROUPS * n + g * n:SSD_W + SSD_GROUPS * n + (g + 1) * n]
        scores = lax.dot_general(cg, bg, (((1,), (1,)), ((), ())), preferred_element_type=F32)
        hg = h_ref[g]
        y_off = _dot(cg, hg.astype(BF16))
        ys, xws, tots = [], [], []
        for t in range(pairs):
            cols = slice(g * hp * hd + t * LANE, g * hp * hd + (t + 1) * LANE)
            acs_b, dt_b = [], []
            for r in (2 * t, 2 * t + 1):
                c = c0 + g * hp + r
                acs_b.append(jnp.broadcast_to(dta_ref[0, rows, N_DT + c:N_DT + c + 1], (q, LANE)))
                dt_b.append(jnp.broadcast_to(dta_ref[0, rows, c:c + 1], (q, LANE)))
            acs2 = jnp.where(low, acs_b[0], acs_b[1])
            dt2 = jnp.where(low, dt_b[0], dt_b[1])
            xdt = xbc_ref[0, rows, cols].astype(F32) * dt2
            yd = None
            for i, r in enumerate((2 * t, 2 * t + 1)):
                c = c0 + g * hp + r
                seg = jnp.concatenate([acs_b[i]] * (q // LANE), axis=1) - dtat_ref[0, N_DT + c:N_DT + c + 1, rows]
                m = (jnp.exp2(jnp.where(keep, seg, -1e30)) * scores).astype(BF16)
                xh = jnp.where(low if i == 0 else jnp.logical_not(low), xdt, 0.0).astype(BF16)
                part = _dot(m, xh)
                yd = part if yd is None else yd + part
            ys.append(yd + jnp.exp2(acs2) * y_off[:, t * LANE:(t + 1) * LANE])
            tot = acs2[end:end + 1, :]
            xws.append((xdt * jnp.exp2(tot - acs2)).astype(BF16))
            tots.append(tot)
        y_ref[0, rows, g * hp * hd:(g + 1) * hp * hd] = jnp.concatenate(ys, axis=1).astype(y_ref.dtype)
        upd = lax.dot_general(bg, jnp.concatenate(xws, axis=1), (((0,), (0,)), ((), ())),
                              preferred_element_type=F32)
        h_ref[g] = hg * jnp.exp2(jnp.concatenate(tots, axis=1)) + upd


def _ssd_kernel(xf_ref, dtaf_ref, dtatf_ref, xb_ref, dtab_ref, dtatb_ref, h0f_ref, h0b_ref,
                yf_ref, yb_ref, hf_out, hb_out, hf_ref, hb_ref, *, q):
    k = pl.program_id(1)

    @pl.when(k == 0)
    def _():
        hf_ref[...] = h0f_ref[0]
        hb_ref[...] = h0b_ref[0]

    cps = xf_ref.shape[1] // q
    for j in range(cps):
        _ssd_dir(xf_ref, dtaf_ref, dtatf_ref, hf_ref, yf_ref, q=q, rev=False, r0=j * q)
        _ssd_dir(xb_ref, dtab_ref, dtatb_ref, hb_ref, yb_ref, q=q, rev=True, r0=(cps - 1 - j) * q)

    @pl.when(k == pl.num_programs(1) - 1)
    def _():
        hf_out[0] = hf_ref[...]
        hb_out[0] = hb_ref[...]


def _ssd(xbc, dta, dtat, h0f, h0b, q, cps):
    bsz, length, _ = xbc.shape
    qs = q * cps
    nc = length // qs
    hshape = (SSD_GROUPS, SSD_STATE, SSD_W // SSD_GROUPS)
    fw = lambda b, k: (b, k, 0)
    bw = lambda b, k: (b, nc - 1 - k, 0)
    st = pl.BlockSpec((1,) + hshape, lambda b, k: (b, 0, 0, 0))
    return pl.pallas_call(
        functools.partial(_ssd_kernel, q=q),
        grid=(bsz, nc),
        in_specs=[
            pl.BlockSpec((1, qs, SSD_XBC), fw),
            pl.BlockSpec((1, qs, 2 * N_DT), fw),
            pl.BlockSpec((1, 2 * N_DT, qs), lambda b, k: (b, 0, k)),
            pl.BlockSpec((1, qs, SSD_XBC), bw),
            pl.BlockSpec((1, qs, 2 * N_DT), bw),
            pl.BlockSpec((1, 2 * N_DT, qs), lambda b, k: (b, 0, nc - 1 - k)),
            st, st,
        ],
        out_specs=[pl.BlockSpec((1, qs, SSD_W), fw), pl.BlockSpec((1, qs, SSD_W), bw), st, st],
        out_shape=[
            jax.ShapeDtypeStruct((bsz, length, SSD_W), BF16),
            jax.ShapeDtypeStruct((bsz, length, SSD_W), BF16),
            jax.ShapeDtypeStruct((bsz,) + hshape, F32),
            jax.ShapeDtypeStruct((bsz,) + hshape, F32),
        ],
        scratch_shapes=[pltpu.VMEM(hshape, F32), pltpu.VMEM(hshape, F32)],
        compiler_params=_cparams(("parallel", "arbitrary")),
        name="ssd",
    )(xbc, dta, dtat, xbc, dta, dtat, h0f, h0b)


def _outproj_kernel(x_ref, mod_ref, g_ref, hy_ref, yf_ref, yb_ref, xs_ref, z_ref, sc_ref, dexp_ref, mg_ref,
                    why_ref, wssd_ref, wsc_ref, wg_ref, wu_ref, wo_ref, o_ref, *, bounds):
    x = x_ref[0]
    mod = mod_ref[0]
    mg = mg_ref[...]
    y_hy = _rms(hy_ref[0], mg[:, :HY_W]).astype(BF16)
    y = yf_ref[0].astype(F32) + yb_ref[0].astype(F32) + dexp_ref[...] * xs_ref[0].astype(F32)
    y_ssd = _rms(y * _silu(z_ref[0].astype(F32)), mg[:, HY_W:HY_W + SSD_W]).astype(BF16)
    o = _dot(y_hy, why_ref[...]) + _dot(y_ssd, wssd_ref[...]) + _dot(sc_ref[0], wsc_ref[...])
    x = x + _rms(o, mod[5:6] * g_ref[3:4])
    o_ref[0] = _ffn_rows(x, mod, g_ref[...], wg_ref, wu_ref, wo_ref, 2, bounds)


def _outproj(x, mod, g, hy, yf, yb, xbc, z, sc, lw, ffn_w, tm):
    bsz, length, d = x.shape
    per_batch = mod.shape[0] != 1
    tok = lambda w: pl.BlockSpec((1, tm, w), lambda b, t: (b, t, 0))
    consts = (lw["d_exp"], lw["mix_gain"], lw["w_out_hy"], lw["w_out_ssd"], lw["w_out_sc"]) + tuple(ffn_w)
    return pl.pallas_call(
        functools.partial(_outproj_kernel, bounds=_ffn_bounds(ffn_w[0].shape[1], FFN_CHUNK)),
        grid=(bsz, length // tm),
        in_specs=[
            tok(d),
            pl.BlockSpec((1, N_MOD, d), (lambda b, t: (b, 0, 0)) if per_batch else (lambda b, t: (0, 0, 0))),
            _const_spec(g.shape),
            tok(HY_W), tok(SSD_W), tok(SSD_W), tok(SSD_W), tok(SSD_W), tok(SC_W),
        ] + [_const_spec(a.shape) for a in consts],
        out_specs=tok(d),
        out_shape=jax.ShapeDtypeStruct(x.shape, F32),
        compiler_params=_cparams(("parallel", "parallel")),
        name="outproj",
    )(x, mod, g, hy, yf, yb, xbc, z, sc, *consts)


def _mixer(xs, mod, g, lw, ffn_w, kspec, dft, seg, tm, p, q, h0f, h0b, batch_shape, want_out):
    bsz, length = batch_shape
    hy, z, xbc, sc, dta, dtat = _inproj(xs, mod, g, lw, seg, tm, q)
    if xs.shape[0] != bsz:
        dtat = dtat.reshape(2 * N_DT, bsz, length).transpose(1, 0, 2)
    seq = lambda a: a.reshape(bsz, length, a.shape[-1])
    yf, yb, hf, hb = _ssd(seq(xbc), seq(dta), dtat, h0f, h0b, q, _pick(length // q, (SSD_CPS, 2, 1)))
    if not want_out:
        return None, hf, hb
    fwd, inv = dft
    nct = HY_W // HY_CT
    hy_s = seq(hy)
    fwd_b, inv_b = fwd.astype(BF16), inv.astype(BF16)
    z1 = _hyconv(hy_s, 0, hy_s, nct, kspec, 0, lw["hy_bias"][0:1], fwd_b, inv_b, p)
    z2 = _hyconv(z1, 0, hy_s, 2 * nct, kspec, nct, lw["hy_bias"][1:2], fwd_b, inv_b, p)
    tok = lambda a: a.reshape(xs.shape[0], xs.shape[1], a.shape[-1])
    out = _outproj(xs, mod, g, tok(z2), tok(yf), tok(yb), xbc, z, sc, lw, ffn_w, tm)
    return out, hf, hb


def _layer_weights(l, w_in, w_out, hy_conv_w, hy_conv_b, hy_bias, ssd_conv_w, ssd_conv_b, ssd_a_log, ssd_dt_bias,
                   ssd_d, sc_conv_w, mix_gain):
    o_dt = HY_IN + SSD_W + SSD_XBC
    wl = w_in[l]
    return {
        "w_in": jnp.concatenate([wl[:, :o_dt], wl[:, o_dt + N_DT:]], axis=1).astype(BF16),
        "w_dtt": wl[:, o_dt:o_dt + N_DT].astype(BF16).T,
        "hy_conv_w": hy_conv_w[l],
        "hy_conv_b": hy_conv_b[l][None],
        "ssd_conv_w": ssd_conv_w[l],
        "ssd_conv_b": ssd_conv_b[l][None],
        "sc_conv_w": sc_conv_w[l],
        "sc_gain": mix_gain[l][None, HY_W + SSD_W:],
        "dt_bias_t": ssd_dt_bias[l].reshape(N_DT, 1),
        "hy_bias": hy_bias[l],
        "a2_col": (-jnp.exp(ssd_a_log[l]) * math.log2(math.e)).reshape(N_DT, 1),
        "d_exp": jnp.repeat(ssd_d[l], SSD_HEAD_DIM)[None],
        "mix_gain": mix_gain[l][None],
        "w_out_hy": w_out[l, :HY_W].astype(BF16),
        "w_out_ssd": w_out[l, HY_W:HY_W + SSD_W].astype(BF16),
        "w_out_sc": w_out[l, HY_W + SSD_W:].astype(BF16),
    }


def _pick(n, prefs):
    for v in prefs:
        if n % v == 0:
            return v
    return n


def kernel(x, c, ctx, c_ctx, w_mod, b_mod, norm_g, ffn_w_in, ffn_w_out, w_in, w_out, hy_conv_w, hy_conv_b, hy_fw1,
           hy_fb1, hy_fw2, hy_fb2, hy_fw3, hy_fb3, hy_fw4, hy_freq, hy_bias, ssd_conv_w, ssd_conv_b, ssd_a_log,
           ssd_dt_bias, ssd_d, sc_conv_w, mix_gain):
    bsz, n_lat, d = x.shape
    _, ctx_len, _ = ctx.shape
    depth = w_mod.shape[0]
    dff = ffn_w_out.shape[2]

    rows = -(-(bsz + 1) // 8) * 8
    cvec = jnp.zeros((rows, d), F32).at[:bsz].set(c).at[bsz].set(c_ctx)
    mod_all = _mod_table(cvec, w_mod, b_mod).reshape(depth, rows, N_MOD, d)

    tm = _pick(n_lat, (512, 256, 128, GRID_W))
    p_lat = _pick(n_lat, (512, 256, 128))
    q_lat = _pick(n_lat, (256, 128))
    xc = ctx.reshape(1, bsz * ctx_len, d)
    tm_c = ctx_len * _pick(bsz, (2, 1)) if ctx_len < 512 else _pick(ctx_len, (512, 256))
    p_ctx = _pick(ctx_len, (512, 256, 128))
    q_ctx = _pick(ctx_len, (256, 128))
    dft_lat = _dft_mats(p_lat)
    dft_ctx = _dft_mats(p_ctx)

    h_zero = jnp.zeros((bsz, SSD_GROUPS, SSD_STATE, SSD_W // SSD_GROUPS), F32)
    for l in range(depth):
        last = l == depth - 1
        mod_x = mod_all[l, :bsz]
        mod_c = mod_all[l, bsz:bsz + 1]
        g = norm_g[l]
        lw = _layer_weights(l, w_in, w_out, hy_conv_w, hy_conv_b, hy_bias, ssd_conv_w, ssd_conv_b, ssd_a_log,
                            ssd_dt_bias, ssd_d, sc_conv_w, mix_gain)
        fw = {"fw1": hy_fw1[l], "fb1": hy_fb1[l], "fw2": hy_fw2[l], "fb2": hy_fb2[l], "fw3": hy_fw3[l],
              "fb3": hy_fb3[l], "fw4": hy_fw4[l], "freq": hy_freq[l]}
        ffn_w = [(ffn_w_in[l, i, :, :dff].astype(BF16), ffn_w_in[l, i, :, dff:].astype(BF16),
                  ffn_w_out[l, i].astype(BF16)) for i in range(2)]

        x = _ffn(x, mod_x, g, *ffn_w[0], 0, tm)
        xc = _ffn(xc, mod_c, g, *ffn_w[0], 0, tm_c)

        ks_ctx = None
        if not last:
            ks_ctx = _hyena_spectra(_hyena_filter(fw, ctx_len, p_ctx), dft_ctx[0], p_ctx)
        xc, h_f, h_b = _mixer(xc, mod_c, g, lw, ffn_w[1], ks_ctx, dft_ctx, ctx_len, tm_c, p_ctx, q_ctx, h_zero,
                              h_zero, (bsz, ctx_len), not last)
        ks_lat = _hyena_spectra(_hyena_filter(fw, n_lat, p_lat), dft_lat[0], p_lat)
        x, _, _ = _mixer(x, mod_x, g, lw, ffn_w[1], ks_lat, dft_lat, GRID_W, tm, p_lat, q_lat, h_f, h_b,
                         (bsz, n_lat), True)
    return x
```

```python
import functools
import math

import jax
import jax.numpy as jnp
import numpy as np
from jax import lax
from jax.experimental import pallas as pl
from jax.experimental.pallas import tpu as pltpu

GRID_W = 64
EPS = 1e-6
FFN_RES_W = 0.5
N_MOD = 9
HY_W = 256
SSD_HEADS = 8
SSD_HEAD_DIM = 64
SSD_GROUPS = 2
SSD_STATE = 128
SC_W = 256
SHORT_K = 3
HY_ORDER = 2
HY_EMB = 33
HY_FAST = 0.3
HY_SLOW = 1.5
HY_TARGET = 1e-2

SSD_W = SSD_HEADS * SSD_HEAD_DIM
HY_IN = 3 * HY_W
SSD_XBC = SSD_W + 2 * SSD_GROUPS * SSD_STATE
SC_IN = 3 * SC_W
N_DT = 2 * SSD_HEADS

LANE = 128
SUBLANE = 8
MXU_TILE = 256
FFN_CHUNK = 3 * MXU_TILE
INPROJ_CHUNK = 2 * MXU_TILE
SSD_CPS = 4
VMEM_LIMIT = 56 * 1024 * 1024

F32 = jnp.float32
BF16 = jnp.bfloat16


def _cparams(sem):
    return pltpu.CompilerParams(dimension_semantics=sem, vmem_limit_bytes=VMEM_LIMIT)


def _const_spec(shape):
    nd = len(shape)
    return pl.BlockSpec(shape, lambda *_: (0,) * nd, pipeline_mode=pl.Buffered(1))


def _dot(a, b):
    return jnp.dot(a, b, preferred_element_type=F32)


def _split2(a):
    hi = a.astype(BF16)
    lo = (a - hi.astype(F32)).astype(BF16)
    return hi, lo


def _split3(a):
    hi = a.astype(BF16)
    r = a - hi.astype(F32)
    mid = r.astype(BF16)
    lo = (r - mid.astype(F32)).astype(BF16)
    return hi, mid, lo


def _dot_hp(a, b):
    ah, al = _split2(a)
    bh, bl = _split2(b)
    return _dot(ah, bh) + (_dot(ah, bl) + _dot(al, bh))


def _dot_hp_t(a_t, b):
    dn = (((0,), (0,)), ((), ()))
    ah, al = _split2(a_t)
    bh, bl = _split2(b)
    dg = lambda u, v: lax.dot_general(u, v, dn, preferred_element_type=F32)
    return dg(ah, bh) + (dg(ah, bl) + dg(al, bh))


def _silu(x):
    hx = 0.5 * x
    return hx + hx * jnp.tanh(hx)


def _rms(x, g):
    ms = jnp.mean(x * x, axis=-1, keepdims=True)
    return x * (lax.rsqrt(ms + EPS) * g)


def _mod_kernel(c_ref, w_ref, b_ref, o_ref):
    o_ref[0] = _dot_hp(_silu(c_ref[...]), w_ref[0]) + b_ref[0]


def _mod_table(cvec, w_mod, b_mod):
    depth, d, nm = w_mod.shape
    r = cvec.shape[0]
    tn = 1024 if nm % 1024 == 0 else d
    return pl.pallas_call(
        _mod_kernel,
        grid=(depth, nm // tn),
        in_specs=[
            pl.BlockSpec((r, d), lambda l, n: (0, 0)),
            pl.BlockSpec((1, d, tn), lambda l, n: (l, 0, n)),
            pl.BlockSpec((1, 1, tn), lambda l, n: (l, 0, n)),
        ],
        out_specs=pl.BlockSpec((1, r, tn), lambda l, n: (l, 0, n)),
        out_shape=jax.ShapeDtypeStruct((depth, r, nm), F32),
        compiler_params=_cparams(("parallel", "parallel")),
        name="mod_table",
    )(cvec, w_mod, b_mod.reshape(depth, 1, nm))


def _ffn_rows(x, mod, g, wg_ref, wu_ref, wo_ref, sub, bounds):
    shift, scale, gate = mod[3 * sub:3 * sub + 1], mod[3 * sub + 1:3 * sub + 2], mod[3 * sub + 2:3 * sub + 3]
    g_pre, g_post = g[2 * sub:2 * sub + 1], g[2 * sub + 1:2 * sub + 2]
    h = (_rms(x, g_pre * (1.0 + scale)) + shift).astype(BF16)
    acc = None
    for lo, hi in zip(bounds[:-1], bounds[1:]):
        gg = _dot(h, wg_ref[:, lo:hi])
        uu = _dot(h, wu_ref[:, lo:hi])
        a = (_silu(gg) * uu).astype(BF16)
        part = _dot(a, wo_ref[lo:hi, :])
        acc = part if acc is None else acc + part
    return x + _rms(acc, FFN_RES_W * gate * g_post)


def _ffn_kernel(x_ref, mod_ref, g_ref, wg_ref, wu_ref, wo_ref, o_ref, *, sub, bounds):
    o_ref[0] = _ffn_rows(x_ref[0], mod_ref[0], g_ref[...], wg_ref, wu_ref, wo_ref, sub, bounds)


def _ffn_bounds(dff, width):
    return tuple(range(0, dff, width)) + (dff,)


def _ffn(x, mod, g, wg, wu, wo, sub, tm, width=FFN_CHUNK):
    bsz, length, d = x.shape
    dff = wg.shape[1]
    per_batch = mod.shape[0] != 1
    return pl.pallas_call(
        functools.partial(_ffn_kernel, sub=sub, bounds=_ffn_bounds(dff, width)),
        grid=(bsz, length // tm),
        in_specs=[
            pl.BlockSpec((1, tm, d), lambda b, t: (b, t, 0)),
            pl.BlockSpec((1, N_MOD, d), (lambda b, t: (b, 0, 0)) if per_batch else (lambda b, t: (0, 0, 0))),
            _const_spec(g.shape),
            _const_spec(wg.shape),
            _const_spec(wu.shape),
            _const_spec(wo.shape),
        ],
        out_specs=pl.BlockSpec((1, tm, d), lambda b, t: (b, t, 0)),
        out_shape=jax.ShapeDtypeStruct(x.shape, F32),
        compiler_params=_cparams(("parallel", "parallel")),
        name="ffn",
    )(x, mod, g, wg, wu, wo)


def _conv3(a, w, seg):
    n, c = a.shape
    a4 = a.reshape(n // seg, seg // SUBLANE, SUBLANE, c)
    sub = lax.broadcasted_iota(jnp.int32, a4.shape, 2)
    edge = jnp.zeros((n // seg, 1, SUBLANE, c), F32)
    down = pltpu.roll(a4, 1, 2)
    prev = jnp.where(sub == 0, jnp.concatenate([edge, down[:, :-1]], axis=1), down)
    up = pltpu.roll(a4, SUBLANE - 1, 2)
    nxt = jnp.where(sub == SUBLANE - 1, jnp.concatenate([up[:, 1:], edge], axis=1), up)
    return (prev.reshape(n, c) * w[0:1] + a * w[1:2]) + nxt.reshape(n, c) * w[2:3]


def _chunk_scan(da, q, axis):
    n = da.shape[axis]
    pos = lax.broadcasted_iota(jnp.int32, da.shape, axis) % q
    head = lax.broadcasted_iota(jnp.int32, da.shape, 1 - axis)
    up, down = da, da
    s = 1
    while s < q:
        up = up + jnp.where(pos >= s, pltpu.roll(up, s, axis), 0.0)
        down = down + jnp.where(pos < q - s, pltpu.roll(down, n - s, axis), 0.0)
        s *= 2
    return jnp.where(head < SSD_HEADS, up, down)


def _inproj_kernel(x_ref, mod_ref, g_ref, w_ref, wdtt_ref, hyw_ref, hyb_ref, sw_ref, sb_ref, scw_ref,
                   scg_ref, dtbt_ref, a2c_ref, hy_ref, z_ref, xbc_ref, sc_ref, dta_ref, dtat_ref, *, seg, q):
    x = x_ref[0]
    mod = mod_ref[0]
    h = (_rms(x, g_ref[2:3] * (1.0 + mod[4:5])) + mod[3:4]).astype(BF16)
    dtt = lax.dot_general(wdtt_ref[...], h, (((1,), (1,)), ((), ())), preferred_element_type=F32)
    dtt = jax.nn.softplus(dtt + dtbt_ref[...])
    rowform = jnp.concatenate([dtt, _chunk_scan(dtt * a2c_ref[...], q, 1)], axis=0)
    dtat_ref[0] = rowform
    pad = jnp.zeros((LANE - 2 * N_DT, rowform.shape[1]), F32)
    dta_ref[0] = jnp.concatenate([rowform, pad], axis=0).T[:, :2 * N_DT]
    o_z, o_xbc, o_sc = HY_IN, HY_IN + SSD_W, HY_IN + SSD_W + SSD_XBC

    def post_hy(p, lo, hi):
        hy_ref[0, :, lo:hi] = _conv3(p, hyw_ref[:, lo:hi], seg) + hyb_ref[:, lo:hi]

    def post_z(p, lo, hi):
        z_ref[0, :, lo - o_z:hi - o_z] = p.astype(BF16)

    def post_xbc(p, lo, hi):
        lo, hi = lo - o_xbc, hi - o_xbc
        xbc_ref[0, :, lo:hi] = _silu(_conv3(p, sw_ref[:, lo:hi], seg) + sb_ref[:, lo:hi]).astype(BF16)

    def post_sc(p, lo, hi):
        gb, gc, hx = p[:, :SC_W], p[:, SC_W:2 * SC_W], p[:, 2 * SC_W:]
        sc_ref[0] = _rms(gb * _conv3(gc * hx, scw_ref[...], seg), scg_ref[...]).astype(BF16)

    stages = []
    for post, lo, hi in ((post_hy, 0, o_z), (post_z, o_z, o_xbc), (post_xbc, o_xbc, o_sc)):
        stages += [(post, c, min(c + INPROJ_CHUNK, hi)) for c in range(lo, hi, INPROJ_CHUNK)]
    stages.append((post_sc, o_sc, o_sc + SC_IN))
    pending = None
    for post, lo, hi in stages:
        p = _dot(h, w_ref[:, lo:hi])
        if pending is not None:
            pending[0](*pending[1:])
        pending = (post, p, lo, hi)
    pending[0](*pending[1:])


def _inproj(x, mod, g, lw, seg, tm, q):
    bsz, length, d = x.shape
    per_batch = mod.shape[0] != 1
    tok = lambda w: pl.BlockSpec((1, tm, w), lambda b, t: (b, t, 0))
    consts = (lw["w_in"], lw["w_dtt"], lw["hy_conv_w"], lw["hy_conv_b"], lw["ssd_conv_w"],
              lw["ssd_conv_b"], lw["sc_conv_w"], lw["sc_gain"], lw["dt_bias_t"], lw["a2_col"])
    return pl.pallas_call(
        functools.partial(_inproj_kernel, seg=seg, q=q),
        grid=(bsz, length // tm),
        in_specs=[
            tok(d),
            pl.BlockSpec((1, N_MOD, d), (lambda b, t: (b, 0, 0)) if per_batch else (lambda b, t: (0, 0, 0))),
            _const_spec(g.shape),
        ] + [_const_spec(a.shape) for a in consts],
        out_specs=[tok(HY_IN), tok(SSD_W), tok(SSD_XBC), tok(SC_W), tok(2 * N_DT),
                   pl.BlockSpec((1, 2 * N_DT, tm), lambda b, t: (b, 0, t))],
        out_shape=[
            jax.ShapeDtypeStruct((bsz, length, HY_IN), F32),
            jax.ShapeDtypeStruct((bsz, length, SSD_W), BF16),
            jax.ShapeDtypeStruct((bsz, length, SSD_XBC), BF16),
            jax.ShapeDtypeStruct((bsz, length, SC_W), BF16),
            jax.ShapeDtypeStruct((bsz, length, 2 * N_DT), F32),
            jax.ShapeDtypeStruct((bsz, 2 * N_DT, length), F32),
        ],
        compiler_params=_cparams(("parallel", "parallel")),
        name="inproj",
    )(x, mod, g, *consts)


def _dft_mats(p):
    f = np.arange(p, dtype=np.int64)[:, None]
    t = np.arange(p, dtype=np.int64)[None, :]
    ang = (np.pi / (2 * p)) * (((2 * f + 1) * t) % (4 * p)).astype(np.float64)
    fwd = np.concatenate([np.cos(ang), -np.sin(ang)], axis=0)
    return jnp.asarray(fwd, F32), jnp.asarray(fwd.T / p, F32)


def _filter_kernel(fb_ref, dl_ref, w1t_ref, w1c_ref, w1s_ref, b1_ref, w2_ref, b2_ref, w3_ref, b3_ref, w4_ref,
                   fr_ref, k_ref, *, length, tr):
    base = pl.program_id(0) * tr - length
    n_row = jnp.abs(base + lax.broadcasted_iota(jnp.int32, (1, tr), 1)).astype(F32)
    ang = fb_ref[...] * (n_row * (2.0 * math.pi / length))
    fr = fr_ref[...]
    h = w1t_ref[...] * (n_row * (1.0 / (length - 1)))
    h = h + _dot_hp(w1c_ref[...], jnp.cos(ang)) - _dot_hp(w1s_ref[...], jnp.sin(ang))
    h = jnp.sin(fr * (h + b1_ref[...]))
    h = jnp.sin(fr * (_dot_hp(w2_ref[...], h) + b2_ref[...]))
    h = jnp.sin(fr * (_dot_hp(w3_ref[...], h) + b3_ref[...]))
    lag = base + lax.broadcasted_iota(jnp.int32, (tr, 1), 0)
    t_col = jnp.abs(lag).astype(F32) * (1.0 / (length - 1))
    decay = jnp.exp(-t_col * jnp.abs(dl_ref[...]))
    for o in range(HY_ORDER):
        lo = o * 2 * HY_W
        kf = _dot_hp_t(h, w4_ref[:, lo:lo + HY_W]) * decay
        kb = _dot_hp_t(h, w4_ref[:, lo + HY_W:lo + 2 * HY_W]) * decay
        k_ref[:, o * HY_W:(o + 1) * HY_W] = jnp.where(lag >= 0, kf, jnp.where(lag == -length, 0.0, kb))


def _hyena_filter(fw, length, tr):
    bands = (HY_EMB - 1) // 2
    fbands = jnp.linspace(1e-4, bands - 1, bands, dtype=F32)[:, None]
    deltas = jnp.linspace(math.log(HY_TARGET) / HY_SLOW, math.log(HY_TARGET) / HY_FAST, HY_W, dtype=F32)[None, :]
    fw1 = fw["fw1"]
    col = lambda v: v[:, None]
    consts = (fbands, deltas, fw1[0][:, None], fw1[1:1 + bands].T, fw1[1 + bands:].T, col(fw["fb1"]), fw["fw2"].T,
              col(fw["fb2"]), fw["fw3"].T, col(fw["fb3"]), fw["fw4"], col(fw["freq"]))
    return pl.pallas_call(
        functools.partial(_filter_kernel, length=length, tr=tr),
        grid=(2 * length // tr,),
        in_specs=[pl.BlockSpec(a.shape, lambda i: (0, 0)) for a in consts],
        out_specs=pl.BlockSpec((tr, HY_ORDER * HY_W), lambda i: (i, 0)),
        out_shape=jax.ShapeDtypeStruct((2 * length, HY_ORDER * HY_W), F32),
        compiler_params=_cparams(("parallel",)),
        name="hyena_filter",
    )(*consts)


def _kspec_kernel(f_ref, k0_ref, k1_ref, o_ref, prev_ref, *, p):
    fwd = f_ref[...]

    @pl.when(pl.program_id(0) == 0)
    def _():
        prev_ref[...] = _dot_hp(fwd, k0_ref[...])

    a = _dot_hp(fwd, k1_ref[...])
    a1 = prev_ref[...]
    prev_ref[...] = a
    a1_re = a1[:p] - k0_ref[0:1, :]
    a1_im = a1[p:]
    sign = 1.0 - 2.0 * (lax.broadcasted_iota(jnp.int32, (p, 1), 0) % 2).astype(F32)
    o_ref[0, :p, :] = a[:p] - sign * a1_im
    o_ref[0, p:, :] = a[p:] + sign * a1_re


def _hyena_spectra(kfull, fwd, p):
    nb = kfull.shape[0] // (2 * p)
    cols = kfull.shape[1]
    return pl.pallas_call(
        functools.partial(_kspec_kernel, p=p),
        grid=(2 * nb - 1,),
        in_specs=[
            pl.BlockSpec(fwd.shape, lambda i: (0, 0)),
            pl.BlockSpec((p, cols), lambda i: (i, 0)),
            pl.BlockSpec((p, cols), lambda i: (i + 1, 0)),
        ],
        out_specs=pl.BlockSpec((1, 2 * p, cols), lambda i: (i, 0, 0)),
        out_shape=jax.ShapeDtypeStruct((2 * nb - 1, 2 * p, cols), F32),
        scratch_shapes=[pltpu.VMEM((2 * p, cols), F32)],
        compiler_params=_cparams(("arbitrary",)),
        name="hyena_spectra",
    )(fwd, kfull, kfull)


HY_CT = 128
HY_NB = 2
HY_RT = 32


def _hyconv_kernel(zin_ref, gate_ref, k_ref, bias_ref, fwd_ref, inv_ref, o_ref, zf_ref, ya_ref, yb_ref, *, nb, p):
    fwd = fwd_ref[...]
    for j in range(nb):
        zb = jnp.concatenate([zin_ref[s, j * p:(j + 1) * p, :] for s in range(HY_NB)], axis=1).astype(BF16)
        zf_ref[j] = _dot(fwd, zb)

    def mac(i, y_ref):
        for rt in range(p // HY_RT):
            r0 = rt * HY_RT
            acc_re = [None] * HY_NB
            acc_im = [None] * HY_NB
            for j in range(nb):
                d = i - j + (nb - 1)
                kre = k_ref[d, r0:r0 + HY_RT, :]
                kim = k_ref[d, p + r0:p + r0 + HY_RT, :]
                zre = zf_ref[j, r0:r0 + HY_RT, :]
                zim = zf_ref[j, p + r0:p + r0 + HY_RT, :]
                for s in range(HY_NB):
                    zr = zre[:, s * HY_CT:(s + 1) * HY_CT]
                    zi = zim[:, s * HY_CT:(s + 1) * HY_CT]
                    re = kre * zr - kim * zi
                    im = kre * zi + kim * zr
                    acc_re[s] = re if j == 0 else acc_re[s] + re
                    acc_im[s] = im if j == 0 else acc_im[s] + im
            y_ref[r0:r0 + HY_RT, :] = jnp.concatenate(acc_re, axis=1).astype(BF16)
            y_ref[p + r0:p + r0 + HY_RT, :] = jnp.concatenate(acc_im, axis=1).astype(BF16)

    def finish(i, y_ref):
        y = _dot(inv_ref[...], y_ref[...])
        rows = pl.ds(pl.multiple_of(i * p, p), p)
        for s in range(HY_NB):
            zs = zin_ref[s, rows, :]
            o_ref[s, rows, :] = gate_ref[s, rows, :] * (y[:, s * HY_CT:(s + 1) * HY_CT] + bias_ref[...] * zs)

    mac(0, ya_ref)
    for i in range(1, nb):
        mac(i, yb_ref if i % 2 else ya_ref)
        finish(i - 1, ya_ref if i % 2 else yb_ref)
    finish(nb - 1, ya_ref if nb % 2 else yb_ref)


def _hyconv(zin, zin_col, gate, gate_col, kspec, k_col, bias, fwd, inv, p):
    bsz, length, _ = zin.shape
    nb = length // p
    nct = HY_W // HY_CT
    nd = kspec.shape[0]
    return pl.pallas_call(
        functools.partial(_hyconv_kernel, nb=nb, p=p),
        grid=(nct, bsz // HY_NB),
        in_specs=[
            pl.BlockSpec((HY_NB, length, HY_CT), lambda c, b: (b, 0, zin_col + c)),
            pl.BlockSpec((HY_NB, length, HY_CT), lambda c, b: (b, 0, gate_col + c)),
            pl.BlockSpec((nd, 2 * p, HY_CT), lambda c, b: (0, 0, k_col + c), pipeline_mode=pl.Buffered(1)),
            pl.BlockSpec((1, HY_CT), lambda c, b: (0, c)),
            _const_spec(fwd.shape),
            _const_spec(inv.shape),
        ],
        out_specs=pl.BlockSpec((HY_NB, length, HY_CT), lambda c, b: (b, 0, c)),
        out_shape=jax.ShapeDtypeStruct((bsz, length, HY_W), F32),
        scratch_shapes=[
            pltpu.VMEM((nb, 2 * p, HY_NB * HY_CT), F32),
            pltpu.VMEM((2 * p, HY_NB * HY_CT), BF16),
            pltpu.VMEM((2 * p, HY_NB * HY_CT), BF16),
        ],
        compiler_params=_cparams(("arbitrary", "arbitrary")),
        name="hyconv",
    )(zin, gate, kspec, bias, fwd, inv)


def _ssd_dir(xbc_ref, dta_ref, dtat_ref, h_ref, y_ref, *, q, rev, r0):
    rows = slice(r0, r0 + q)
    n, hp, hd = SSD_STATE, SSD_HEADS // SSD_GROUPS, SSD_HEAD_DIM
    c0 = SSD_HEADS if rev else 0
    end = 0 if rev else q - 1
    li = lax.broadcasted_iota(jnp.int32, (q, q), 0)
    si = lax.broadcasted_iota(jnp.int32, (q, q), 1)
    keep = (li <= si) if rev else (li >= si)
    low = lax.broadcasted_iota(jnp.int32, (q, LANE), 1) < hd
    pairs = hp * hd // LANE
    for g in range(SSD_GROUPS):
        bg = xbc_ref[0, rows, SSD_W + g * n:SSD_W + (g + 1) * n]
        cg = xbc_ref[0, rows, SSD_W + SSD_GROUPS * n + g * n:SSD_W + SSD_GROUPS * n + (g + 1) * n]
        scores = lax.dot_general(cg, bg, (((1,), (1,)), ((), ())), preferred_element_type=F32)
        hg = h_ref[g]
        y_off = _dot(cg, hg.astype(BF16))
        ys, xws, tots = [], [], []
        for t in range(pairs):
            cols = slice(g * hp * hd + t * LANE, g * hp * hd + (t + 1) * LANE)
            acs_b, dt_b = [], []
            for r in (2 * t, 2 * t + 1):
                c = c0 + g * hp + r
                acs_b.append(jnp.broadcast_to(dta_ref[0, rows, N_DT + c:N_DT + c + 1], (q, LANE)))
                dt_b.append(jnp.broadcast_to(dta_ref[0, rows, c:c + 1], (q, LANE)))
            acs2 = jnp.where(low, acs_b[0], acs_b[1])
            dt2 = jnp.where(low, dt_b[0], dt_b[1])
            xdt = xbc_ref[0, rows, cols].astype(F32) * dt2
            yd = None
            for i, r in enumerate((2 * t, 2 * t + 1)):
                c = c0 + g * hp + r
                seg = jnp.concatenate([acs_b[i]] * (q // LANE), axis=1) - dtat_ref[0, N_DT + c:N_DT + c + 1, rows]
                m = (jnp.exp2(jnp.where(keep, seg, -1e30)) * scores).astype(BF16)
                xh = jnp.where(low if i == 0 else jnp.logical_not(low), xdt, 0.0).astype(BF16)
                part = _dot(m, xh)
                yd = part if yd is None else yd + part
            ys.append(yd + jnp.exp2(acs2) * y_off[:, t * LANE:(t + 1) * LANE])
            tot = acs2[end:end + 1, :]
            xws.append((xdt * jnp.exp2(tot - acs2)).astype(BF16))
            tots.append(tot)
        y_ref[0, rows, g * hp * hd:(g + 1) * hp * hd] = jnp.concatenate(ys, axis=1).astype(y_ref.dtype)
        upd = lax.dot_general(bg, jnp.concatenate(xws, axis=1), (((0,), (0,)), ((), ())),
                              preferred_element_type=F32)
        h_ref[g] = hg * jnp.exp2(jnp.concatenate(tots, axis=1)) + upd


def _ssd_kernel(xf_ref, dtaf_ref, dtatf_ref, xb_ref, dtab_ref, dtatb_ref, h0f_ref, h0b_ref,
                yf_ref, yb_ref, hf_out, hb_out, hf_ref, hb_ref, *, q):
    k = pl.program_id(1)

    @pl.when(k == 0)
    def _():
        hf_ref[...] = h0f_ref[0]
        hb_ref[...] = h0b_ref[0]

    cps = xf_ref.shape[1] // q
    for j in range(cps):
        _ssd_dir(xf_ref, dtaf_ref, dtatf_ref, hf_ref, yf_ref, q=q, rev=False, r0=j * q)
        _ssd_dir(xb_ref, dtab_ref, dtatb_ref, hb_ref, yb_ref, q=q, rev=True, r0=(cps - 1 - j) * q)

    @pl.when(k == pl.num_programs(1) - 1)
    def _():
        hf_out[0] = hf_ref[...]
        hb_out[0] = hb_ref[...]


def _ssd(xbc, dta, dtat, h0f, h0b, q, cps):
    bsz, length, _ = xbc.shape
    qs = q * cps
    nc = length // qs
    hshape = (SSD_GROUPS, SSD_STATE, SSD_W // SSD_GROUPS)
    fw = lambda b, k: (b, k, 0)
    bw = lambda b, k: (b, nc - 1 - k, 0)
    st = pl.BlockSpec((1,) + hshape, lambda b, k: (b, 0, 0, 0))
    return pl.pallas_call(
        functools.partial(_ssd_kernel, q=q),
        grid=(bsz, nc),
        in_specs=[
            pl.BlockSpec((1, qs, SSD_XBC), fw),
            pl.BlockSpec((1, qs, 2 * N_DT), fw),
            pl.BlockSpec((1, 2 * N_DT, qs), lambda b, k: (b, 0, k)),
            pl.BlockSpec((1, qs, SSD_XBC), bw),
            pl.BlockSpec((1, qs, 2 * N_DT), bw),
            pl.BlockSpec((1, 2 * N_DT, qs), lambda b, k: (b, 0, nc - 1 - k)),
            st, st,
        ],
        out_specs=[pl.BlockSpec((1, qs, SSD_W), fw), pl.BlockSpec((1, qs, SSD_W), bw), st, st],
        out_shape=[
            jax.ShapeDtypeStruct((bsz, length, SSD_W), BF16),
            jax.ShapeDtypeStruct((bsz, length, SSD_W), BF16),
            jax.ShapeDtypeStruct((bsz,) + hshape, F32),
            jax.ShapeDtypeStruct((bsz,) + hshape, F32),
        ],
        scratch_shapes=[pltpu.VMEM(hshape, F32), pltpu.VMEM(hshape, F32)],
        compiler_params=_cparams(("parallel", "arbitrary")),
        name="ssd",
    )(xbc, dta, dtat, xbc, dta, dtat, h0f, h0b)


def _outproj_kernel(x_ref, mod_ref, g_ref, hy_ref, yf_ref, yb_ref, xs_ref, z_ref, sc_ref, dexp_ref, mg_ref,
                    why_ref, wssd_ref, wsc_ref, wg_ref, wu_ref, wo_ref, o_ref, *, bounds):
    x = x_ref[0]
    mod = mod_ref[0]
    mg = mg_ref[...]
    y_hy = _rms(hy_ref[0], mg[:, :HY_W]).astype(BF16)
    y = yf_ref[0].astype(F32) + yb_ref[0].astype(F32) + dexp_ref[...] * xs_ref[0].astype(F32)
    y_ssd = _rms(y * _silu(z_ref[0].astype(F32)), mg[:, HY_W:HY_W + SSD_W]).astype(BF16)
    o = _dot(y_hy, why_ref[...]) + _dot(y_ssd, wssd_ref[...]) + _dot(sc_ref[0], wsc_ref[...])
    x = x + _rms(o, mod[5:6] * g_ref[3:4])
    o_ref[0] = _ffn_rows(x, mod, g_ref[...], wg_ref, wu_ref, wo_ref, 2, bounds)


def _outproj(x, mod, g, hy, yf, yb, xbc, z, sc, lw, ffn_w, tm):
    bsz, length, d = x.shape
    per_batch = mod.shape[0] != 1
    tok = lambda w: pl.BlockSpec((1, tm, w), lambda b, t: (b, t, 0))
    consts = (lw["d_exp"], lw["mix_gain"], lw["w_out_hy"], lw["w_out_ssd"], lw["w_out_sc"]) + tuple(ffn_w)
    return pl.pallas_call(
        functools.partial(_outproj_kernel, bounds=_ffn_bounds(ffn_w[0].shape[1], FFN_CHUNK)),
        grid=(bsz, length // tm),
        in_specs=[
            tok(d),
            pl.BlockSpec((1, N_MOD, d), (lambda b, t: (b, 0, 0)) if per_batch else (lambda b, t: (0, 0, 0))),
            _const_spec(g.shape),
            tok(HY_W), tok(SSD_W), tok(SSD_W), tok(SSD_W), tok(SSD_W), tok(SC_W),
        ] + [_const_spec(a.shape) for a in consts],
        out_specs=tok(d),
        out_shape=jax.ShapeDtypeStruct(x.shape, F32),
        compiler_params=_cparams(("parallel", "parallel")),
        name="outproj",
    )(x, mod, g, hy, yf, yb, xbc, z, sc, *consts)


def _mixer(xs, mod, g, lw, ffn_w, kspec, dft, seg, tm, p, q, h0f, h0b, batch_shape, want_out):
    bsz, length = batch_shape
    hy, z, xbc, sc, dta, dtat = _inproj(xs, mod, g, lw, seg, tm, q)
    if xs.shape[0] != bsz:
        dtat = dtat.reshape(2 * N_DT, bsz, length).transpose(1, 0, 2)
    seq = lambda a: a.reshape(bsz, length, a.shape[-1])
    yf, yb, hf, hb = _ssd(seq(xbc), seq(dta), dtat, h0f, h0b, q, _pick(length // q, (SSD_CPS, 2, 1)))
    if not want_out:
        return None, hf, hb
    fwd, inv = dft
    nct = HY_W // HY_CT
    hy_s = seq(hy)
    fwd_b, inv_b = fwd.astype(BF16), inv.astype(BF16)
    z1 = _hyconv(hy_s, 0, hy_s, nct, kspec, 0, lw["hy_bias"][0:1], fwd_b, inv_b, p)
    z2 = _hyconv(z1, 0, hy_s, 2 * nct, kspec, nct, lw["hy_bias"][1:2], fwd_b, inv_b, p)
    tok = lambda a: a.reshape(xs.shape[0], xs.shape[1], a.shape[-1])
    out = _outproj(xs, mod, g, tok(z2), tok(yf), tok(yb), xbc, z, sc, lw, ffn_w, tm)
    return out, hf, hb


def _layer_weights(l, w_in, w_out, hy_conv_w, hy_conv_b, hy_bias, ssd_conv_w, ssd_conv_b, ssd_a_log, ssd_dt_bias,
                   ssd_d, sc_conv_w, mix_gain):
    o_dt = HY_IN + SSD_W + SSD_XBC
    wl = w_in[l]
    return {
        "w_in": jnp.concatenate([wl[:, :o_dt], wl[:, o_dt + N_DT:]], axis=1).astype(BF16),
        "w_dtt": wl[:, o_dt:o_dt + N_DT].astype(BF16).T,
        "hy_conv_w": hy_conv_w[l],
        "hy_conv_b": hy_conv_b[l][None],
        "ssd_conv_w": ssd_conv_w[l],
        "ssd_conv_b": ssd_conv_b[l][None],
        "sc_conv_w": sc_conv_w[l],
        "sc_gain": mix_gain[l][None, HY_W + SSD_W:],
        "dt_bias_t": ssd_dt_bias[l].reshape(N_DT, 1),
        "hy_bias": hy_bias[l],
        "a2_col": (-jnp.exp(ssd_a_log[l]) * math.log2(math.e)).reshape(N_DT, 1),
        "d_exp": jnp.repeat(ssd_d[l], SSD_HEAD_DIM)[None],
        "mix_gain": mix_gain[l][None],
        "w_out_hy": w_out[l, :HY_W].astype(BF16),
        "w_out_ssd": w_out[l, HY_W:HY_W + SSD_W].astype(BF16),
        "w_out_sc": w_out[l, HY_W + SSD_W:].astype(BF16),
    }


def _pick(n, prefs):
    for v in prefs:
        if n % v == 0:
            return v
    return n


def kernel(x, c, ctx, c_ctx, w_mod, b_mod, norm_g, ffn_w_in, ffn_w_out, w_in, w_out, hy_conv_w, hy_conv_b, hy_fw1,
           hy_fb1, hy_fw2, hy_fb2, hy_fw3, hy_fb3, hy_fw4, hy_freq, hy_bias, ssd_conv_w, ssd_conv_b, ssd_a_log,
           ssd_dt_bias, ssd_d, sc_conv_w, mix_gain):
    bsz, n_lat, d = x.shape
    _, ctx_len, _ = ctx.shape
    depth = w_mod.shape[0]
    dff = ffn_w_out.shape[2]

    rows = -(-(bsz + 1) // 8) * 8
    cvec = jnp.zeros((rows, d), F32).at[:bsz].set(c).at[bsz].set(c_ctx)
    mod_all = _mod_table(cvec, w_mod, b_mod).reshape(depth, rows, N_MOD, d)

    tm = _pick(n_lat, (512, 256, 128, GRID_W))
    p_lat = _pick(n_lat, (512, 256, 128))
    q_lat = _pick(n_lat, (256, 128))
    xc = ctx.reshape(1, bsz * ctx_len, d)
    tm_c = ctx_len * _pick(bsz, (2, 1)) if ctx_len < 512 else _pick(ctx_len, (512, 256))
    p_ctx = _pick(ctx_len, (512, 256, 128))
    q_ctx = _pick(ctx_len, (256, 128))
    dft_lat = _dft_mats(p_lat)
    dft_ctx = _dft_mats(p_ctx)

    h_zero = jnp.zeros((bsz, SSD_GROUPS, SSD_STATE, SSD_W // SSD_GROUPS), F32)
    for l in range(depth):
        last = l == depth - 1
        mod_x = mod_all[l, :bsz]
        mod_c = mod_all[l, bsz:bsz + 1]
        g = norm_g[l]
        lw = _layer_weights(l, w_in, w_out, hy_conv_w, hy_conv_b, hy_bias, ssd_conv_w, ssd_conv_b, ssd_a_log,
                            ssd_dt_bias, ssd_d, sc_conv_w, mix_gain)
        fw = {"fw1": hy_fw1[l], "fb1": hy_fb1[l], "fw2": hy_fw2[l], "fb2": hy_fb2[l], "fw3": hy_fw3[l],
              "fb3": hy_fb3[l], "fw4": hy_fw4[l], "freq": hy_freq[l]}
        ffn_w = [(ffn_w_in[l, i, :, :dff].astype(BF16), ffn_w_in[l, i, :, dff:].astype(BF16),
                  ffn_w_out[l, i].astype(BF16)) for i in range(2)]

        x = _ffn(x, mod_x, g, *ffn_w[0], 0, tm)
        xc = _ffn(xc, mod_c, g, *ffn_w[0], 0, tm_c)

        ks_ctx = None
        if not last:
            ks_ctx = _hyena_spectra(_hyena_filter(fw, ctx_len, p_ctx), dft_ctx[0], p_ctx)
        xc, h_f, h_b = _mixer(xc, mod_c, g, lw, ffn_w[1], ks_ctx, dft_ctx, ctx_len, tm_c, p_ctx, q_ctx, h_zero,
                              h_zero, (bsz, ctx_len), not last)
        ks_lat = _hyena_spectra(_hyena_filter(fw, n_lat, p_lat), dft_lat[0], p_lat)
        x, _, _ = _mixer(x, mod_x, g, lw, ffn_w[1], ks_lat, dft_lat, GRID_W, tm, p_lat, q_lat, h_f, h_b,
                         (bsz, n_lat), True)
    return x
```

```python
import functools
import math

import jax
import jax.numpy as jnp
import numpy as np
from jax import lax
from jax.experimental import pallas as pl
from jax.experimental.pallas import tpu as pltpu

GRID_W = 64
EPS = 1e-6
FFN_RES_W = 0.5
N_MOD = 9
HY_W = 256
SSD_HEADS = 8
SSD_HEAD_DIM = 64
SSD_GROUPS = 2
SSD_STATE = 128
SC_W = 256
SHORT_K = 3
HY_ORDER = 2
HY_EMB = 33
HY_FAST = 0.3
HY_SLOW = 1.5
HY_TARGET = 1e-2

SSD_W = SSD_HEADS * SSD_HEAD_DIM
HY_IN = 3 * HY_W
SSD_XBC = SSD_W + 2 * SSD_GROUPS * SSD_STATE
SC_IN = 3 * SC_W
N_DT = 2 * SSD_HEADS

LANE = 128
SUBLANE = 8
MXU_TILE = 256
FFN_CHUNK = 3 * MXU_TILE
INPROJ_CHUNK = 2 * MXU_TILE
SSD_CPS = 4
VMEM_LIMIT = 56 * 1024 * 1024

F32 = jnp.float32
BF16 = jnp.bfloat16


def _cparams(sem):
    return pltpu.CompilerParams(dimension_semantics=sem, vmem_limit_bytes=VMEM_LIMIT)


def _const_spec(shape):
    nd = len(shape)
    return pl.BlockSpec(shape, lambda *_: (0,) * nd, pipeline_mode=pl.Buffered(1))


def _dot(a, b):
    return jnp.dot(a, b, preferred_element_type=F32)


def _split2(a):
    hi = a.astype(BF16)
    lo = (a - hi.astype(F32)).astype(BF16)
    return hi, lo


def _split3(a):
    hi = a.astype(BF16)
    r = a - hi.astype(F32)
    mid = r.astype(BF16)
    lo = (r - mid.astype(F32)).astype(BF16)
    return hi, mid, lo


def _dot_hp(a, b):
    ah, al = _split2(a)
    bh, bl = _split2(b)
    return _dot(ah, bh) + (_dot(ah, bl) + _dot(al, bh))


def _dot_hp_t(a_t, b):
    dn = (((0,), (0,)), ((), ()))
    ah, al = _split2(a_t)
    bh, bl = _split2(b)
    dg = lambda u, v: lax.dot_general(u, v, dn, preferred_element_type=F32)
    return dg(ah, bh) + (dg(ah, bl) + dg(al, bh))


def _silu(x):
    hx = 0.5 * x
    return hx + hx * jnp.tanh(hx)


def _rms(x, g):
    ms = jnp.mean(x * x, axis=-1, keepdims=True)
    return x * (lax.rsqrt(ms + EPS) * g)


def _mod_kernel(c_ref, w_ref, b_ref, o_ref):
    o_ref[0] = _dot_hp(_silu(c_ref[...]), w_ref[0]) + b_ref[0]


def _mod_table(cvec, w_mod, b_mod):
    depth, d, nm = w_mod.shape
    r = cvec.shape[0]
    tn = 1024 if nm % 1024 == 0 else d
    return pl.pallas_call(
        _mod_kernel,
        grid=(depth, nm // tn),
        in_specs=[
            pl.BlockSpec((r, d), lambda l, n: (0, 0)),
            pl.BlockSpec((1, d, tn), lambda l, n: (l, 0, n)),
            pl.BlockSpec((1, 1, tn), lambda l, n: (l, 0, n)),
        ],
        out_specs=pl.BlockSpec((1, r, tn), lambda l, n: (l, 0, n)),
        out_shape=jax.ShapeDtypeStruct((depth, r, nm), F32),
        compiler_params=_cparams(("parallel", "parallel")),
        name="mod_table",
    )(cvec, w_mod, b_mod.reshape(depth, 1, nm))


def _ffn_rows(x, mod, g, wg_ref, wu_ref, wo_ref, sub, bounds):
    shift, scale, gate = mod[3 * sub:3 * sub + 1], mod[3 * sub + 1:3 * sub + 2], mod[3 * sub + 2:3 * sub + 3]
    g_pre, g_post = g[2 * sub:2 * sub + 1], g[2 * sub + 1:2 * sub + 2]
    h = (_rms(x, g_pre * (1.0 + scale)) + shift).astype(BF16)
    acc = None
    for lo, hi in zip(bounds[:-1], bounds[1:]):
        gg = _dot(h, wg_ref[:, lo:hi])
        uu = _dot(h, wu_ref[:, lo:hi])
        a = (_silu(gg) * uu).astype(BF16)
        part = _dot(a, wo_ref[lo:hi, :])
        acc = part if acc is None else acc + part
    return x + _rms(acc, FFN_RES_W * gate * g_post)


def _ffn_kernel(x_ref, mod_ref, g_ref, wg_ref, wu_ref, wo_ref, o_ref, *, sub, bounds):
    o_ref[0] = _ffn_rows(x_ref[0], mod_ref[0], g_ref[...], wg_ref, wu_ref, wo_ref, sub, bounds)


def _ffn_bounds(dff, width):
    return tuple(range(0, dff, width)) + (dff,)


def _ffn_weight_specs(w_in_all, w_out_all, l, i):
    d, dff = w_out_all.shape[3], w_out_all.shape[2]
    one = pl.Buffered(1)
    return [
        pl.BlockSpec((None, None, d, dff), lambda *_: (l, i, 0, 0), pipeline_mode=one),
        pl.BlockSpec((None, None, d, dff), lambda *_: (l, i, 0, 1), pipeline_mode=one),
        pl.BlockSpec((None, None, dff, d), lambda *_: (l, i, 0, 0), pipeline_mode=one),
    ]


def _ffn(x, mod, g, ffn_w, sub, tm):
    w_in_all, w_out_all, l, i = ffn_w
    bsz, length, d = x.shape
    per_batch = mod.shape[0] != 1
    return pl.pallas_call(
        functools.partial(_ffn_kernel, sub=sub, bounds=_ffn_bounds(w_out_all.shape[2], FFN_CHUNK)),
        grid=(bsz, length // tm),
        in_specs=[
            pl.BlockSpec((1, tm, d), lambda b, t: (b, t, 0)),
            pl.BlockSpec((1, N_MOD, d), (lambda b, t: (b, 0, 0)) if per_batch else (lambda b, t: (0, 0, 0))),
            _const_spec(g.shape),
        ] + _ffn_weight_specs(*ffn_w),
        out_specs=pl.BlockSpec((1, tm, d), lambda b, t: (b, t, 0)),
        out_shape=jax.ShapeDtypeStruct(x.shape, F32),
        compiler_params=_cparams(("parallel", "parallel")),
        name="ffn",
    )(x, mod, g, w_in_all, w_in_all, w_out_all)


def _conv3(a, w, seg):
    n, c = a.shape
    a4 = a.reshape(n // seg, seg // SUBLANE, SUBLANE, c)
    sub = lax.broadcasted_iota(jnp.int32, a4.shape, 2)
    edge = jnp.zeros((n // seg, 1, SUBLANE, c), F32)
    down = pltpu.roll(a4, 1, 2)
    prev = jnp.where(sub == 0, jnp.concatenate([edge, down[:, :-1]], axis=1), down)
    up = pltpu.roll(a4, SUBLANE - 1, 2)
    nxt = jnp.where(sub == SUBLANE - 1, jnp.concatenate([up[:, 1:], edge], axis=1), up)
    return (prev.reshape(n, c) * w[0:1] + a * w[1:2]) + nxt.reshape(n, c) * w[2:3]


def _chunk_scan(da, q, axis):
    n = da.shape[axis]
    pos = lax.broadcasted_iota(jnp.int32, da.shape, axis) % q
    head = lax.broadcasted_iota(jnp.int32, da.shape, 1 - axis)
    up, down = da, da
    s = 1
    while s < q:
        up = up + jnp.where(pos >= s, pltpu.roll(up, s, axis), 0.0)
        down = down + jnp.where(pos < q - s, pltpu.roll(down, n - s, axis), 0.0)
        s *= 2
    return jnp.where(head < SSD_HEADS, up, down)


def _inproj_kernel(x_ref, mod_ref, g_ref, w_ref, wdtt_ref, hyw_ref, hyb_ref, sw_ref, sb_ref, scw_ref,
                   scg_ref, dtbt_ref, a2c_ref, hy_ref, z_ref, xbc_ref, sc_ref, dta_ref, dtat_ref, *, seg, q):
    x = x_ref[0]
    mod = mod_ref[0]
    h = (_rms(x, g_ref[2:3] * (1.0 + mod[4:5])) + mod[3:4]).astype(BF16)
    dtt = lax.dot_general(wdtt_ref[...], h, (((1,), (1,)), ((), ())), preferred_element_type=F32)
    dtt = jax.nn.softplus(dtt + dtbt_ref[...])
    rowform = jnp.concatenate([dtt, _chunk_scan(dtt * a2c_ref[...], q, 1)], axis=0)
    dtat_ref[0] = rowform
    pad = jnp.zeros((LANE - 2 * N_DT, rowform.shape[1]), F32)
    dta_ref[0] = jnp.concatenate([rowform, pad], axis=0).T[:, :2 * N_DT]
    o_z, o_xbc, o_sc = HY_IN, HY_IN + SSD_W, HY_IN + SSD_W + SSD_XBC

    def post_hy(p, lo, hi):
        hy_ref[0, :, lo:hi] = _conv3(p, hyw_ref[:, lo:hi], seg) + hyb_ref[:, lo:hi]

    def post_z(p, lo, hi):
        z_ref[0, :, lo - o_z:hi - o_z] = p.astype(BF16)

    def post_xbc(p, lo, hi):
        lo, hi = lo - o_xbc, hi - o_xbc
        xbc_ref[0, :, lo:hi] = _silu(_conv3(p, sw_ref[:, lo:hi], seg) + sb_ref[:, lo:hi]).astype(BF16)

    def post_sc(p, lo, hi):
        gb, gc, hx = p[:, :SC_W], p[:, SC_W:2 * SC_W], p[:, 2 * SC_W:]
        sc_ref[0] = _rms(gb * _conv3(gc * hx, scw_ref[...], seg), scg_ref[...]).astype(BF16)

    stages = []
    for post, lo, hi in ((post_hy, 0, o_z), (post_z, o_z, o_xbc), (post_xbc, o_xbc, o_sc)):
        stages += [(post, c, min(c + INPROJ_CHUNK, hi)) for c in range(lo, hi, INPROJ_CHUNK)]
    stages.append((post_sc, o_sc, o_sc + SC_IN))
    pending = None
    for post, lo, hi in stages:
        p = _dot(h, w_ref[:, lo:hi])
        if pending is not None:
            pending[0](*pending[1:])
        pending = (post, p, lo, hi)
    pending[0](*pending[1:])


def _inproj(x, mod, g, lw, seg, tm, q):
    bsz, length, d = x.shape
    per_batch = mod.shape[0] != 1
    tok = lambda w: pl.BlockSpec((1, tm, w), lambda b, t: (b, t, 0))
    consts = (lw["w_in"], lw["w_dtt"], lw["hy_conv_w"], lw["hy_conv_b"], lw["ssd_conv_w"],
              lw["ssd_conv_b"], lw["sc_conv_w"], lw["sc_gain"], lw["dt_bias_t"], lw["a2_col"])
    return pl.pallas_call(
        functools.partial(_inproj_kernel, seg=seg, q=q),
        grid=(bsz, length // tm),
        in_specs=[
            tok(d),
            pl.BlockSpec((1, N_MOD, d), (lambda b, t: (b, 0, 0)) if per_batch else (lambda b, t: (0, 0, 0))),
            _const_spec(g.shape),
        ] + [_const_spec(a.shape) for a in consts],
        out_specs=[tok(HY_IN), tok(SSD_W), tok(SSD_XBC), tok(SC_W), tok(2 * N_DT),
                   pl.BlockSpec((1, 2 * N_DT, tm), lambda b, t: (b, 0, t))],
        out_shape=[
            jax.ShapeDtypeStruct((bsz, length, HY_IN), F32),
            jax.ShapeDtypeStruct((bsz, length, SSD_W), BF16),
            jax.ShapeDtypeStruct((bsz, length, SSD_XBC), BF16),
            jax.ShapeDtypeStruct((bsz, length, SC_W), BF16),
            jax.ShapeDtypeStruct((bsz, length, 2 * N_DT), F32),
            jax.ShapeDtypeStruct((bsz, 2 * N_DT, length), F32),
        ],
        compiler_params=_cparams(("parallel", "parallel")),
        name="inproj",
    )(x, mod, g, *consts)


def _dft_mats(p):
    f = np.arange(p, dtype=np.int64)[:, None]
    t = np.arange(p, dtype=np.int64)[None, :]
    ang = (np.pi / (2 * p)) * (((2 * f + 1) * t) % (4 * p)).astype(np.float64)
    fwd = np.concatenate([np.cos(ang), -np.sin(ang)], axis=0)
    return jnp.asarray(fwd, F32), jnp.asarray(fwd.T / p, F32)


def _filter_kernel(fb_ref, dl_ref, w1t_ref, w1c_ref, w1s_ref, b1_ref, w2_ref, b2_ref, w3_ref, b3_ref, w4_ref,
                   fr_ref, k_ref, *, length, tr):
    base = pl.program_id(0) * tr - length
    n_row = jnp.abs(base + lax.broadcasted_iota(jnp.int32, (1, tr), 1)).astype(F32)
    ang = fb_ref[...] * (n_row * (2.0 * math.pi / length))
    fr = fr_ref[...]
    h = w1t_ref[...] * (n_row * (1.0 / (length - 1)))
    h = h + _dot_hp(w1c_ref[...], jnp.cos(ang)) - _dot_hp(w1s_ref[...], jnp.sin(ang))
    h = jnp.sin(fr * (h + b1_ref[...]))
    h = jnp.sin(fr * (_dot_hp(w2_ref[...], h) + b2_ref[...]))
    h = jnp.sin(fr * (_dot_hp(w3_ref[...], h) + b3_ref[...]))
    lag = base + lax.broadcasted_iota(jnp.int32, (tr, 1), 0)
    t_col = jnp.abs(lag).astype(F32) * (1.0 / (length - 1))
    decay = jnp.exp(-t_col * jnp.abs(dl_ref[...]))
    for o in range(HY_ORDER):
        lo = o * 2 * HY_W
        kf = _dot_hp_t(h, w4_ref[:, lo:lo + HY_W]) * decay
        kb = _dot_hp_t(h, w4_ref[:, lo + HY_W:lo + 2 * HY_W]) * decay
        k_ref[:, o * HY_W:(o + 1) * HY_W] = jnp.where(lag >= 0, kf, jnp.where(lag == -length, 0.0, kb))


def _hyena_filter(fw, length, tr):
    bands = (HY_EMB - 1) // 2
    fbands = jnp.linspace(1e-4, bands - 1, bands, dtype=F32)[:, None]
    deltas = jnp.linspace(math.log(HY_TARGET) / HY_SLOW, math.log(HY_TARGET) / HY_FAST, HY_W, dtype=F32)[None, :]
    fw1 = fw["fw1"]
    col = lambda v: v[:, None]
    consts = (fbands, deltas, fw1[0][:, None], fw1[1:1 + bands].T, fw1[1 + bands:].T, col(fw["fb1"]), fw["fw2"].T,
              col(fw["fb2"]), fw["fw3"].T, col(fw["fb3"]), fw["fw4"], col(fw["freq"]))
    return pl.pallas_call(
        functools.partial(_filter_kernel, length=length, tr=tr),
        grid=(2 * length // tr,),
        in_specs=[pl.BlockSpec(a.shape, lambda i: (0, 0)) for a in consts],
        out_specs=pl.BlockSpec((tr, HY_ORDER * HY_W), lambda i: (i, 0)),
        out_shape=jax.ShapeDtypeStruct((2 * length, HY_ORDER * HY_W), F32),
        compiler_params=_cparams(("parallel",)),
        name="hyena_filter",
    )(*consts)


def _kspec_kernel(f_ref, k0_ref, k1_ref, o_ref, prev_ref, *, p):
    fwd = f_ref[...]

    @pl.when(pl.program_id(0) == 0)
    def _():
        prev_ref[...] = _dot_hp(fwd, k0_ref[...])

    a = _dot_hp(fwd, k1_ref[...])
    a1 = prev_ref[...]
    prev_ref[...] = a
    a1_re = a1[:p] - k0_ref[0:1, :]
    a1_im = a1[p:]
    sign = 1.0 - 2.0 * (lax.broadcasted_iota(jnp.int32, (p, 1), 0) % 2).astype(F32)
    o_ref[0, :p, :] = a[:p] - sign * a1_im
    o_ref[0, p:, :] = a[p:] + sign * a1_re


def _hyena_spectra(kfull, fwd, p):
    nb = kfull.shape[0] // (2 * p)
    cols = kfull.shape[1]
    return pl.pallas_call(
        functools.partial(_kspec_kernel, p=p),
        grid=(2 * nb - 1,),
        in_specs=[
            pl.BlockSpec(fwd.shape, lambda i: (0, 0)),
            pl.BlockSpec((p, cols), lambda i: (i, 0)),
            pl.BlockSpec((p, cols), lambda i: (i + 1, 0)),
        ],
        out_specs=pl.BlockSpec((1, 2 * p, cols), lambda i: (i, 0, 0)),
        out_shape=jax.ShapeDtypeStruct((2 * nb - 1, 2 * p, cols), F32),
        scratch_shapes=[pltpu.VMEM((2 * p, cols), F32)],
        compiler_params=_cparams(("arbitrary",)),
        name="hyena_spectra",
    )(fwd, kfull, kfull)


HY_CT = 128
HY_NB = 2
HY_RT = 32


def _hyconv_kernel(zin_ref, gate_ref, k_ref, bias_ref, fwd_ref, inv_ref, o_ref, zf_ref, ya_ref, yb_ref, *, nb, p):
    fwd = fwd_ref[...]
    for j in range(nb):
        zb = jnp.concatenate([zin_ref[s, j * p:(j + 1) * p, :] for s in range(HY_NB)], axis=1).astype(BF16)
        zf_ref[j] = _dot(fwd, zb)

    def mac(i, y_ref):
        for rt in range(p // HY_RT):
            r0 = rt * HY_RT
            acc_re = [None] * HY_NB
            acc_im = [None] * HY_NB
            for j in range(nb):
                d = i - j + (nb - 1)
                kre = k_ref[d, r0:r0 + HY_RT, :]
                kim = k_ref[d, p + r0:p + r0 + HY_RT, :]
                zre = zf_ref[j, r0:r0 + HY_RT, :]
                zim = zf_ref[j, p + r0:p + r0 + HY_RT, :]
                for s in range(HY_NB):
                    zr = zre[:, s * HY_CT:(s + 1) * HY_CT]
                    zi = zim[:, s * HY_CT:(s + 1) * HY_CT]
                    re = kre * zr - kim * zi
                    im = kre * zi + kim * zr
                    acc_re[s] = re if j == 0 else acc_re[s] + re
                    acc_im[s] = im if j == 0 else acc_im[s] + im
            y_ref[r0:r0 + HY_RT, :] = jnp.concatenate(acc_re, axis=1).astype(BF16)
            y_ref[p + r0:p + r0 + HY_RT, :] = jnp.concatenate(acc_im, axis=1).astype(BF16)

    def finish(i, y_ref):
        y = _dot(inv_ref[...], y_ref[...])
        rows = pl.ds(pl.multiple_of(i * p, p), p)
        for s in range(HY_NB):
            zs = zin_ref[s, rows, :]
            o_ref[s, rows, :] = gate_ref[s, rows, :] * (y[:, s * HY_CT:(s + 1) * HY_CT] + bias_ref[...] * zs)

    mac(0, ya_ref)
    for i in range(1, nb):
        mac(i, yb_ref if i % 2 else ya_ref)
        finish(i - 1, ya_ref if i % 2 else yb_ref)
    finish(nb - 1, ya_ref if nb % 2 else yb_ref)


def _hyconv(zin, zin_col, gate, gate_col, kspec, k_col, bias, fwd, inv, p):
    bsz, length, _ = zin.shape
    nb = length // p
    nct = HY_W // HY_CT
    nd = kspec.shape[0]
    return pl.pallas_call(
        functools.partial(_hyconv_kernel, nb=nb, p=p),
        grid=(nct, bsz // HY_NB),
        in_specs=[
            pl.BlockSpec((HY_NB, length, HY_CT), lambda c, b: (b, 0, zin_col + c)),
            pl.BlockSpec((HY_NB, length, HY_CT), lambda c, b: (b, 0, gate_col + c)),
            pl.BlockSpec((nd, 2 * p, HY_CT), lambda c, b: (0, 0, k_col + c), pipeline_mode=pl.Buffered(1)),
            pl.BlockSpec((1, HY_CT), lambda c, b: (0, c)),
            _const_spec(fwd.shape),
            _const_spec(inv.shape),
        ],
        out_specs=pl.BlockSpec((HY_NB, length, HY_CT), lambda c, b: (b, 0, c)),
        out_shape=jax.ShapeDtypeStruct((bsz, length, HY_W), F32),
        scratch_shapes=[
            pltpu.VMEM((nb, 2 * p, HY_NB * HY_CT), F32),
            pltpu.VMEM((2 * p, HY_NB * HY_CT), BF16),
            pltpu.VMEM((2 * p, HY_NB * HY_CT), BF16),
        ],
        compiler_params=_cparams(("arbitrary", "arbitrary")),
        name="hyconv",
    )(zin, gate, kspec, bias, fwd, inv)


def _ssd_dir(xbc_ref, dta_ref, dtat_ref, spread_ref, h_ref, y_ref, *, q, rev, r0):
    rows = slice(r0, r0 + q)
    n, hp, hd = SSD_STATE, SSD_HEADS // SSD_GROUPS, SSD_HEAD_DIM
    c0 = SSD_HEADS if rev else 0
    end = 0 if rev else q - 1
    li = lax.broadcasted_iota(jnp.int32, (q, q), 0)
    si = lax.broadcasted_iota(jnp.int32, (q, q), 1)
    keep = (li <= si) if rev else (li >= si)
    low = lax.broadcasted_iota(jnp.int32, (q, LANE), 1) < hd
    pairs = hp * hd // LANE
    dt_hl = jnp.concatenate(_split2(dta_ref[0, rows, :N_DT]), axis=1)
    dt_lanes = _dot(dt_hl, spread_ref[:, c0 * hd:(c0 + SSD_HEADS) * hd])
    for g in range(SSD_GROUPS):
        bg = xbc_ref[0, rows, SSD_W + g * n:SSD_W + (g + 1) * n]
        cg = xbc_ref[0, rows, SSD_W + SSD_GROUPS * n + g * n:SSD_W + SSD_GROUPS * n + (g + 1) * n]
        scores = lax.dot_general(cg, bg, (((1,), (1,)), ((), ())), preferred_element_type=F32)
        hg = h_ref[g]
        y_off = _dot(cg, hg.astype(BF16))
        ys, xws, tots = [], [], []
        for t in range(pairs):
            cols = slice(g * hp * hd + t * LANE, g * hp * hd + (t + 1) * LANE)
            acs_b = []
            for r in (2 * t, 2 * t + 1):
                c = c0 + g * hp + r
                acs_b.append(jnp.broadcast_to(dta_ref[0, rows, N_DT + c:N_DT + c + 1], (q, LANE)))
            acs2 = jnp.where(low, acs_b[0], acs_b[1])
            xdt = xbc_ref[0, rows, cols].astype(F32) * dt_lanes[:, cols]
            yd = None
            for i, r in enumerate((2 * t, 2 * t + 1)):
                c = c0 + g * hp + r
                seg = jnp.concatenate([acs_b[i]] * (q // LANE), axis=1) - dtat_ref[0, N_DT + c:N_DT + c + 1, rows]
                m = (jnp.exp2(jnp.where(keep, seg, -1e30)) * scores).astype(BF16)
                xh = jnp.where(low if i == 0 else jnp.logical_not(low), xdt, 0.0).astype(BF16)
                part = _dot(m, xh)
                yd = part if yd is None else yd + part
            ys.append(yd + jnp.exp2(acs2) * y_off[:, t * LANE:(t + 1) * LANE])
            tot = acs2[end:end + 1, :]
            xws.append((xdt * jnp.exp2(tot - acs2)).astype(BF16))
            tots.append(tot)
        y_ref[0, rows, g * hp * hd:(g + 1) * hp * hd] = jnp.concatenate(ys, axis=1).astype(y_ref.dtype)
        upd = lax.dot_general(bg, jnp.concatenate(xws, axis=1), (((0,), (0,)), ((), ())),
                              preferred_element_type=F32)
        h_ref[g] = hg * jnp.exp2(jnp.concatenate(tots, axis=1)) + upd


def _dt_spread():
    k = np.arange(2 * N_DT)[:, None] % N_DT
    lane_head = np.arange(2 * SSD_W)[None, :] // SSD_HEAD_DIM
    return jnp.asarray(k == lane_head, BF16)


def _ssd_kernel(xf_ref, dtaf_ref, dtatf_ref, xb_ref, dtab_ref, dtatb_ref, spread_ref, h0f_ref, h0b_ref,
                yf_ref, yb_ref, hf_out, hb_out, hf_ref, hb_ref, *, q):
    k = pl.program_id(1)

    @pl.when(k == 0)
    def _():
        hf_ref[...] = h0f_ref[0]
        hb_ref[...] = h0b_ref[0]

    cps = xf_ref.shape[1] // q
    for j in range(cps):
        _ssd_dir(xf_ref, dtaf_ref, dtatf_ref, spread_ref, hf_ref, yf_ref, q=q, rev=False, r0=j * q)
        _ssd_dir(xb_ref, dtab_ref, dtatb_ref, spread_ref, hb_ref, yb_ref, q=q, rev=True, r0=(cps - 1 - j) * q)

    @pl.when(k == pl.num_programs(1) - 1)
    def _():
        hf_out[0] = hf_ref[...]
        hb_out[0] = hb_ref[...]


def _ssd(xbc, dta, dtat, h0f, h0b, q, cps):
    bsz, length, _ = xbc.shape
    qs = q * cps
    nc = length // qs
    hshape = (SSD_GROUPS, SSD_STATE, SSD_W // SSD_GROUPS)
    fw = lambda b, k: (b, k, 0)
    bw = lambda b, k: (b, nc - 1 - k, 0)
    st = pl.BlockSpec((1,) + hshape, lambda b, k: (b, 0, 0, 0))
    return pl.pallas_call(
        functools.partial(_ssd_kernel, q=q),
        grid=(bsz, nc),
        in_specs=[
            pl.BlockSpec((1, qs, SSD_XBC), fw),
            pl.BlockSpec((1, qs, 2 * N_DT), fw),
            pl.BlockSpec((1, 2 * N_DT, qs), lambda b, k: (b, 0, k)),
            pl.BlockSpec((1, qs, SSD_XBC), bw),
            pl.BlockSpec((1, qs, 2 * N_DT), bw),
            pl.BlockSpec((1, 2 * N_DT, qs), lambda b, k: (b, 0, nc - 1 - k)),
            pl.BlockSpec((2 * N_DT, 2 * SSD_W), lambda b, k: (0, 0)),
            st, st,
        ],
        out_specs=[pl.BlockSpec((1, qs, SSD_W), fw), pl.BlockSpec((1, qs, SSD_W), bw), st, st],
        out_shape=[
            jax.ShapeDtypeStruct((bsz, length, SSD_W), BF16),
            jax.ShapeDtypeStruct((bsz, length, SSD_W), BF16),
            jax.ShapeDtypeStruct((bsz,) + hshape, F32),
            jax.ShapeDtypeStruct((bsz,) + hshape, F32),
        ],
        scratch_shapes=[pltpu.VMEM(hshape, F32), pltpu.VMEM(hshape, F32)],
        compiler_params=_cparams(("parallel", "arbitrary")),
        name="ssd",
    )(xbc, dta, dtat, xbc, dta, dtat, _dt_spread(), h0f, h0b)


def _outproj_kernel(x_ref, mod_ref, g_ref, hy_ref, yf_ref, yb_ref, xs_ref, z_ref, sc_ref, dexp_ref, mg_ref,
                    why_ref, wssd_ref, wsc_ref, wg_ref, wu_ref, wo_ref, o_ref, *, bounds):
    x = x_ref[0]
    mod = mod_ref[0]
    mg = mg_ref[...]
    y_hy = _rms(hy_ref[0], mg[:, :HY_W]).astype(BF16)
    y = yf_ref[0].astype(F32) + yb_ref[0].astype(F32) + dexp_ref[...] * xs_ref[0].astype(F32)
    y_ssd = _rms(y * _silu(z_ref[0].astype(F32)), mg[:, HY_W:HY_W + SSD_W]).astype(BF16)
    o = _dot(y_hy, why_ref[...]) + _dot(y_ssd, wssd_ref[...]) + _dot(sc_ref[0], wsc_ref[...])
    x = x + _rms(o, mod[5:6] * g_ref[3:4])
    o_ref[0] = _ffn_rows(x, mod, g_ref[...], wg_ref, wu_ref, wo_ref, 2, bounds)


def _outproj(x, mod, g, hy, yf, yb, xbc, z, sc, lw, ffn_w, tm):
    bsz, length, d = x.shape
    per_batch = mod.shape[0] != 1
    tok = lambda w: pl.BlockSpec((1, tm, w), lambda b, t: (b, t, 0))
    consts = (lw["d_exp"], lw["mix_gain"], lw["w_out_hy"], lw["w_out_ssd"], lw["w_out_sc"])
    w_in_all, w_out_all = ffn_w[:2]
    return pl.pallas_call(
        functools.partial(_outproj_kernel, bounds=_ffn_bounds(w_out_all.shape[2], FFN_CHUNK)),
        grid=(bsz, length // tm),
        in_specs=[
            tok(d),
            pl.BlockSpec((1, N_MOD, d), (lambda b, t: (b, 0, 0)) if per_batch else (lambda b, t: (0, 0, 0))),
            _const_spec(g.shape),
            tok(HY_W), tok(SSD_W), tok(SSD_W), tok(SSD_W), tok(SSD_W), tok(SC_W),
        ] + [_const_spec(a.shape) for a in consts] + _ffn_weight_specs(*ffn_w),
        out_specs=tok(d),
        out_shape=jax.ShapeDtypeStruct(x.shape, F32),
        compiler_params=_cparams(("parallel", "parallel")),
        name="outproj",
    )(x, mod, g, hy, yf, yb, xbc, z, sc, *consts, w_in_all, w_in_all, w_out_all)


def _mixer(xs, mod, g, lw, ffn_w, kspec, dft, seg, tm, p, q, h0f, h0b, batch_shape, want_out):
    bsz, length = batch_shape
    hy, z, xbc, sc, dta, dtat = _inproj(xs, mod, g, lw, seg, tm, q)
    if xs.shape[0] != bsz:
        dtat = dtat.reshape(2 * N_DT, bsz, length).transpose(1, 0, 2)
    seq = lambda a: a.reshape(bsz, length, a.shape[-1])
    yf, yb, hf, hb = _ssd(seq(xbc), seq(dta), dtat, h0f, h0b, q, _pick(length // q, (SSD_CPS, 2, 1)))
    if not want_out:
        return None, hf, hb
    fwd, inv = dft
    nct = HY_W // HY_CT
    hy_s = seq(hy)
    fwd_b, inv_b = fwd.astype(BF16), inv.astype(BF16)
    z1 = _hyconv(hy_s, 0, hy_s, nct, kspec, 0, lw["hy_bias"][0:1], fwd_b, inv_b, p)
    z2 = _hyconv(z1, 0, hy_s, 2 * nct, kspec, nct, lw["hy_bias"][1:2], fwd_b, inv_b, p)
    tok = lambda a: a.reshape(xs.shape[0], xs.shape[1], a.shape[-1])
    out = _outproj(xs, mod, g, tok(z2), tok(yf), tok(yb), xbc, z, sc, lw, ffn_w, tm)
    return out, hf, hb


def _layer_weights(l, w_in, w_out, hy_conv_w, hy_conv_b, hy_bias, ssd_conv_w, ssd_conv_b, ssd_a_log, ssd_dt_bias,
                   ssd_d, sc_conv_w, mix_gain):
    o_dt = HY_IN + SSD_W + SSD_XBC
    wl = w_in[l]
    return {
        "w_in": jnp.concatenate([wl[:, :o_dt], wl[:, o_dt + N_DT:]], axis=1).astype(BF16),
        "w_dtt": wl[:, o_dt:o_dt + N_DT].astype(BF16).T,
        "hy_conv_w": hy_conv_w[l],
        "hy_conv_b": hy_conv_b[l][None],
        "ssd_conv_w": ssd_conv_w[l],
        "ssd_conv_b": ssd_conv_b[l][None],
        "sc_conv_w": sc_conv_w[l],
        "sc_gain": mix_gain[l][None, HY_W + SSD_W:],
        "dt_bias_t": ssd_dt_bias[l].reshape(N_DT, 1),
        "hy_bias": hy_bias[l],
        "a2_col": (-jnp.exp(ssd_a_log[l]) * math.log2(math.e)).reshape(N_DT, 1),
        "d_exp": jnp.repeat(ssd_d[l], SSD_HEAD_DIM)[None],
        "mix_gain": mix_gain[l][None],
        "w_out_hy": w_out[l, :HY_W].astype(BF16),
        "w_out_ssd": w_out[l, HY_W:HY_W + SSD_W].astype(BF16),
        "w_out_sc": w_out[l, HY_W + SSD_W:].astype(BF16),
    }


def _pick(n, prefs):
    for v in prefs:
        if n % v == 0:
            return v
    return n


def kernel(x, c, ctx, c_ctx, w_mod, b_mod, norm_g, ffn_w_in, ffn_w_out, w_in, w_out, hy_conv_w, hy_conv_b, hy_fw1,
           hy_fb1, hy_fw2, hy_fb2, hy_fw3, hy_fb3, hy_fw4, hy_freq, hy_bias, ssd_conv_w, ssd_conv_b, ssd_a_log,
           ssd_dt_bias, ssd_d, sc_conv_w, mix_gain):
    bsz, n_lat, d = x.shape
    _, ctx_len, _ = ctx.shape
    depth = w_mod.shape[0]
    ffn_in_b, ffn_out_b = ffn_w_in.astype(BF16), ffn_w_out.astype(BF16)

    rows = -(-(bsz + 1) // 8) * 8
    cvec = jnp.zeros((rows, d), F32).at[:bsz].set(c).at[bsz].set(c_ctx)
    mod_all = _mod_table(cvec, w_mod, b_mod).reshape(depth, rows, N_MOD, d)

    tm = _pick(n_lat, (512, 256, 128, GRID_W))
    p_lat = _pick(n_lat, (512, 256, 128))
    q_lat = _pick(n_lat, (256, 128))
    xc = ctx.reshape(1, bsz * ctx_len, d)
    tm_c = ctx_len * _pick(bsz, (2, 1)) if ctx_len < 512 else _pick(ctx_len, (512, 256))
    p_ctx = _pick(ctx_len, (512, 256, 128))
    q_ctx = _pick(ctx_len, (256, 128))
    dft_lat = _dft_mats(p_lat)
    dft_ctx = _dft_mats(p_ctx)

    h_zero = jnp.zeros((bsz, SSD_GROUPS, SSD_STATE, SSD_W // SSD_GROUPS), F32)
    for l in range(depth):
        last = l == depth - 1
        mod_x = mod_all[l, :bsz]
        mod_c = mod_all[l, bsz:bsz + 1]
        g = norm_g[l]
        lw = _layer_weights(l, w_in, w_out, hy_conv_w, hy_conv_b, hy_bias, ssd_conv_w, ssd_conv_b, ssd_a_log,
                            ssd_dt_bias, ssd_d, sc_conv_w, mix_gain)
        fw = {"fw1": hy_fw1[l], "fb1": hy_fb1[l], "fw2": hy_fw2[l], "fb2": hy_fb2[l], "fw3": hy_fw3[l],
              "fb3": hy_fb3[l], "fw4": hy_fw4[l], "freq": hy_freq[l]}
        ffn_w = [(ffn_in_b, ffn_out_b, l, i) for i in range(2)]

        x = _ffn(x, mod_x, g, ffn_w[0], 0, tm)
        xc = _ffn(xc, mod_c, g, ffn_w[0], 0, tm_c)

        ks_ctx = None
        if not last:
            ks_ctx = _hyena_spectra(_hyena_filter(fw, ctx_len, p_ctx), dft_ctx[0], p_ctx)
        xc, h_f, h_b = _mixer(xc, mod_c, g, lw, ffn_w[1], ks_ctx, dft_ctx, ctx_len, tm_c, p_ctx, q_ctx, h_zero,
                              h_zero, (bsz, ctx_len), not last)
        ks_lat = _hyena_spectra(_hyena_filter(fw, n_lat, p_lat), dft_lat[0], p_lat)
        x, _, _ = _mixer(x, mod_x, g, lw, ffn_w[1], ks_lat, dft_lat, GRID_W, tm, p_lat, q_lat, h_f, h_b,
                         (bsz, n_lat), True)
    return x
```

```python
import functools
import math

import jax
import jax.numpy as jnp
import numpy as np
from jax import lax
from jax.experimental import pallas as pl
from jax.experimental.pallas import tpu as pltpu

GRID_W = 64
EPS = 1e-6
FFN_RES_W = 0.5
N_MOD = 9
HY_W = 256
SSD_HEADS = 8
SSD_HEAD_DIM = 64
SSD_GROUPS = 2
SSD_STATE = 128
SC_W = 256
SHORT_K = 3
HY_ORDER = 2
HY_EMB = 33
HY_FAST = 0.3
HY_SLOW = 1.5
HY_TARGET = 1e-2

SSD_W = SSD_HEADS * SSD_HEAD_DIM
HY_IN = 3 * HY_W
SSD_XBC = SSD_W + 2 * SSD_GROUPS * SSD_STATE
SC_IN = 3 * SC_W
N_DT = 2 * SSD_HEADS

LANE = 128
SUBLANE = 8
MXU_TILE = 256
FFN_CHUNK = 3 * MXU_TILE
INPROJ_CHUNK = 2 * MXU_TILE
SSD_CPS = 4
VMEM_LIMIT = 56 * 1024 * 1024

F32 = jnp.float32
BF16 = jnp.bfloat16


def _cparams(sem):
    return pltpu.CompilerParams(dimension_semantics=sem, vmem_limit_bytes=VMEM_LIMIT)


def _const_spec(shape):
    nd = len(shape)
    return pl.BlockSpec(shape, lambda *_: (0,) * nd, pipeline_mode=pl.Buffered(1))


def _dot(a, b):
    return jnp.dot(a, b, preferred_element_type=F32)


def _split2(a):
    hi = a.astype(BF16)
    lo = (a - hi.astype(F32)).astype(BF16)
    return hi, lo


def _split3(a):
    hi = a.astype(BF16)
    r = a - hi.astype(F32)
    mid = r.astype(BF16)
    lo = (r - mid.astype(F32)).astype(BF16)
    return hi, mid, lo


def _dot_hp(a, b):
    ah, al = _split2(a)
    bh, bl = _split2(b)
    return _dot(ah, bh) + (_dot(ah, bl) + _dot(al, bh))


def _dot_hp_t(a_t, b):
    dn = (((0,), (0,)), ((), ()))
    ah, al = _split2(a_t)
    bh, bl = _split2(b)
    dg = lambda u, v: lax.dot_general(u, v, dn, preferred_element_type=F32)
    return dg(ah, bh) + (dg(ah, bl) + dg(al, bh))


def _silu(x):
    hx = 0.5 * x
    return hx + hx * jnp.tanh(hx)


def _rms(x, g):
    ms = jnp.mean(x * x, axis=-1, keepdims=True)
    return x * (lax.rsqrt(ms + EPS) * g)


def _mod_kernel(c_ref, w_ref, b_ref, o_ref):
    o_ref[0] = _dot_hp(_silu(c_ref[...]), w_ref[0]) + b_ref[0]


def _mod_table(cvec, w_mod, b_mod):
    depth, d, nm = w_mod.shape
    r = cvec.shape[0]
    tn = 1024 if nm % 1024 == 0 else d
    return pl.pallas_call(
        _mod_kernel,
        grid=(depth, nm // tn),
        in_specs=[
            pl.BlockSpec((r, d), lambda l, n: (0, 0)),
            pl.BlockSpec((1, d, tn), lambda l, n: (l, 0, n)),
            pl.BlockSpec((1, 1, tn), lambda l, n: (l, 0, n)),
        ],
        out_specs=pl.BlockSpec((1, r, tn), lambda l, n: (l, 0, n)),
        out_shape=jax.ShapeDtypeStruct((depth, r, nm), F32),
        compiler_params=_cparams(("parallel", "parallel")),
        name="mod_table",
    )(cvec, w_mod, b_mod.reshape(depth, 1, nm))


def _ffn_rows(x, mod, g, wg_ref, wu_ref, wo_ref, sub, bounds):
    shift, scale, gate = mod[3 * sub:3 * sub + 1], mod[3 * sub + 1:3 * sub + 2], mod[3 * sub + 2:3 * sub + 3]
    g_pre, g_post = g[2 * sub:2 * sub + 1], g[2 * sub + 1:2 * sub + 2]
    h = (_rms(x, g_pre * (1.0 + scale)) + shift).astype(BF16)
    acc = None
    for lo, hi in zip(bounds[:-1], bounds[1:]):
        gg = _dot(h, wg_ref[:, lo:hi])
        uu = _dot(h, wu_ref[:, lo:hi])
        a = (_silu(gg) * uu).astype(BF16)
        part = _dot(a, wo_ref[lo:hi, :])
        acc = part if acc is None else acc + part
    return x + _rms(acc, FFN_RES_W * gate * g_post)


def _ffn_kernel(x_ref, mod_ref, g_ref, wg_ref, wu_ref, wo_ref, o_ref, *, sub, bounds):
    o_ref[0] = _ffn_rows(x_ref[0], mod_ref[0], g_ref[...], wg_ref, wu_ref, wo_ref, sub, bounds)


def _ffn_bounds(dff, width):
    return tuple(range(0, dff, width)) + (dff,)


def _ffn_weight_specs(w_in_all, w_out_all, l, i):
    d, dff = w_out_all.shape[3], w_out_all.shape[2]
    one = pl.Buffered(1)
    return [
        pl.BlockSpec((None, None, d, dff), lambda *_: (l, i, 0, 0), pipeline_mode=one),
        pl.BlockSpec((None, None, d, dff), lambda *_: (l, i, 0, 1), pipeline_mode=one),
        pl.BlockSpec((None, None, dff, d), lambda *_: (l, i, 0, 0), pipeline_mode=one),
    ]


def _ffn(x, mod, g, ffn_w, sub, tm):
    w_in_all, w_out_all, l, i = ffn_w
    bsz, length, d = x.shape
    per_batch = mod.shape[0] != 1
    return pl.pallas_call(
        functools.partial(_ffn_kernel, sub=sub, bounds=_ffn_bounds(w_out_all.shape[2], FFN_CHUNK)),
        grid=(bsz, length // tm),
        in_specs=[
            pl.BlockSpec((1, tm, d), lambda b, t: (b, t, 0)),
            pl.BlockSpec((1, N_MOD, d), (lambda b, t: (b, 0, 0)) if per_batch else (lambda b, t: (0, 0, 0))),
            _const_spec(g.shape),
        ] + _ffn_weight_specs(*ffn_w),
        out_specs=pl.BlockSpec((1, tm, d), lambda b, t: (b, t, 0)),
        out_shape=jax.ShapeDtypeStruct(x.shape, F32),
        compiler_params=_cparams(("parallel", "parallel")),
        name="ffn",
    )(x, mod, g, w_in_all, w_in_all, w_out_all)


def _conv3(a, w, seg, scr):
    n, c = a.shape
    scr[SUBLANE:SUBLANE + n, :c] = a
    row = lax.broadcasted_iota(jnp.int32, (n, c), 0) % seg
    prev = jnp.where(row == 0, 0.0, scr[SUBLANE - 1:SUBLANE - 1 + n, :c])
    nxt = jnp.where(row == seg - 1, 0.0, scr[SUBLANE + 1:SUBLANE + 1 + n, :c])
    return (prev * w[0:1] + a * w[1:2]) + nxt * w[2:3]


def _chunk_scan(da, q, axis):
    n = da.shape[axis]
    pos = lax.broadcasted_iota(jnp.int32, da.shape, axis) % q
    head = lax.broadcasted_iota(jnp.int32, da.shape, 1 - axis)
    up, down = da, da
    s = 1
    while s < q:
        up = up + jnp.where(pos >= s, pltpu.roll(up, s, axis), 0.0)
        down = down + jnp.where(pos < q - s, pltpu.roll(down, n - s, axis), 0.0)
        s *= 2
    return jnp.where(head < SSD_HEADS, up, down)


def _inproj_kernel(x_ref, mod_ref, g_ref, w_ref, wdtt_ref, hyw_ref, hyb_ref, sw_ref, sb_ref, scw_ref,
                   scg_ref, dtbt_ref, a2c_ref, hy_ref, z_ref, xbc_ref, sc_ref, dta_ref, dtat_ref, scr_ref, *, seg, q):
    x = x_ref[0]
    mod = mod_ref[0]
    pad = jnp.zeros((SUBLANE, scr_ref.shape[1]), F32)
    scr_ref[:SUBLANE, :] = pad
    scr_ref[scr_ref.shape[0] - SUBLANE:, :] = pad
    h = (_rms(x, g_ref[2:3] * (1.0 + mod[4:5])) + mod[3:4]).astype(BF16)
    dtt = lax.dot_general(wdtt_ref[...], h, (((1,), (1,)), ((), ())), preferred_element_type=F32)
    dtt = jax.nn.softplus(dtt + dtbt_ref[...])
    rowform = jnp.concatenate([dtt, _chunk_scan(dtt * a2c_ref[...], q, 1)], axis=0)
    dtat_ref[0] = rowform
    pad = jnp.zeros((LANE - 2 * N_DT, rowform.shape[1]), F32)
    dta_ref[0] = jnp.concatenate([rowform, pad], axis=0).T[:, :2 * N_DT]
    o_z, o_xbc, o_sc = HY_IN, HY_IN + SSD_W, HY_IN + SSD_W + SSD_XBC

    def post_hy(p, lo, hi):
        hy_ref[0, :, lo:hi] = _conv3(p, hyw_ref[:, lo:hi], seg, scr_ref) + hyb_ref[:, lo:hi]

    def post_z(p, lo, hi):
        z_ref[0, :, lo - o_z:hi - o_z] = p.astype(BF16)

    def post_xbc(p, lo, hi):
        lo, hi = lo - o_xbc, hi - o_xbc
        xbc_ref[0, :, lo:hi] = _silu(_conv3(p, sw_ref[:, lo:hi], seg, scr_ref) + sb_ref[:, lo:hi]).astype(BF16)

    def post_sc(p, lo, hi):
        gb, gc, hx = p[:, :SC_W], p[:, SC_W:2 * SC_W], p[:, 2 * SC_W:]
        sc_ref[0] = _rms(gb * _conv3(gc * hx, scw_ref[...], seg, scr_ref), scg_ref[...]).astype(BF16)

    stages = []
    for post, lo, hi in ((post_hy, 0, o_z), (post_z, o_z, o_xbc), (post_xbc, o_xbc, o_sc)):
        stages += [(post, c, min(c + INPROJ_CHUNK, hi)) for c in range(lo, hi, INPROJ_CHUNK)]
    stages.append((post_sc, o_sc, o_sc + SC_IN))
    pending = None
    for post, lo, hi in stages:
        p = _dot(h, w_ref[:, lo:hi])
        if pending is not None:
            pending[0](*pending[1:])
        pending = (post, p, lo, hi)
    pending[0](*pending[1:])


def _inproj(x, mod, g, lw, seg, tm, q):
    bsz, length, d = x.shape
    per_batch = mod.shape[0] != 1
    tok = lambda w: pl.BlockSpec((1, tm, w), lambda b, t: (b, t, 0))
    consts = (lw["w_in"], lw["w_dtt"], lw["hy_conv_w"], lw["hy_conv_b"], lw["ssd_conv_w"],
              lw["ssd_conv_b"], lw["sc_conv_w"], lw["sc_gain"], lw["dt_bias_t"], lw["a2_col"])
    return pl.pallas_call(
        functools.partial(_inproj_kernel, seg=seg, q=q),
        grid=(bsz, length // tm),
        in_specs=[
            tok(d),
            pl.BlockSpec((1, N_MOD, d), (lambda b, t: (b, 0, 0)) if per_batch else (lambda b, t: (0, 0, 0))),
            _const_spec(g.shape),
        ] + [_const_spec(a.shape) for a in consts],
        out_specs=[tok(HY_IN), tok(SSD_W), tok(SSD_XBC), tok(SC_W), tok(2 * N_DT),
                   pl.BlockSpec((1, 2 * N_DT, tm), lambda b, t: (b, 0, t))],
        out_shape=[
            jax.ShapeDtypeStruct((bsz, length, HY_IN), F32),
            jax.ShapeDtypeStruct((bsz, length, SSD_W), BF16),
            jax.ShapeDtypeStruct((bsz, length, SSD_XBC), BF16),
            jax.ShapeDtypeStruct((bsz, length, SC_W), BF16),
            jax.ShapeDtypeStruct((bsz, length, 2 * N_DT), F32),
            jax.ShapeDtypeStruct((bsz, 2 * N_DT, length), F32),
        ],
        scratch_shapes=[pltpu.VMEM((tm + 2 * SUBLANE, INPROJ_CHUNK), F32)],
        compiler_params=_cparams(("parallel", "parallel")),
        name="inproj",
    )(x, mod, g, *consts)


def _dft_mats(p):
    f = np.arange(p, dtype=np.int64)[:, None]
    t = np.arange(p, dtype=np.int64)[None, :]
    ang = (np.pi / (2 * p)) * (((2 * f + 1) * t) % (4 * p)).astype(np.float64)
    fwd = np.concatenate([np.cos(ang), -np.sin(ang)], axis=0)
    return jnp.asarray(fwd, F32), jnp.asarray(fwd.T / p, F32)


def _filter_kernel(fb_ref, dl_ref, w1t_ref, w1c_ref, w1s_ref, b1_ref, w2_ref, b2_ref, w3_ref, b3_ref, w4_ref,
                   fr_ref, k_ref, *, length, tr):
    base = pl.program_id(0) * tr - length
    n_row = jnp.abs(base + lax.broadcasted_iota(jnp.int32, (1, tr), 1)).astype(F32)
    ang = fb_ref[...] * (n_row * (2.0 * math.pi / length))
    fr = fr_ref[...]
    h = w1t_ref[...] * (n_row * (1.0 / (length - 1)))
    h = h + _dot_hp(w1c_ref[...], jnp.cos(ang)) - _dot_hp(w1s_ref[...], jnp.sin(ang))
    h = jnp.sin(fr * (h + b1_ref[...]))
    h = jnp.sin(fr * (_dot_hp(w2_ref[...], h) + b2_ref[...]))
    h = jnp.sin(fr * (_dot_hp(w3_ref[...], h) + b3_ref[...]))
    lag = base + lax.broadcasted_iota(jnp.int32, (tr, 1), 0)
    t_col = jnp.abs(lag).astype(F32) * (1.0 / (length - 1))
    decay = jnp.exp(-t_col * jnp.abs(dl_ref[...]))
    for o in range(HY_ORDER):
        lo = o * 2 * HY_W
        kf = _dot_hp_t(h, w4_ref[:, lo:lo + HY_W]) * decay
        kb = _dot_hp_t(h, w4_ref[:, lo + HY_W:lo + 2 * HY_W]) * decay
        k_ref[:, o * HY_W:(o + 1) * HY_W] = jnp.where(lag >= 0, kf, jnp.where(lag == -length, 0.0, kb))


def _hyena_filter(fw, length, tr):
    bands = (HY_EMB - 1) // 2
    fbands = jnp.linspace(1e-4, bands - 1, bands, dtype=F32)[:, None]
    deltas = jnp.linspace(math.log(HY_TARGET) / HY_SLOW, math.log(HY_TARGET) / HY_FAST, HY_W, dtype=F32)[None, :]
    fw1 = fw["fw1"]
    col = lambda v: v[:, None]
    consts = (fbands, deltas, fw1[0][:, None], fw1[1:1 + bands].T, fw1[1 + bands:].T, col(fw["fb1"]), fw["fw2"].T,
              col(fw["fb2"]), fw["fw3"].T, col(fw["fb3"]), fw["fw4"], col(fw["freq"]))
    return pl.pallas_call(
        functools.partial(_filter_kernel, length=length, tr=tr),
        grid=(2 * length // tr,),
        in_specs=[pl.BlockSpec(a.shape, lambda i: (0, 0)) for a in consts],
        out_specs=pl.BlockSpec((tr, HY_ORDER * HY_W), lambda i: (i, 0)),
        out_shape=jax.ShapeDtypeStruct((2 * length, HY_ORDER * HY_W), F32),
        compiler_params=_cparams(("parallel",)),
        name="hyena_filter",
    )(*consts)


def _kspec_kernel(f_ref, k0_ref, k1_ref, o_ref, prev_ref, *, p):
    fwd = f_ref[...]

    @pl.when(pl.program_id(0) == 0)
    def _():
        prev_ref[...] = _dot_hp(fwd, k0_ref[...])

    a = _dot_hp(fwd, k1_ref[...])
    a1 = prev_ref[...]
    prev_ref[...] = a
    a1_re = a1[:p] - k0_ref[0:1, :]
    a1_im = a1[p:]
    sign = 1.0 - 2.0 * (lax.broadcasted_iota(jnp.int32, (p, 1), 0) % 2).astype(F32)
    o_ref[0, :p, :] = a[:p] - sign * a1_im
    o_ref[0, p:, :] = a[p:] + sign * a1_re


def _hyena_spectra(kfull, fwd, p):
    nb = kfull.shape[0] // (2 * p)
    cols = kfull.shape[1]
    return pl.pallas_call(
        functools.partial(_kspec_kernel, p=p),
        grid=(2 * nb - 1,),
        in_specs=[
            pl.BlockSpec(fwd.shape, lambda i: (0, 0)),
            pl.BlockSpec((p, cols), lambda i: (i, 0)),
            pl.BlockSpec((p, cols), lambda i: (i + 1, 0)),
        ],
        out_specs=pl.BlockSpec((1, 2 * p, cols), lambda i: (i, 0, 0)),
        out_shape=jax.ShapeDtypeStruct((2 * nb - 1, 2 * p, cols), F32),
        scratch_shapes=[pltpu.VMEM((2 * p, cols), F32)],
        compiler_params=_cparams(("arbitrary",)),
        name="hyena_spectra",
    )(fwd, kfull, kfull)


HY_CT = 128
HY_NB = 2
HY_RT = 32


def _hyconv_kernel(zin_ref, gate_ref, k_ref, bias_ref, fwd_ref, inv_ref, o_ref, zf_ref, ya_ref, yb_ref, *, nb, p):
    fwd = fwd_ref[...]
    for j in range(nb):
        zb = jnp.concatenate([zin_ref[s, j * p:(j + 1) * p, :] for s in range(HY_NB)], axis=1).astype(BF16)
        zf_ref[j] = _dot(fwd, zb)

    def mac(i, y_ref):
        for rt in range(p // HY_RT):
            r0 = rt * HY_RT
            acc_re = [None] * HY_NB
            acc_im = [None] * HY_NB
            for j in range(nb):
                d = i - j + (nb - 1)
                kre = k_ref[d, r0:r0 + HY_RT, :]
                kim = k_ref[d, p + r0:p + r0 + HY_RT, :]
                zre = zf_ref[j, r0:r0 + HY_RT, :]
                zim = zf_ref[j, p + r0:p + r0 + HY_RT, :]
                for s in range(HY_NB):
                    zr = zre[:, s * HY_CT:(s + 1) * HY_CT]
                    zi = zim[:, s * HY_CT:(s + 1) * HY_CT]
                    re = kre * zr - kim * zi
                    im = kre * zi + kim * zr
                    acc_re[s] = re if j == 0 else acc_re[s] + re
                    acc_im[s] = im if j == 0 else acc_im[s] + im
            y_ref[r0:r0 + HY_RT, :] = jnp.concatenate(acc_re, axis=1).astype(BF16)
            y_ref[p + r0:p + r0 + HY_RT, :] = jnp.concatenate(acc_im, axis=1).astype(BF16)

    def finish(i, y_ref):
        y = _dot(inv_ref[...], y_ref[...])
        rows = pl.ds(pl.multiple_of(i * p, p), p)
        for s in range(HY_NB):
            zs = zin_ref[s, rows, :]
            o_ref[s, rows, :] = gate_ref[s, rows, :] * (y[:, s * HY_CT:(s + 1) * HY_CT] + bias_ref[...] * zs)

    mac(0, ya_ref)
    for i in range(1, nb):
        mac(i, yb_ref if i % 2 else ya_ref)
        finish(i - 1, ya_ref if i % 2 else yb_ref)
    finish(nb - 1, ya_ref if nb % 2 else yb_ref)


def _hyconv(zin, zin_col, gate, gate_col, kspec, k_col, bias, fwd, inv, p):
    bsz, length, _ = zin.shape
    nb = length // p
    nct = HY_W // HY_CT
    nd = kspec.shape[0]
    return pl.pallas_call(
        functools.partial(_hyconv_kernel, nb=nb, p=p),
        grid=(nct, bsz // HY_NB),
        in_specs=[
            pl.BlockSpec((HY_NB, length, HY_CT), lambda c, b: (b, 0, zin_col + c)),
            pl.BlockSpec((HY_NB, length, HY_CT), lambda c, b: (b, 0, gate_col + c)),
            pl.BlockSpec((nd, 2 * p, HY_CT), lambda c, b: (0, 0, k_col + c), pipeline_mode=pl.Buffered(1)),
            pl.BlockSpec((1, HY_CT), lambda c, b: (0, c)),
            _const_spec(fwd.shape),
            _const_spec(inv.shape),
        ],
        out_specs=pl.BlockSpec((HY_NB, length, HY_CT), lambda c, b: (b, 0, c)),
        out_shape=jax.ShapeDtypeStruct((bsz, length, HY_W), F32),
        scratch_shapes=[
            pltpu.VMEM((nb, 2 * p, HY_NB * HY_CT), F32),
            pltpu.VMEM((2 * p, HY_NB * HY_CT), BF16),
            pltpu.VMEM((2 * p, HY_NB * HY_CT), BF16),
        ],
        compiler_params=_cparams(("arbitrary", "arbitrary")),
        name="hyconv",
    )(zin, gate, kspec, bias, fwd, inv)


def _ssd_dir(xbc_ref, dta_ref, dtat_ref, spread_ref, h_ref, y_ref, *, q, rev, r0):
    rows = slice(r0, r0 + q)
    n, hp, hd = SSD_STATE, SSD_HEADS // SSD_GROUPS, SSD_HEAD_DIM
    c0 = SSD_HEADS if rev else 0
    end = 0 if rev else q - 1
    li = lax.broadcasted_iota(jnp.int32, (q, q), 0)
    si = lax.broadcasted_iota(jnp.int32, (q, q), 1)
    keep = (li <= si) if rev else (li >= si)
    low = lax.broadcasted_iota(jnp.int32, (q, LANE), 1) < hd
    pairs = hp * hd // LANE
    dt_hl = jnp.concatenate(_split2(dta_ref[0, rows, :N_DT]), axis=1)
    dt_lanes = _dot(dt_hl, spread_ref[:, c0 * hd:(c0 + SSD_HEADS) * hd])
    for g in range(SSD_GROUPS):
        bg = xbc_ref[0, rows, SSD_W + g * n:SSD_W + (g + 1) * n]
        cg = xbc_ref[0, rows, SSD_W + SSD_GROUPS * n + g * n:SSD_W + SSD_GROUPS * n + (g + 1) * n]
        scores = lax.dot_general(cg, bg, (((1,), (1,)), ((), ())), preferred_element_type=F32)
        hg = h_ref[g]
        y_off = _dot(cg, hg.astype(BF16))
        ys, xws, tots = [], [], []
        for t in range(pairs):
            cols = slice(g * hp * hd + t * LANE, g * hp * hd + (t + 1) * LANE)
            acs_b = []
            for r in (2 * t, 2 * t + 1):
                c = c0 + g * hp + r
                acs_b.append(jnp.broadcast_to(dta_ref[0, rows, N_DT + c:N_DT + c + 1], (q, LANE)))
            acs2 = jnp.where(low, acs_b[0], acs_b[1])
            xdt = xbc_ref[0, rows, cols].astype(F32) * dt_lanes[:, cols]
            yd = None
            for i, r in enumerate((2 * t, 2 * t + 1)):
                c = c0 + g * hp + r
                seg = jnp.concatenate([acs_b[i]] * (q // LANE), axis=1) - dtat_ref[0, N_DT + c:N_DT + c + 1, rows]
                m = (jnp.exp2(jnp.where(keep, seg, -1e30)) * scores).astype(BF16)
                xh = jnp.where(low if i == 0 else jnp.logical_not(low), xdt, 0.0).astype(BF16)
                part = _dot(m, xh)
                yd = part if yd is None else yd + part
            ys.append(yd + jnp.exp2(acs2) * y_off[:, t * LANE:(t + 1) * LANE])
            tot = acs2[end:end + 1, :]
            xws.append((xdt * jnp.exp2(tot - acs2)).astype(BF16))
            tots.append(tot)
        y_ref[0, rows, g * hp * hd:(g + 1) * hp * hd] = jnp.concatenate(ys, axis=1).astype(y_ref.dtype)
        upd = lax.dot_general(bg, jnp.concatenate(xws, axis=1), (((0,), (0,)), ((), ())),
                              preferred_element_type=F32)
        h_ref[g] = hg * jnp.exp2(jnp.concatenate(tots, axis=1)) + upd


def _dt_spread():
    k = np.arange(2 * N_DT)[:, None] % N_DT
    lane_head = np.arange(2 * SSD_W)[None, :] // SSD_HEAD_DIM
    return jnp.asarray(k == lane_head, BF16)


def _ssd_kernel(xf_ref, dtaf_ref, dtatf_ref, xb_ref, dtab_ref, dtatb_ref, spread_ref, h0f_ref, h0b_ref,
                yf_ref, yb_ref, hf_out, hb_out, hf_ref, hb_ref, *, q):
    k = pl.program_id(1)

    @pl.when(k == 0)
    def _():
        hf_ref[...] = h0f_ref[0]
        hb_ref[...] = h0b_ref[0]

    cps = xf_ref.shape[1] // q
    for j in range(cps):
        _ssd_dir(xf_ref, dtaf_ref, dtatf_ref, spread_ref, hf_ref, yf_ref, q=q, rev=False, r0=j * q)
        _ssd_dir(xb_ref, dtab_ref, dtatb_ref, spread_ref, hb_ref, yb_ref, q=q, rev=True, r0=(cps - 1 - j) * q)

    @pl.when(k == pl.num_programs(1) - 1)
    def _():
        hf_out[0] = hf_ref[...]
        hb_out[0] = hb_ref[...]


def _ssd(xbc, dta, dtat, h0f, h0b, q, cps):
    bsz, length, _ = xbc.shape
    qs = q * cps
    nc = length // qs
    hshape = (SSD_GROUPS, SSD_STATE, SSD_W // SSD_GROUPS)
    fw = lambda b, k: (b, k, 0)
    bw = lambda b, k: (b, nc - 1 - k, 0)
    st = pl.BlockSpec((1,) + hshape, lambda b, k: (b, 0, 0, 0))
    return pl.pallas_call(
        functools.partial(_ssd_kernel, q=q),
        grid=(bsz, nc),
        in_specs=[
            pl.BlockSpec((1, qs, SSD_XBC), fw),
            pl.BlockSpec((1, qs, 2 * N_DT), fw),
            pl.BlockSpec((1, 2 * N_DT, qs), lambda b, k: (b, 0, k)),
            pl.BlockSpec((1, qs, SSD_XBC), bw),
            pl.BlockSpec((1, qs, 2 * N_DT), bw),
            pl.BlockSpec((1, 2 * N_DT, qs), lambda b, k: (b, 0, nc - 1 - k)),
            pl.BlockSpec((2 * N_DT, 2 * SSD_W), lambda b, k: (0, 0)),
            st, st,
        ],
        out_specs=[pl.BlockSpec((1, qs, SSD_W), fw), pl.BlockSpec((1, qs, SSD_W), bw), st, st],
        out_shape=[
            jax.ShapeDtypeStruct((bsz, length, SSD_W), BF16),
            jax.ShapeDtypeStruct((bsz, length, SSD_W), BF16),
            jax.ShapeDtypeStruct((bsz,) + hshape, F32),
            jax.ShapeDtypeStruct((bsz,) + hshape, F32),
        ],
        scratch_shapes=[pltpu.VMEM(hshape, F32), pltpu.VMEM(hshape, F32)],
        compiler_params=_cparams(("parallel", "arbitrary")),
        name="ssd",
    )(xbc, dta, dtat, xbc, dta, dtat, _dt_spread(), h0f, h0b)


def _outproj_kernel(x_ref, mod_ref, g_ref, hy_ref, yf_ref, yb_ref, xs_ref, z_ref, sc_ref, dexp_ref, mg_ref,
                    why_ref, wssd_ref, wsc_ref, wg_ref, wu_ref, wo_ref, o_ref, *, bounds):
    x = x_ref[0]
    mod = mod_ref[0]
    mg = mg_ref[...]
    y_hy = _rms(hy_ref[0], mg[:, :HY_W]).astype(BF16)
    y = yf_ref[0].astype(F32) + yb_ref[0].astype(F32) + dexp_ref[...] * xs_ref[0].astype(F32)
    y_ssd = _rms(y * _silu(z_ref[0].astype(F32)), mg[:, HY_W:HY_W + SSD_W]).astype(BF16)
    o = _dot(y_hy, why_ref[...]) + _dot(y_ssd, wssd_ref[...]) + _dot(sc_ref[0], wsc_ref[...])
    x = x + _rms(o, mod[5:6] * g_ref[3:4])
    o_ref[0] = _ffn_rows(x, mod, g_ref[...], wg_ref, wu_ref, wo_ref, 2, bounds)


def _outproj(x, mod, g, hy, yf, yb, xbc, z, sc, lw, ffn_w, tm):
    bsz, length, d = x.shape
    per_batch = mod.shape[0] != 1
    tok = lambda w: pl.BlockSpec((1, tm, w), lambda b, t: (b, t, 0))
    consts = (lw["d_exp"], lw["mix_gain"], lw["w_out_hy"], lw["w_out_ssd"], lw["w_out_sc"])
    w_in_all, w_out_all = ffn_w[:2]
    return pl.pallas_call(
        functools.partial(_outproj_kernel, bounds=_ffn_bounds(w_out_all.shape[2], FFN_CHUNK)),
        grid=(bsz, length // tm),
        in_specs=[
            tok(d),
            pl.BlockSpec((1, N_MOD, d), (lambda b, t: (b, 0, 0)) if per_batch else (lambda b, t: (0, 0, 0))),
            _const_spec(g.shape),
            tok(HY_W), tok(SSD_W), tok(SSD_W), tok(SSD_W), tok(SSD_W), tok(SC_W),
        ] + [_const_spec(a.shape) for a in consts] + _ffn_weight_specs(*ffn_w),
        out_specs=tok(d),
        out_shape=jax.ShapeDtypeStruct(x.shape, F32),
        compiler_params=_cparams(("parallel", "parallel")),
        name="outproj",
    )(x, mod, g, hy, yf, yb, xbc, z, sc, *consts, w_in_all, w_in_all, w_out_all)


def _mixer(xs, mod, g, lw, ffn_w, kspec, dft, seg, tm, p, q, h0f, h0b, batch_shape, want_out):
    bsz, length = batch_shape
    hy, z, xbc, sc, dta, dtat = _inproj(xs, mod, g, lw, seg, tm, q)
    if xs.shape[0] != bsz:
        dtat = dtat.reshape(2 * N_DT, bsz, length).transpose(1, 0, 2)
    seq = lambda a: a.reshape(bsz, length, a.shape[-1])
    yf, yb, hf, hb = _ssd(seq(xbc), seq(dta), dtat, h0f, h0b, q, _pick(length // q, (SSD_CPS, 2, 1)))
    if not want_out:
        return None, hf, hb
    fwd, inv = dft
    nct = HY_W // HY_CT
    hy_s = seq(hy)
    fwd_b, inv_b = fwd.astype(BF16), inv.astype(BF16)
    z1 = _hyconv(hy_s, 0, hy_s, nct, kspec, 0, lw["hy_bias"][0:1], fwd_b, inv_b, p)
    z2 = _hyconv(z1, 0, hy_s, 2 * nct, kspec, nct, lw["hy_bias"][1:2], fwd_b, inv_b, p)
    tok = lambda a: a.reshape(xs.shape[0], xs.shape[1], a.shape[-1])
    out = _outproj(xs, mod, g, tok(z2), tok(yf), tok(yb), xbc, z, sc, lw, ffn_w, tm)
    return out, hf, hb


def _layer_weights(l, w_in, w_out, hy_conv_w, hy_conv_b, hy_bias, ssd_conv_w, ssd_conv_b, ssd_a_log, ssd_dt_bias,
                   ssd_d, sc_conv_w, mix_gain):
    o_dt = HY_IN + SSD_W + SSD_XBC
    wl = w_in[l]
    return {
        "w_in": jnp.concatenate([wl[:, :o_dt], wl[:, o_dt + N_DT:]], axis=1).astype(BF16),
        "w_dtt": wl[:, o_dt:o_dt + N_DT].astype(BF16).T,
        "hy_conv_w": hy_conv_w[l],
        "hy_conv_b": hy_conv_b[l][None],
        "ssd_conv_w": ssd_conv_w[l],
        "ssd_conv_b": ssd_conv_b[l][None],
        "sc_conv_w": sc_conv_w[l],
        "sc_gain": mix_gain[l][None, HY_W + SSD_W:],
        "dt_bias_t": ssd_dt_bias[l].reshape(N_DT, 1),
        "hy_bias": hy_bias[l],
        "a2_col": (-jnp.exp(ssd_a_log[l]) * math.log2(math.e)).reshape(N_DT, 1),
        "d_exp": jnp.repeat(ssd_d[l], SSD_HEAD_DIM)[None],
        "mix_gain": mix_gain[l][None],
        "w_out_hy": w_out[l, :HY_W].astype(BF16),
        "w_out_ssd": w_out[l, HY_W:HY_W + SSD_W].astype(BF16),
        "w_out_sc": w_out[l, HY_W + SSD_W:].astype(BF16),
    }


def _pick(n, prefs):
    for v in prefs:
        if n % v == 0:
            return v
    return n


def kernel(x, c, ctx, c_ctx, w_mod, b_mod, norm_g, ffn_w_in, ffn_w_out, w_in, w_out, hy_conv_w, hy_conv_b, hy_fw1,
           hy_fb1, hy_fw2, hy_fb2, hy_fw3, hy_fb3, hy_fw4, hy_freq, hy_bias, ssd_conv_w, ssd_conv_b, ssd_a_log,
           ssd_dt_bias, ssd_d, sc_conv_w, mix_gain):
    bsz, n_lat, d = x.shape
    _, ctx_len, _ = ctx.shape
    depth = w_mod.shape[0]
    ffn_in_b, ffn_out_b = ffn_w_in.astype(BF16), ffn_w_out.astype(BF16)

    rows = -(-(bsz + 1) // 8) * 8
    cvec = jnp.zeros((rows, d), F32).at[:bsz].set(c).at[bsz].set(c_ctx)
    mod_all = _mod_table(cvec, w_mod, b_mod).reshape(depth, rows, N_MOD, d)

    tm = _pick(n_lat, (512, 256, 128, GRID_W))
    p_lat = _pick(n_lat, (512, 256, 128))
    q_lat = _pick(n_lat, (256, 128))
    xc = ctx.reshape(1, bsz * ctx_len, d)
    tm_c = ctx_len * _pick(bsz, (2, 1)) if ctx_len < 512 else _pick(ctx_len, (512, 256))
    p_ctx = _pick(ctx_len, (512, 256, 128))
    q_ctx = _pick(ctx_len, (256, 128))
    dft_lat = _dft_mats(p_lat)
    dft_ctx = _dft_mats(p_ctx)

    h_zero = jnp.zeros((bsz, SSD_GROUPS, SSD_STATE, SSD_W // SSD_GROUPS), F32)
    for l in range(depth):
        last = l == depth - 1
        mod_x = mod_all[l, :bsz]
        mod_c = mod_all[l, bsz:bsz + 1]
        g = norm_g[l]
        lw = _layer_weights(l, w_in, w_out, hy_conv_w, hy_conv_b, hy_bias, ssd_conv_w, ssd_conv_b, ssd_a_log,
                            ssd_dt_bias, ssd_d, sc_conv_w, mix_gain)
        fw = {"fw1": hy_fw1[l], "fb1": hy_fb1[l], "fw2": hy_fw2[l], "fb2": hy_fb2[l], "fw3": hy_fw3[l],
              "fb3": hy_fb3[l], "fw4": hy_fw4[l], "freq": hy_freq[l]}
        ffn_w = [(ffn_in_b, ffn_out_b, l, i) for i in range(2)]

        x = _ffn(x, mod_x, g, ffn_w[0], 0, tm)
        xc = _ffn(xc, mod_c, g, ffn_w[0], 0, tm_c)

        ks_ctx = None
        if not last:
            ks_ctx = _hyena_spectra(_hyena_filter(fw, ctx_len, p_ctx), dft_ctx[0], p_ctx)
        xc, h_f, h_b = _mixer(xc, mod_c, g, lw, ffn_w[1], ks_ctx, dft_ctx, ctx_len, tm_c, p_ctx, q_ctx, h_zero,
                              h_zero, (bsz, ctx_len), not last)
        ks_lat = _hyena_spectra(_hyena_filter(fw, n_lat, p_lat), dft_lat[0], p_lat)
        x, _, _ = _mixer(x, mod_x, g, lw, ffn_w[1], ks_lat, dft_lat, GRID_W, tm, p_lat, q_lat, h_f, h_b,
                         (bsz, n_lat), True)
    return x
```

```python
import functools
import math

import jax
import jax.numpy as jnp
import numpy as np
from jax import lax
from jax.experimental import pallas as pl
from jax.experimental.pallas import tpu as pltpu

GRID_W = 64
EPS = 1e-6
FFN_RES_W = 0.5
N_MOD = 9
HY_W = 256
SSD_HEADS = 8
SSD_HEAD_DIM = 64
SSD_GROUPS = 2
SSD_STATE = 128
SC_W = 256
HY_ORDER = 2
HY_EMB = 33
HY_FAST = 0.3
HY_SLOW = 1.5
HY_TARGET = 1e-2

SSD_W = SSD_HEADS * SSD_HEAD_DIM
HY_IN = 3 * HY_W
SSD_XBC = SSD_W + 2 * SSD_GROUPS * SSD_STATE
SC_IN = 3 * SC_W
N_DT = 2 * SSD_HEADS

LANE = 128
SUBLANE = 8
MXU_TILE = 256
FFN_CHUNK = 3 * MXU_TILE
INPROJ_CHUNK = 2 * MXU_TILE
SSD_CPS = 8
ROW_TILE = 512
HY_BLOCK = 512
SSD_CHUNK = 256
VMEM_LIMIT = 56 * 1024 * 1024

F32 = jnp.float32
BF16 = jnp.bfloat16


def _cparams(sem):
    return pltpu.CompilerParams(dimension_semantics=sem, vmem_limit_bytes=VMEM_LIMIT)


def _const_spec(shape):
    nd = len(shape)
    return pl.BlockSpec(shape, lambda *_: (0,) * nd, pipeline_mode=pl.Buffered(1))


def _dot(a, b):
    return jnp.dot(a, b, preferred_element_type=F32)


def _split2(a):
    hi = a.astype(BF16)
    lo = (a - hi.astype(F32)).astype(BF16)
    return hi, lo


def _dot_hp(a, b):
    ah, al = _split2(a)
    bh, bl = _split2(b)
    return _dot(ah, bh) + (_dot(ah, bl) + _dot(al, bh))


def _dot_hp_t(a_t, b):
    dn = (((0,), (0,)), ((), ()))
    ah, al = _split2(a_t)
    bh, bl = _split2(b)
    dg = lambda u, v: lax.dot_general(u, v, dn, preferred_element_type=F32)
    return dg(ah, bh) + (dg(ah, bl) + dg(al, bh))


def _silu(x):
    hx = 0.5 * x
    return hx + hx * jnp.tanh(hx)


def _rms(x, g):
    ms = jnp.mean(x * x, axis=-1, keepdims=True)
    return x * (lax.rsqrt(ms + EPS) * g)


def _mod_kernel(c_ref, w_ref, b_ref, o_ref):
    o_ref[0] = _dot_hp(_silu(c_ref[...]), w_ref[0]) + b_ref[0]


def _mod_table(cvec, w_mod, b_mod):
    depth, d, nm = w_mod.shape
    r = cvec.shape[0]
    tn = 1024 if nm % 1024 == 0 else d
    return pl.pallas_call(
        _mod_kernel,
        grid=(depth, nm // tn),
        in_specs=[
            pl.BlockSpec((r, d), lambda l, n: (0, 0)),
            pl.BlockSpec((1, d, tn), lambda l, n: (l, 0, n)),
            pl.BlockSpec((1, 1, tn), lambda l, n: (l, 0, n)),
        ],
        out_specs=pl.BlockSpec((1, r, tn), lambda l, n: (l, 0, n)),
        out_shape=jax.ShapeDtypeStruct((depth, r, nm), F32),
        compiler_params=_cparams(("parallel", "parallel")),
        name="mod_table",
    )(cvec, w_mod, b_mod.reshape(depth, 1, nm))


def _ffn_rows(x, mod, g, wg_ref, wu_ref, wo_ref, sub, bounds):
    shift, scale, gate = mod[3 * sub:3 * sub + 1], mod[3 * sub + 1:3 * sub + 2], mod[3 * sub + 2:3 * sub + 3]
    g_pre, g_post = g[2 * sub:2 * sub + 1], g[2 * sub + 1:2 * sub + 2]
    h = (_rms(x, g_pre * (1.0 + scale)) + shift).astype(BF16)
    acc = None
    for lo, hi in zip(bounds[:-1], bounds[1:]):
        gg = _dot(h, wg_ref[:, lo:hi])
        uu = _dot(h, wu_ref[:, lo:hi])
        a = (_silu(gg) * uu).astype(BF16)
        part = _dot(a, wo_ref[lo:hi, :])
        acc = part if acc is None else acc + part
    return x + _rms(acc, FFN_RES_W * gate * g_post)


def _ffn_kernel(x_ref, mod_ref, g_ref, wg_ref, wu_ref, wo_ref, o_ref, *, sub, bounds):
    o_ref[0] = _ffn_rows(x_ref[0], mod_ref[0], g_ref[...], wg_ref, wu_ref, wo_ref, sub, bounds)


def _ffn_bounds(dff, width):
    return tuple(range(0, dff, width)) + (dff,)


def _ffn_weight_specs(w_in_all, w_out_all, l, i):
    d, dff = w_out_all.shape[3], w_out_all.shape[2]
    one = pl.Buffered(1)
    return [
        pl.BlockSpec((None, None, d, dff), lambda *_: (l, i, 0, 0), pipeline_mode=one),
        pl.BlockSpec((None, None, d, dff), lambda *_: (l, i, 0, 1), pipeline_mode=one),
        pl.BlockSpec((None, None, dff, d), lambda *_: (l, i, 0, 0), pipeline_mode=one),
    ]


def _ffn(x, mod, g, ffn_w, sub, tm):
    w_in_all, w_out_all, l, i = ffn_w
    bsz, length, d = x.shape
    per_batch = mod.shape[0] != 1
    return pl.pallas_call(
        functools.partial(_ffn_kernel, sub=sub, bounds=_ffn_bounds(w_out_all.shape[2], FFN_CHUNK)),
        grid=(bsz, length // tm),
        in_specs=[
            pl.BlockSpec((1, tm, d), lambda b, t: (b, t, 0)),
            pl.BlockSpec((1, N_MOD, d), (lambda b, t: (b, 0, 0)) if per_batch else (lambda b, t: (0, 0, 0))),
            _const_spec(g.shape),
        ] + _ffn_weight_specs(*ffn_w),
        out_specs=pl.BlockSpec((1, tm, d), lambda b, t: (b, t, 0)),
        out_shape=jax.ShapeDtypeStruct(x.shape, F32),
        compiler_params=_cparams(("parallel", "parallel")),
        name="ffn",
    )(x, mod, g, w_in_all, w_in_all, w_out_all)


def _conv3(a, w, seg, scr):
    n, c = a.shape
    scr[SUBLANE:SUBLANE + n, :c] = a
    row = lax.broadcasted_iota(jnp.int32, (n, c), 0) % seg
    prev = jnp.where(row == 0, 0.0, scr[SUBLANE - 1:SUBLANE - 1 + n, :c])
    nxt = jnp.where(row == seg - 1, 0.0, scr[SUBLANE + 1:SUBLANE + 1 + n, :c])
    return (prev * w[0:1] + a * w[1:2]) + nxt * w[2:3]


def _chunk_scan(da, q, axis):
    n = da.shape[axis]
    pos = lax.broadcasted_iota(jnp.int32, da.shape, axis) % q
    head = lax.broadcasted_iota(jnp.int32, da.shape, 1 - axis)
    up, down = da, da
    s = 1
    while s < q:
        up = up + jnp.where(pos >= s, pltpu.roll(up, s, axis), 0.0)
        down = down + jnp.where(pos < q - s, pltpu.roll(down, n - s, axis), 0.0)
        s *= 2
    return jnp.where(head < SSD_HEADS, up, down)


def _inproj_kernel(x_ref, mod_ref, g_ref, w_ref, wdtt_ref, hyw_ref, hyb_ref, sw_ref, sb_ref, scw_ref,
                   scg_ref, dtbt_ref, a2c_ref, hy_ref, z_ref, xbc_ref, sc_ref, dta_ref, dtat_ref, scr_ref, *, seg, q):
    x = x_ref[0]
    mod = mod_ref[0]
    pad = jnp.zeros((SUBLANE, scr_ref.shape[1]), F32)
    scr_ref[:SUBLANE, :] = pad
    scr_ref[scr_ref.shape[0] - SUBLANE:, :] = pad
    h = (_rms(x, g_ref[2:3] * (1.0 + mod[4:5])) + mod[3:4]).astype(BF16)
    dtt = lax.dot_general(wdtt_ref[...], h, (((1,), (1,)), ((), ())), preferred_element_type=F32)
    dtt = jax.nn.softplus(dtt + dtbt_ref[...])
    rowform = jnp.concatenate([dtt, _chunk_scan(dtt * a2c_ref[...], q, 1)], axis=0)
    dtat_ref[0] = rowform
    pad = jnp.zeros((LANE - 2 * N_DT, rowform.shape[1]), F32)
    dta_ref[0] = jnp.concatenate([rowform, pad], axis=0).T[:, :2 * N_DT]
    o_z, o_xbc, o_sc = HY_IN, HY_IN + SSD_W, HY_IN + SSD_W + SSD_XBC

    def post_hy(p, lo, hi):
        hy_ref[0, :, lo:hi] = _conv3(p, hyw_ref[:, lo:hi], seg, scr_ref) + hyb_ref[:, lo:hi]

    def post_z(p, lo, hi):
        z_ref[0, :, lo - o_z:hi - o_z] = p.astype(BF16)

    def post_xbc(p, lo, hi):
        lo, hi = lo - o_xbc, hi - o_xbc
        xbc_ref[0, :, lo:hi] = _silu(_conv3(p, sw_ref[:, lo:hi], seg, scr_ref) + sb_ref[:, lo:hi]).astype(BF16)

    def post_sc(p, lo, hi):
        gb, gc, hx = p[:, :SC_W], p[:, SC_W:2 * SC_W], p[:, 2 * SC_W:]
        sc_ref[0] = _rms(gb * _conv3(gc * hx, scw_ref[...], seg, scr_ref), scg_ref[...]).astype(BF16)

    stages = []
    for post, lo, hi in ((post_hy, 0, o_z), (post_z, o_z, o_xbc), (post_xbc, o_xbc, o_sc)):
        stages += [(post, c, min(c + INPROJ_CHUNK, hi)) for c in range(lo, hi, INPROJ_CHUNK)]
    stages.append((post_sc, o_sc, o_sc + SC_IN))
    pending = None
    for post, lo, hi in stages:
        p = _dot(h, w_ref[:, lo:hi])
        if pending is not None:
            pending[0](*pending[1:])
        pending = (post, p, lo, hi)
    pending[0](*pending[1:])


def _inproj(x, mod, g, lw, seg, tm, q):
    bsz, length, d = x.shape
    per_batch = mod.shape[0] != 1
    tok = lambda w: pl.BlockSpec((1, tm, w), lambda b, t: (b, t, 0))
    consts = (lw["w_in"], lw["w_dtt"], lw["hy_conv_w"], lw["hy_conv_b"], lw["ssd_conv_w"],
              lw["ssd_conv_b"], lw["sc_conv_w"], lw["sc_gain"], lw["dt_bias_t"], lw["a2_col"])
    return pl.pallas_call(
        functools.partial(_inproj_kernel, seg=seg, q=q),
        grid=(bsz, length // tm),
        in_specs=[
            tok(d),
            pl.BlockSpec((1, N_MOD, d), (lambda b, t: (b, 0, 0)) if per_batch else (lambda b, t: (0, 0, 0))),
            _const_spec(g.shape),
        ] + [_const_spec(a.shape) for a in consts],
        out_specs=[tok(HY_IN), tok(SSD_W), tok(SSD_XBC), tok(SC_W), tok(2 * N_DT),
                   pl.BlockSpec((1, 2 * N_DT, tm), lambda b, t: (b, 0, t))],
        out_shape=[
            jax.ShapeDtypeStruct((bsz, length, HY_IN), F32),
            jax.ShapeDtypeStruct((bsz, length, SSD_W), BF16),
            jax.ShapeDtypeStruct((bsz, length, SSD_XBC), BF16),
            jax.ShapeDtypeStruct((bsz, length, SC_W), BF16),
            jax.ShapeDtypeStruct((bsz, length, 2 * N_DT), F32),
            jax.ShapeDtypeStruct((bsz, 2 * N_DT, length), F32),
        ],
        scratch_shapes=[pltpu.VMEM((tm + 2 * SUBLANE, INPROJ_CHUNK), F32)],
        compiler_params=_cparams(("parallel", "parallel")),
        name="inproj",
    )(x, mod, g, *consts)


def _dft_mats(p):
    f = np.arange(p, dtype=np.int64)[:, None]
    t = np.arange(p, dtype=np.int64)[None, :]
    ang = (np.pi / (2 * p)) * (((2 * f + 1) * t) % (4 * p)).astype(np.float64)
    fwd = np.concatenate([np.cos(ang), -np.sin(ang)], axis=0)
    return jnp.asarray(fwd, F32), jnp.asarray(fwd.T / p, F32)


def _filter_kernel(fb_ref, dl_ref, w1t_ref, w1c_ref, w1s_ref, b1_ref, w2_ref, b2_ref, w3_ref, b3_ref, w4_ref,
                   fr_ref, k_ref, *, length, tr):
    base = pl.program_id(0) * tr - length
    n_row = jnp.abs(base + lax.broadcasted_iota(jnp.int32, (1, tr), 1)).astype(F32)
    ang = fb_ref[...] * (n_row * (2.0 * math.pi / length))
    fr = fr_ref[...]
    h = w1t_ref[...] * (n_row * (1.0 / (length - 1)))
    h = h + _dot_hp(w1c_ref[...], jnp.cos(ang)) - _dot_hp(w1s_ref[...], jnp.sin(ang))
    h = jnp.sin(fr * (h + b1_ref[...]))
    h = jnp.sin(fr * (_dot_hp(w2_ref[...], h) + b2_ref[...]))
    h = jnp.sin(fr * (_dot_hp(w3_ref[...], h) + b3_ref[...]))
    lag = base + lax.broadcasted_iota(jnp.int32, (tr, 1), 0)
    t_col = jnp.abs(lag).astype(F32) * (1.0 / (length - 1))
    decay = jnp.exp(-t_col * jnp.abs(dl_ref[...]))
    for o in range(HY_ORDER):
        lo = o * 2 * HY_W
        kf = _dot_hp_t(h, w4_ref[:, lo:lo + HY_W]) * decay
        kb = _dot_hp_t(h, w4_ref[:, lo + HY_W:lo + 2 * HY_W]) * decay
        k_ref[:, o * HY_W:(o + 1) * HY_W] = jnp.where(lag >= 0, kf, jnp.where(lag == -length, 0.0, kb))


def _hyena_filter(fw, length, tr):
    bands = (HY_EMB - 1) // 2
    fbands = jnp.linspace(1e-4, bands - 1, bands, dtype=F32)[:, None]
    deltas = jnp.linspace(math.log(HY_TARGET) / HY_SLOW, math.log(HY_TARGET) / HY_FAST, HY_W, dtype=F32)[None, :]
    fw1 = fw["fw1"]
    col = lambda v: v[:, None]
    consts = (fbands, deltas, fw1[0][:, None], fw1[1:1 + bands].T, fw1[1 + bands:].T, col(fw["fb1"]), fw["fw2"].T,
              col(fw["fb2"]), fw["fw3"].T, col(fw["fb3"]), fw["fw4"], col(fw["freq"]))
    return pl.pallas_call(
        functools.partial(_filter_kernel, length=length, tr=tr),
        grid=(2 * length // tr,),
        in_specs=[pl.BlockSpec(a.shape, lambda i: (0, 0)) for a in consts],
        out_specs=pl.BlockSpec((tr, HY_ORDER * HY_W), lambda i: (i, 0)),
        out_shape=jax.ShapeDtypeStruct((2 * length, HY_ORDER * HY_W), F32),
        compiler_params=_cparams(("parallel",)),
        name="hyena_filter",
    )(*consts)


def _kspec_kernel(f_ref, k0_ref, k1_ref, o_ref, prev_ref, *, p):
    fwd = f_ref[...]

    @pl.when(pl.program_id(0) == 0)
    def _():
        prev_ref[...] = _dot_hp(fwd, k0_ref[...])

    a = _dot_hp(fwd, k1_ref[...])
    a1 = prev_ref[...]
    prev_ref[...] = a
    a1_re = a1[:p] - k0_ref[0:1, :]
    a1_im = a1[p:]
    sign = 1.0 - 2.0 * (lax.broadcasted_iota(jnp.int32, (p, 1), 0) % 2).astype(F32)
    o_ref[0, :p, :] = a[:p] - sign * a1_im
    o_ref[0, p:, :] = a[p:] + sign * a1_re


def _hyena_spectra(kfull, fwd, p):
    nb = kfull.shape[0] // (2 * p)
    cols = kfull.shape[1]
    return pl.pallas_call(
        functools.partial(_kspec_kernel, p=p),
        grid=(2 * nb - 1,),
        in_specs=[
            pl.BlockSpec(fwd.shape, lambda i: (0, 0)),
            pl.BlockSpec((p, cols), lambda i: (i, 0)),
            pl.BlockSpec((p, cols), lambda i: (i + 1, 0)),
        ],
        out_specs=pl.BlockSpec((1, 2 * p, cols), lambda i: (i, 0, 0)),
        out_shape=jax.ShapeDtypeStruct((2 * nb - 1, 2 * p, cols), F32),
        scratch_shapes=[pltpu.VMEM((2 * p, cols), F32)],
        compiler_params=_cparams(("arbitrary",)),
        name="hyena_spectra",
    )(fwd, kfull, kfull)


HY_CT = 128
HY_NB = 2
HY_RT = 32


def _hyconv_kernel(zin_ref, gate_ref, k_ref, bias_ref, fwd_ref, inv_ref, o_ref, zf_ref, ya_ref, yb_ref, *, nb, p):
    fwd = fwd_ref[...]
    for j in range(nb):
        zb = jnp.concatenate([zin_ref[s, j * p:(j + 1) * p, :] for s in range(HY_NB)], axis=1).astype(BF16)
        zf_ref[j] = _dot(fwd, zb)

    def mac(i, y_ref):
        for rt in range(p // HY_RT):
            r0 = rt * HY_RT
            acc_re = [None] * HY_NB
            acc_im = [None] * HY_NB
            for j in range(nb):
                d = i - j + (nb - 1)
                kre = k_ref[d, r0:r0 + HY_RT, :]
                kim = k_ref[d, p + r0:p + r0 + HY_RT, :]
                zre = zf_ref[j, r0:r0 + HY_RT, :]
                zim = zf_ref[j, p + r0:p + r0 + HY_RT, :]
                for s in range(HY_NB):
                    zr = zre[:, s * HY_CT:(s + 1) * HY_CT]
                    zi = zim[:, s * HY_CT:(s + 1) * HY_CT]
                    re = kre * zr - kim * zi
                    im = kre * zi + kim * zr
                    acc_re[s] = re if j == 0 else acc_re[s] + re
                    acc_im[s] = im if j == 0 else acc_im[s] + im
            y_ref[r0:r0 + HY_RT, :] = jnp.concatenate(acc_re, axis=1).astype(BF16)
            y_ref[p + r0:p + r0 + HY_RT, :] = jnp.concatenate(acc_im, axis=1).astype(BF16)

    def finish(i, y_ref):
        y = _dot(inv_ref[...], y_ref[...])
        rows = pl.ds(pl.multiple_of(i * p, p), p)
        for s in range(HY_NB):
            zs = zin_ref[s, rows, :]
            o_ref[s, rows, :] = gate_ref[s, rows, :] * (y[:, s * HY_CT:(s + 1) * HY_CT] + bias_ref[...] * zs)

    mac(0, ya_ref)
    for i in range(1, nb):
        mac(i, yb_ref if i % 2 else ya_ref)
        finish(i - 1, ya_ref if i % 2 else yb_ref)
    finish(nb - 1, ya_ref if nb % 2 else yb_ref)


def _hyconv(zin, zin_col, gate, gate_col, kspec, k_col, bias, fwd, inv, p):
    bsz, length, _ = zin.shape
    nb = length // p
    nct = HY_W // HY_CT
    nd = kspec.shape[0]
    return pl.pallas_call(
        functools.partial(_hyconv_kernel, nb=nb, p=p),
        grid=(nct, bsz // HY_NB),
        in_specs=[
            pl.BlockSpec((HY_NB, length, HY_CT), lambda c, b: (b, 0, zin_col + c)),
            pl.BlockSpec((HY_NB, length, HY_CT), lambda c, b: (b, 0, gate_col + c)),
            pl.BlockSpec((nd, 2 * p, HY_CT), lambda c, b: (0, 0, k_col + c), pipeline_mode=pl.Buffered(1)),
            pl.BlockSpec((1, HY_CT), lambda c, b: (0, c)),
            _const_spec(fwd.shape),
            _const_spec(inv.shape),
        ],
        out_specs=pl.BlockSpec((HY_NB, length, HY_CT), lambda c, b: (b, 0, c)),
        out_shape=jax.ShapeDtypeStruct((bsz, length, HY_W), F32),
        scratch_shapes=[
            pltpu.VMEM((nb, 2 * p, HY_NB * HY_CT), F32),
            pltpu.VMEM((2 * p, HY_NB * HY_CT), BF16),
            pltpu.VMEM((2 * p, HY_NB * HY_CT), BF16),
        ],
        compiler_params=_cparams(("arbitrary", "arbitrary")),
        name="hyconv",
    )(zin, gate, kspec, bias, fwd, inv)


def _ssd_dir(xbc_ref, dta_ref, dtat_ref, spread_ref, h_ref, y_ref, *, q, rev, r0):
    rows = slice(r0, r0 + q)
    n, hp, hd = SSD_STATE, SSD_HEADS // SSD_GROUPS, SSD_HEAD_DIM
    c0 = SSD_HEADS if rev else 0
    end = 0 if rev else q - 1
    li = lax.broadcasted_iota(jnp.int32, (q, q), 0)
    si = lax.broadcasted_iota(jnp.int32, (q, q), 1)
    keep = (li <= si) if rev else (li >= si)
    low = lax.broadcasted_iota(jnp.int32, (q, LANE), 1) < hd
    pairs = hp * hd // LANE
    dt_hl = jnp.concatenate(_split2(dta_ref[0, rows, :N_DT]), axis=1)
    dt_lanes = _dot(dt_hl, spread_ref[:, c0 * hd:(c0 + SSD_HEADS) * hd])
    for g in range(SSD_GROUPS):
        bg = xbc_ref[0, rows, SSD_W + g * n:SSD_W + (g + 1) * n]
        cg = xbc_ref[0, rows, SSD_W + SSD_GROUPS * n + g * n:SSD_W + SSD_GROUPS * n + (g + 1) * n]
        scores = lax.dot_general(cg, bg, (((1,), (1,)), ((), ())), preferred_element_type=F32)
        hg = h_ref[g]
        y_off = _dot(cg, hg.astype(BF16))
        ys, xws, tots = [], [], []
        for t in range(pairs):
            cols = slice(g * hp * hd + t * LANE, g * hp * hd + (t + 1) * LANE)
            acs_b = []
            for r in (2 * t, 2 * t + 1):
                c = c0 + g * hp + r
                acs_b.append(jnp.broadcast_to(dta_ref[0, rows, N_DT + c:N_DT + c + 1], (q, LANE)))
            acs2 = jnp.where(low, acs_b[0], acs_b[1])
            xdt = xbc_ref[0, rows, cols].astype(F32) * dt_lanes[:, cols]
            yd = None
            for i, r in enumerate((2 * t, 2 * t + 1)):
                c = c0 + g * hp + r
                seg = jnp.concatenate([acs_b[i]] * (q // LANE), axis=1) - dtat_ref[0, N_DT + c:N_DT + c + 1, rows]
                m = (jnp.exp2(jnp.where(keep, seg, -1e30)) * scores).astype(BF16)
                xh = jnp.where(low if i == 0 else jnp.logical_not(low), xdt, 0.0).astype(BF16)
                part = _dot(m, xh)
                yd = part if yd is None else yd + part
            ys.append(yd + jnp.exp2(acs2) * y_off[:, t * LANE:(t + 1) * LANE])
            tot = acs2[end:end + 1, :]
            xws.append((xdt * jnp.exp2(tot - acs2)).astype(BF16))
            tots.append(tot)
        y_ref[0, rows, g * hp * hd:(g + 1) * hp * hd] = jnp.concatenate(ys, axis=1).astype(y_ref.dtype)
        upd = lax.dot_general(bg, jnp.concatenate(xws, axis=1), (((0,), (0,)), ((), ())),
                              preferred_element_type=F32)
        h_ref[g] = hg * jnp.exp2(jnp.concatenate(tots, axis=1)) + upd


def _dt_spread():
    k = np.arange(2 * N_DT)[:, None] % N_DT
    lane_head = np.arange(2 * SSD_W)[None, :] // SSD_HEAD_DIM
    return jnp.asarray(k == lane_head, BF16)


def _ssd_kernel(xf_ref, dtaf_ref, dtatf_ref, xb_ref, dtab_ref, dtatb_ref, spread_ref, h0f_ref, h0b_ref,
                yf_ref, yb_ref, hf_out, hb_out, hf_ref, hb_ref, *, q):
    k = pl.program_id(1)

    @pl.when(k == 0)
    def _():
        hf_ref[...] = h0f_ref[0]
        hb_ref[...] = h0b_ref[0]

    cps = xf_ref.shape[1] // q
    for j in range(cps):
        _ssd_dir(xf_ref, dtaf_ref, dtatf_ref, spread_ref, hf_ref, yf_ref, q=q, rev=False, r0=j * q)
        _ssd_dir(xb_ref, dtab_ref, dtatb_ref, spread_ref, hb_ref, yb_ref, q=q, rev=True, r0=(cps - 1 - j) * q)

    @pl.when(k == pl.num_programs(1) - 1)
    def _():
        hf_out[0] = hf_ref[...]
        hb_out[0] = hb_ref[...]


def _ssd(xbc, dta, dtat, h0f, h0b, q, cps):
    bsz, length, _ = xbc.shape
    qs = q * cps
    nc = length // qs
    hshape = (SSD_GROUPS, SSD_STATE, SSD_W // SSD_GROUPS)
    fw = lambda b, k: (b, k, 0)
    bw = lambda b, k: (b, nc - 1 - k, 0)
    st = pl.BlockSpec((1,) + hshape, lambda b, k: (b, 0, 0, 0))
    return pl.pallas_call(
        functools.partial(_ssd_kernel, q=q),
        grid=(bsz, nc),
        in_specs=[
            pl.BlockSpec((1, qs, SSD_XBC), fw),
            pl.BlockSpec((1, qs, 2 * N_DT), fw),
            pl.BlockSpec((1, 2 * N_DT, qs), lambda b, k: (b, 0, k)),
            pl.BlockSpec((1, qs, SSD_XBC), bw),
            pl.BlockSpec((1, qs, 2 * N_DT), bw),
            pl.BlockSpec((1, 2 * N_DT, qs), lambda b, k: (b, 0, nc - 1 - k)),
            pl.BlockSpec((2 * N_DT, 2 * SSD_W), lambda b, k: (0, 0)),
            st, st,
        ],
        out_specs=[pl.BlockSpec((1, qs, SSD_W), fw), pl.BlockSpec((1, qs, SSD_W), bw), st, st],
        out_shape=[
            jax.ShapeDtypeStruct((bsz, length, SSD_W), BF16),
            jax.ShapeDtypeStruct((bsz, length, SSD_W), BF16),
            jax.ShapeDtypeStruct((bsz,) + hshape, F32),
            jax.ShapeDtypeStruct((bsz,) + hshape, F32),
        ],
        scratch_shapes=[pltpu.VMEM(hshape, F32), pltpu.VMEM(hshape, F32)],
        compiler_params=_cparams(("parallel", "arbitrary")),
        name="ssd",
    )(xbc, dta, dtat, xbc, dta, dtat, _dt_spread(), h0f, h0b)


def _outproj_kernel(x_ref, mod_ref, g_ref, hy_ref, yf_ref, yb_ref, xs_ref, z_ref, sc_ref, dexp_ref, mg_ref,
                    why_ref, wssd_ref, wsc_ref, wg_ref, wu_ref, wo_ref, o_ref, *, bounds):
    x = x_ref[0]
    mod = mod_ref[0]
    mg = mg_ref[...]
    y_hy = _rms(hy_ref[0], mg[:, :HY_W]).astype(BF16)
    y = yf_ref[0].astype(F32) + yb_ref[0].astype(F32) + dexp_ref[...] * xs_ref[0].astype(F32)
    y_ssd = _rms(y * _silu(z_ref[0].astype(F32)), mg[:, HY_W:HY_W + SSD_W]).astype(BF16)
    o = _dot(y_hy, why_ref[...]) + _dot(y_ssd, wssd_ref[...]) + _dot(sc_ref[0], wsc_ref[...])
    x = x + _rms(o, mod[5:6] * g_ref[3:4])
    o_ref[0] = _ffn_rows(x, mod, g_ref[...], wg_ref, wu_ref, wo_ref, 2, bounds)


def _outproj(x, mod, g, hy, yf, yb, xbc, z, sc, lw, ffn_w, tm):
    bsz, length, d = x.shape
    per_batch = mod.shape[0] != 1
    tok = lambda w: pl.BlockSpec((1, tm, w), lambda b, t: (b, t, 0))
    consts = (lw["d_exp"], lw["mix_gain"], lw["w_out_hy"], lw["w_out_ssd"], lw["w_out_sc"])
    w_in_all, w_out_all = ffn_w[:2]
    return pl.pallas_call(
        functools.partial(_outproj_kernel, bounds=_ffn_bounds(w_out_all.shape[2], FFN_CHUNK)),
        grid=(bsz, length // tm),
        in_specs=[
            tok(d),
            pl.BlockSpec((1, N_MOD, d), (lambda b, t: (b, 0, 0)) if per_batch else (lambda b, t: (0, 0, 0))),
            _const_spec(g.shape),
            tok(HY_W), tok(SSD_W), tok(SSD_W), tok(SSD_W), tok(SSD_W), tok(SC_W),
        ] + [_const_spec(a.shape) for a in consts] + _ffn_weight_specs(*ffn_w),
        out_specs=tok(d),
        out_shape=jax.ShapeDtypeStruct(x.shape, F32),
        compiler_params=_cparams(("parallel", "parallel")),
        name="outproj",
    )(x, mod, g, hy, yf, yb, xbc, z, sc, *consts, w_in_all, w_in_all, w_out_all)


def _mixer(xs, mod, g, lw, ffn_w, kspec, dft, seg, tm, p, q, h0f, h0b, batch_shape, want_out):
    bsz, length = batch_shape
    hy, z, xbc, sc, dta, dtat = _inproj(xs, mod, g, lw, seg, tm, q)
    if xs.shape[0] != bsz:
        dtat = dtat.reshape(2 * N_DT, bsz, length).transpose(1, 0, 2)
    seq = lambda a: a.reshape(bsz, length, a.shape[-1])
    yf, yb, hf, hb = _ssd(seq(xbc), seq(dta), dtat, h0f, h0b, q, _pick(length // q, (SSD_CPS, 4, 2, 1)))
    if not want_out:
        return None, hf, hb
    fwd, inv = dft
    nct = HY_W // HY_CT
    hy_s = seq(hy)
    fwd_b, inv_b = fwd.astype(BF16), inv.astype(BF16)
    z1 = _hyconv(hy_s, 0, hy_s, nct, kspec, 0, lw["hy_bias"][0:1], fwd_b, inv_b, p)
    z2 = _hyconv(z1, 0, hy_s, 2 * nct, kspec, nct, lw["hy_bias"][1:2], fwd_b, inv_b, p)
    tok = lambda a: a.reshape(xs.shape[0], xs.shape[1], a.shape[-1])
    out = _outproj(xs, mod, g, tok(z2), tok(yf), tok(yb), xbc, z, sc, lw, ffn_w, tm)
    return out, hf, hb


def _layer_weights(l, w_in, w_out, hy_conv_w, hy_conv_b, hy_bias, ssd_conv_w, ssd_conv_b, ssd_a_log, ssd_dt_bias,
                   ssd_d, sc_conv_w, mix_gain):
    o_dt = HY_IN + SSD_W + SSD_XBC
    wl = w_in[l]
    return {
        "w_in": jnp.concatenate([wl[:, :o_dt], wl[:, o_dt + N_DT:]], axis=1).astype(BF16),
        "w_dtt": wl[:, o_dt:o_dt + N_DT].astype(BF16).T,
        "hy_conv_w": hy_conv_w[l],
        "hy_conv_b": hy_conv_b[l][None],
        "ssd_conv_w": ssd_conv_w[l],
        "ssd_conv_b": ssd_conv_b[l][None],
        "sc_conv_w": sc_conv_w[l],
        "sc_gain": mix_gain[l][None, HY_W + SSD_W:],
        "dt_bias_t": ssd_dt_bias[l].reshape(N_DT, 1),
        "hy_bias": hy_bias[l],
        "a2_col": (-jnp.exp(ssd_a_log[l]) * math.log2(math.e)).reshape(N_DT, 1),
        "d_exp": jnp.repeat(ssd_d[l], SSD_HEAD_DIM)[None],
        "mix_gain": mix_gain[l][None],
        "w_out_hy": w_out[l, :HY_W].astype(BF16),
        "w_out_ssd": w_out[l, HY_W:HY_W + SSD_W].astype(BF16),
        "w_out_sc": w_out[l, HY_W + SSD_W:].astype(BF16),
    }


def _pick(n, prefs):
    for v in prefs:
        if n % v == 0:
            return v
    return n


def kernel(x, c, ctx, c_ctx, w_mod, b_mod, norm_g, ffn_w_in, ffn_w_out, w_in, w_out, hy_conv_w, hy_conv_b, hy_fw1,
           hy_fb1, hy_fw2, hy_fb2, hy_fw3, hy_fb3, hy_fw4, hy_freq, hy_bias, ssd_conv_w, ssd_conv_b, ssd_a_log,
           ssd_dt_bias, ssd_d, sc_conv_w, mix_gain):
    bsz, n_lat, d = x.shape
    _, ctx_len, _ = ctx.shape
    depth = w_mod.shape[0]
    ffn_in_b, ffn_out_b = ffn_w_in.astype(BF16), ffn_w_out.astype(BF16)

    rows = -(-(bsz + 1) // SUBLANE) * SUBLANE
    cvec = jnp.zeros((rows, d), F32).at[:bsz].set(c).at[bsz].set(c_ctx)
    mod_all = _mod_table(cvec, w_mod, b_mod).reshape(depth, rows, N_MOD, d)

    tm = _pick(n_lat, (ROW_TILE, 256, 128, GRID_W))
    p_lat = _pick(n_lat, (HY_BLOCK, 256, 128))
    q_lat = _pick(n_lat, (SSD_CHUNK, 128))
    xc = ctx.reshape(1, bsz * ctx_len, d)
    tm_c = ctx_len * _pick(bsz, (2, 1)) if ctx_len < ROW_TILE else _pick(ctx_len, (ROW_TILE, 256))
    p_ctx = _pick(ctx_len, (HY_BLOCK, 256, 128))
    q_ctx = _pick(ctx_len, (SSD_CHUNK, 128))
    dft_lat = _dft_mats(p_lat)
    dft_ctx = _dft_mats(p_ctx)

    h_zero = jnp.zeros((bsz, SSD_GROUPS, SSD_STATE, SSD_W // SSD_GROUPS), F32)
    for l in range(depth):
        last = l == depth - 1
        mod_x = mod_all[l, :bsz]
        mod_c = mod_all[l, bsz:bsz + 1]
        g = norm_g[l]
        lw = _layer_weights(l, w_in, w_out, hy_conv_w, hy_conv_b, hy_bias, ssd_conv_w, ssd_conv_b, ssd_a_log,
                            ssd_dt_bias, ssd_d, sc_conv_w, mix_gain)
        fw = {"fw1": hy_fw1[l], "fb1": hy_fb1[l], "fw2": hy_fw2[l], "fb2": hy_fb2[l], "fw3": hy_fw3[l],
              "fb3": hy_fb3[l], "fw4": hy_fw4[l], "freq": hy_freq[l]}
        ffn_w = [(ffn_in_b, ffn_out_b, l, i) for i in range(2)]

        x = _ffn(x, mod_x, g, ffn_w[0], 0, tm)
        xc = _ffn(xc, mod_c, g, ffn_w[0], 0, tm_c)

        ks_ctx = None
        if not last:
            ks_ctx = _hyena_spectra(_hyena_filter(fw, ctx_len, p_ctx), dft_ctx[0], p_ctx)
        xc, h_f, h_b = _mixer(xc, mod_c, g, lw, ffn_w[1], ks_ctx, dft_ctx, ctx_len, tm_c, p_ctx, q_ctx, h_zero,
                              h_zero, (bsz, ctx_len), not last)
        ks_lat = _hyena_spectra(_hyena_filter(fw, n_lat, p_lat), dft_lat[0], p_lat)
        x, _, _ = _mixer(x, mod_x, g, lw, ffn_w[1], ks_lat, dft_lat, GRID_W, tm, p_lat, q_lat, h_f, h_b,
                         (bsz, n_lat), True)
    return x
```

```python
import functools
import math

import jax
import jax.numpy as jnp
import numpy as np
from jax import lax
from jax.experimental import pallas as pl
from jax.experimental.pallas import tpu as pltpu

GRID_W = 64
EPS = 1e-6
FFN_RES_W = 0.5
N_MOD = 9
HY_W = 256
SSD_HEADS = 8
SSD_HEAD_DIM = 64
SSD_GROUPS = 2
SSD_STATE = 128
SC_W = 256
HY_ORDER = 2
HY_EMB = 33
HY_FAST = 0.3
HY_SLOW = 1.5
HY_TARGET = 1e-2

SSD_W = SSD_HEADS * SSD_HEAD_DIM
HY_IN = 3 * HY_W
SSD_XBC = SSD_W + 2 * SSD_GROUPS * SSD_STATE
SC_IN = 3 * SC_W
N_DT = 2 * SSD_HEADS

LANE = 128
SUBLANE = 8
MXU_TILE = 256
FFN_CHUNK = 3 * MXU_TILE
INPROJ_CHUNK = 2 * MXU_TILE
SSD_CPS = 8
ROW_TILE = 512
INPROJ_ROWS = 1024
HY_BLOCK = 512
SSD_CHUNK = 256
VMEM_LIMIT = 56 * 1024 * 1024

F32 = jnp.float32
BF16 = jnp.bfloat16


def _cparams(sem):
    return pltpu.CompilerParams(dimension_semantics=sem, vmem_limit_bytes=VMEM_LIMIT)


def _const_spec(shape):
    nd = len(shape)
    return pl.BlockSpec(shape, lambda *_: (0,) * nd, pipeline_mode=pl.Buffered(1))


def _dot(a, b):
    return jnp.dot(a, b, preferred_element_type=F32)


def _split2(a):
    hi = a.astype(BF16)
    lo = (a - hi.astype(F32)).astype(BF16)
    return hi, lo


def _dot_hp(a, b):
    ah, al = _split2(a)
    bh, bl = _split2(b)
    return _dot(ah, bh) + (_dot(ah, bl) + _dot(al, bh))


def _dot_hp_t(a_t, b):
    dn = (((0,), (0,)), ((), ()))
    ah, al = _split2(a_t)
    bh, bl = _split2(b)
    dg = lambda u, v: lax.dot_general(u, v, dn, preferred_element_type=F32)
    return dg(ah, bh) + (dg(ah, bl) + dg(al, bh))


def _silu(x):
    hx = 0.5 * x
    return hx + hx * jnp.tanh(hx)


def _rms(x, g):
    ms = jnp.mean(x * x, axis=-1, keepdims=True)
    return x * (lax.rsqrt(ms + EPS) * g)


def _mod_kernel(c_ref, w_ref, b_ref, o_ref):
    o_ref[0] = _dot_hp(_silu(c_ref[...]), w_ref[0]) + b_ref[0]


def _mod_table(cvec, w_mod, b_mod):
    depth, d, nm = w_mod.shape
    r = cvec.shape[0]
    tn = 1024 if nm % 1024 == 0 else d
    return pl.pallas_call(
        _mod_kernel,
        grid=(depth, nm // tn),
        in_specs=[
            pl.BlockSpec((r, d), lambda l, n: (0, 0)),
            pl.BlockSpec((1, d, tn), lambda l, n: (l, 0, n)),
            pl.BlockSpec((1, 1, tn), lambda l, n: (l, 0, n)),
        ],
        out_specs=pl.BlockSpec((1, r, tn), lambda l, n: (l, 0, n)),
        out_shape=jax.ShapeDtypeStruct((depth, r, nm), F32),
        compiler_params=_cparams(("parallel", "parallel")),
        name="mod_table",
    )(cvec, w_mod, b_mod.reshape(depth, 1, nm))


def _ffn_rows(x, mod, g, wg_ref, wu_ref, wo_ref, sub, bounds):
    shift, scale, gate = mod[3 * sub:3 * sub + 1], mod[3 * sub + 1:3 * sub + 2], mod[3 * sub + 2:3 * sub + 3]
    g_pre, g_post = g[2 * sub:2 * sub + 1], g[2 * sub + 1:2 * sub + 2]
    h = (_rms(x, g_pre * (1.0 + scale)) + shift).astype(BF16)
    acc = None
    for lo, hi in zip(bounds[:-1], bounds[1:]):
        gg = _dot(h, wg_ref[:, lo:hi])
        uu = _dot(h, wu_ref[:, lo:hi])
        a = (_silu(gg) * uu).astype(BF16)
        part = _dot(a, wo_ref[lo:hi, :])
        acc = part if acc is None else acc + part
    return x + _rms(acc, FFN_RES_W * gate * g_post)


def _ffn_kernel(x_ref, mod_ref, g_ref, wg_ref, wu_ref, wo_ref, o_ref, *, sub, bounds):
    o_ref[0] = _ffn_rows(x_ref[0], mod_ref[0], g_ref[...], wg_ref, wu_ref, wo_ref, sub, bounds)


def _ffn_bounds(dff, width):
    return tuple(range(0, dff, width)) + (dff,)


def _ffn_weight_specs(w_in_all, w_out_all, l, i):
    d, dff = w_out_all.shape[3], w_out_all.shape[2]
    one = pl.Buffered(1)
    return [
        pl.BlockSpec((None, None, d, dff), lambda *_: (l, i, 0, 0), pipeline_mode=one),
        pl.BlockSpec((None, None, d, dff), lambda *_: (l, i, 0, 1), pipeline_mode=one),
        pl.BlockSpec((None, None, dff, d), lambda *_: (l, i, 0, 0), pipeline_mode=one),
    ]


def _ffn(x, mod, g, ffn_w, sub, tm):
    w_in_all, w_out_all, l, i = ffn_w
    bsz, length, d = x.shape
    per_batch = mod.shape[0] != 1
    return pl.pallas_call(
        functools.partial(_ffn_kernel, sub=sub, bounds=_ffn_bounds(w_out_all.shape[2], FFN_CHUNK)),
        grid=(bsz, length // tm),
        in_specs=[
            pl.BlockSpec((1, tm, d), lambda b, t: (b, t, 0)),
            pl.BlockSpec((1, N_MOD, d), (lambda b, t: (b, 0, 0)) if per_batch else (lambda b, t: (0, 0, 0))),
            _const_spec(g.shape),
        ] + _ffn_weight_specs(*ffn_w),
        out_specs=pl.BlockSpec((1, tm, d), lambda b, t: (b, t, 0)),
        out_shape=jax.ShapeDtypeStruct(x.shape, F32),
        compiler_params=_cparams(("parallel", "parallel")),
        name="ffn",
    )(x, mod, g, w_in_all, w_in_all, w_out_all)


def _conv3(a, w, seg, scr):
    n, c = a.shape
    scr[SUBLANE:SUBLANE + n, :c] = a
    row = lax.broadcasted_iota(jnp.int32, (n, c), 0) % seg
    prev = jnp.where(row == 0, 0.0, scr[SUBLANE - 1:SUBLANE - 1 + n, :c])
    nxt = jnp.where(row == seg - 1, 0.0, scr[SUBLANE + 1:SUBLANE + 1 + n, :c])
    return (prev * w[0:1] + a * w[1:2]) + nxt * w[2:3]


def _chunk_scan(da, q, axis):
    n = da.shape[axis]
    pos = lax.broadcasted_iota(jnp.int32, da.shape, axis) % q
    head = lax.broadcasted_iota(jnp.int32, da.shape, 1 - axis)
    up, down = da, da
    s = 1
    while s < q:
        up = up + jnp.where(pos >= s, pltpu.roll(up, s, axis), 0.0)
        down = down + jnp.where(pos < q - s, pltpu.roll(down, n - s, axis), 0.0)
        s *= 2
    return jnp.where(head < SSD_HEADS, up, down)


def _inproj_kernel(x_ref, mod_ref, g_ref, w_ref, wdtt_ref, hyw_ref, hyb_ref, sw_ref, sb_ref, scw_ref,
                   scg_ref, dtbt_ref, a2c_ref, hy_ref, z_ref, xbc_ref, sc_ref, dta_ref, dtat_ref, scr_ref, *, seg, q):
    x = x_ref[0]
    mod = mod_ref[0]
    pad = jnp.zeros((SUBLANE, scr_ref.shape[1]), F32)
    scr_ref[:SUBLANE, :] = pad
    scr_ref[scr_ref.shape[0] - SUBLANE:, :] = pad
    h = (_rms(x, g_ref[2:3] * (1.0 + mod[4:5])) + mod[3:4]).astype(BF16)
    dtt = lax.dot_general(wdtt_ref[...], h, (((1,), (1,)), ((), ())), preferred_element_type=F32)
    dtt = jax.nn.softplus(dtt + dtbt_ref[...])
    rowform = jnp.concatenate([dtt, _chunk_scan(dtt * a2c_ref[...], q, 1)], axis=0)
    dtat_ref[0] = rowform
    pad = jnp.zeros((LANE - 2 * N_DT, rowform.shape[1]), F32)
    dta_ref[0] = jnp.concatenate([rowform, pad], axis=0).T[:, :2 * N_DT]
    o_z, o_xbc, o_sc = HY_IN, HY_IN + SSD_W, HY_IN + SSD_W + SSD_XBC

    def post_hy(p, lo, hi):
        hy_ref[0, :, lo:hi] = _conv3(p, hyw_ref[:, lo:hi], seg, scr_ref) + hyb_ref[:, lo:hi]

    def post_z(p, lo, hi):
        z_ref[0, :, lo - o_z:hi - o_z] = p.astype(BF16)

    def post_xbc(p, lo, hi):
        lo, hi = lo - o_xbc, hi - o_xbc
        xbc_ref[0, :, lo:hi] = _silu(_conv3(p, sw_ref[:, lo:hi], seg, scr_ref) + sb_ref[:, lo:hi]).astype(BF16)

    def post_sc(p, lo, hi):
        gb, gc, hx = p[:, :SC_W], p[:, SC_W:2 * SC_W], p[:, 2 * SC_W:]
        sc_ref[0] = _rms(gb * _conv3(gc * hx, scw_ref[...], seg, scr_ref), scg_ref[...]).astype(BF16)

    stages = []
    for post, lo, hi in ((post_hy, 0, o_z), (post_z, o_z, o_xbc), (post_xbc, o_xbc, o_sc)):
        stages += [(post, c, min(c + INPROJ_CHUNK, hi)) for c in range(lo, hi, INPROJ_CHUNK)]
    stages.append((post_sc, o_sc, o_sc + SC_IN))
    pending = None
    for post, lo, hi in stages:
        p = _dot(h, w_ref[:, lo:hi])
        if pending is not None:
            pending[0](*pending[1:])
        pending = (post, p, lo, hi)
    pending[0](*pending[1:])


def _inproj(x, mod, g, lw, seg, tm, q):
    bsz, length, d = x.shape
    per_batch = mod.shape[0] != 1
    tok = lambda w: pl.BlockSpec((1, tm, w), lambda b, t: (b, t, 0))
    consts = (lw["w_in"], lw["w_dtt"], lw["hy_conv_w"], lw["hy_conv_b"], lw["ssd_conv_w"],
              lw["ssd_conv_b"], lw["sc_conv_w"], lw["sc_gain"], lw["dt_bias_t"], lw["a2_col"])
    return pl.pallas_call(
        functools.partial(_inproj_kernel, seg=seg, q=q),
        grid=(bsz, length // tm),
        in_specs=[
            tok(d),
            pl.BlockSpec((1, N_MOD, d), (lambda b, t: (b, 0, 0)) if per_batch else (lambda b, t: (0, 0, 0))),
            _const_spec(g.shape),
        ] + [_const_spec(a.shape) for a in consts],
        out_specs=[tok(HY_IN), tok(SSD_W), tok(SSD_XBC), tok(SC_W), tok(2 * N_DT),
                   pl.BlockSpec((1, 2 * N_DT, tm), lambda b, t: (b, 0, t))],
        out_shape=[
            jax.ShapeDtypeStruct((bsz, length, HY_IN), F32),
            jax.ShapeDtypeStruct((bsz, length, SSD_W), BF16),
            jax.ShapeDtypeStruct((bsz, length, SSD_XBC), BF16),
            jax.ShapeDtypeStruct((bsz, length, SC_W), BF16),
            jax.ShapeDtypeStruct((bsz, length, 2 * N_DT), F32),
            jax.ShapeDtypeStruct((bsz, 2 * N_DT, length), F32),
        ],
        scratch_shapes=[pltpu.VMEM((tm + 2 * SUBLANE, INPROJ_CHUNK), F32)],
        compiler_params=_cparams(("parallel", "parallel")),
        name="inproj",
    )(x, mod, g, *consts)


def _dft_mats(p):
    f = np.arange(p, dtype=np.int64)[:, None]
    t = np.arange(p, dtype=np.int64)[None, :]
    ang = (np.pi / (2 * p)) * (((2 * f + 1) * t) % (4 * p)).astype(np.float64)
    fwd = np.concatenate([np.cos(ang), -np.sin(ang)], axis=0)
    return jnp.asarray(fwd, F32), jnp.asarray(fwd.T / p, F32)


def _filter_kernel(fb_ref, dl_ref, w1t_ref, w1c_ref, w1s_ref, b1_ref, w2_ref, b2_ref, w3_ref, b3_ref, w4_ref,
                   fr_ref, k_ref, *, length, tr):
    base = pl.program_id(0) * tr - length
    n_row = jnp.abs(base + lax.broadcasted_iota(jnp.int32, (1, tr), 1)).astype(F32)
    ang = fb_ref[...] * (n_row * (2.0 * math.pi / length))
    fr = fr_ref[...]
    h = w1t_ref[...] * (n_row * (1.0 / (length - 1)))
    h = h + _dot_hp(w1c_ref[...], jnp.cos(ang)) - _dot_hp(w1s_ref[...], jnp.sin(ang))
    h = jnp.sin(fr * (h + b1_ref[...]))
    h = jnp.sin(fr * (_dot_hp(w2_ref[...], h) + b2_ref[...]))
    h = jnp.sin(fr * (_dot_hp(w3_ref[...], h) + b3_ref[...]))
    lag = base + lax.broadcasted_iota(jnp.int32, (tr, 1), 0)
    t_col = jnp.abs(lag).astype(F32) * (1.0 / (length - 1))
    decay = jnp.exp(-t_col * jnp.abs(dl_ref[...]))
    for o in range(HY_ORDER):
        lo = o * 2 * HY_W
        kf = _dot_hp_t(h, w4_ref[:, lo:lo + HY_W]) * decay
        kb = _dot_hp_t(h, w4_ref[:, lo + HY_W:lo + 2 * HY_W]) * decay
        k_ref[:, o * HY_W:(o + 1) * HY_W] = jnp.where(lag >= 0, kf, jnp.where(lag == -length, 0.0, kb))


def _hyena_filter(fw, length, tr):
    bands = (HY_EMB - 1) // 2
    fbands = jnp.linspace(1e-4, bands - 1, bands, dtype=F32)[:, None]
    deltas = jnp.linspace(math.log(HY_TARGET) / HY_SLOW, math.log(HY_TARGET) / HY_FAST, HY_W, dtype=F32)[None, :]
    fw1 = fw["fw1"]
    col = lambda v: v[:, None]
    consts = (fbands, deltas, fw1[0][:, None], fw1[1:1 + bands].T, fw1[1 + bands:].T, col(fw["fb1"]), fw["fw2"].T,
              col(fw["fb2"]), fw["fw3"].T, col(fw["fb3"]), fw["fw4"], col(fw["freq"]))
    return pl.pallas_call(
        functools.partial(_filter_kernel, length=length, tr=tr),
        grid=(2 * length // tr,),
        in_specs=[pl.BlockSpec(a.shape, lambda i: (0, 0)) for a in consts],
        out_specs=pl.BlockSpec((tr, HY_ORDER * HY_W), lambda i: (i, 0)),
        out_shape=jax.ShapeDtypeStruct((2 * length, HY_ORDER * HY_W), F32),
        compiler_params=_cparams(("parallel",)),
        name="hyena_filter",
    )(*consts)


def _kspec_kernel(f_ref, k0_ref, k1_ref, o_ref, prev_ref, *, p):
    fwd = f_ref[...]

    @pl.when(pl.program_id(0) == 0)
    def _():
        prev_ref[...] = _dot_hp(fwd, k0_ref[...])

    a = _dot_hp(fwd, k1_ref[...])
    a1 = prev_ref[...]
    prev_ref[...] = a
    a1_re = a1[:p] - k0_ref[0:1, :]
    a1_im = a1[p:]
    sign = 1.0 - 2.0 * (lax.broadcasted_iota(jnp.int32, (p, 1), 0) % 2).astype(F32)
    o_ref[0, :p, :] = a[:p] - sign * a1_im
    o_ref[0, p:, :] = a[p:] + sign * a1_re


def _hyena_spectra(kfull, fwd, p):
    nb = kfull.shape[0] // (2 * p)
    cols = kfull.shape[1]
    return pl.pallas_call(
        functools.partial(_kspec_kernel, p=p),
        grid=(2 * nb - 1,),
        in_specs=[
            pl.BlockSpec(fwd.shape, lambda i: (0, 0)),
            pl.BlockSpec((p, cols), lambda i: (i, 0)),
            pl.BlockSpec((p, cols), lambda i: (i + 1, 0)),
        ],
        out_specs=pl.BlockSpec((1, 2 * p, cols), lambda i: (i, 0, 0)),
        out_shape=jax.ShapeDtypeStruct((2 * nb - 1, 2 * p, cols), F32),
        scratch_shapes=[pltpu.VMEM((2 * p, cols), F32)],
        compiler_params=_cparams(("arbitrary",)),
        name="hyena_spectra",
    )(fwd, kfull, kfull)


HY_CT = 128
HY_NB = 2
HY_RT = 32


def _hyconv_kernel(zin_ref, gate_ref, k_ref, bias_ref, fwd_ref, inv_ref, o_ref, zf_ref, ya_ref, yb_ref, *, nb, p):
    fwd = fwd_ref[...]
    for j in range(nb):
        zb = jnp.concatenate([zin_ref[s, j * p:(j + 1) * p, :] for s in range(HY_NB)], axis=1).astype(BF16)
        zf_ref[j] = _dot(fwd, zb)

    def mac(i, y_ref):
        for rt in range(p // HY_RT):
            r0 = rt * HY_RT
            acc_re = [None] * HY_NB
            acc_im = [None] * HY_NB
            for j in range(nb):
                d = i - j + (nb - 1)
                kre = k_ref[d, r0:r0 + HY_RT, :]
                kim = k_ref[d, p + r0:p + r0 + HY_RT, :]
                zre = zf_ref[j, r0:r0 + HY_RT, :]
                zim = zf_ref[j, p + r0:p + r0 + HY_RT, :]
                for s in range(HY_NB):
                    zr = zre[:, s * HY_CT:(s + 1) * HY_CT]
                    zi = zim[:, s * HY_CT:(s + 1) * HY_CT]
                    re = kre * zr - kim * zi
                    im = kre * zi + kim * zr
                    acc_re[s] = re if j == 0 else acc_re[s] + re
                    acc_im[s] = im if j == 0 else acc_im[s] + im
            y_ref[r0:r0 + HY_RT, :] = jnp.concatenate(acc_re, axis=1).astype(BF16)
            y_ref[p + r0:p + r0 + HY_RT, :] = jnp.concatenate(acc_im, axis=1).astype(BF16)

    def finish(i, y_ref):
        y = _dot(inv_ref[...], y_ref[...])
        rows = pl.ds(pl.multiple_of(i * p, p), p)
        for s in range(HY_NB):
            zs = zin_ref[s, rows, :]
            o_ref[s, rows, :] = gate_ref[s, rows, :] * (y[:, s * HY_CT:(s + 1) * HY_CT] + bias_ref[...] * zs)

    mac(0, ya_ref)
    for i in range(1, nb):
        mac(i, yb_ref if i % 2 else ya_ref)
        finish(i - 1, ya_ref if i % 2 else yb_ref)
    finish(nb - 1, ya_ref if nb % 2 else yb_ref)


def _hyconv(zin, zin_col, gate, gate_col, kspec, k_col, bias, fwd, inv, p):
    bsz, length, _ = zin.shape
    nb = length // p
    nct = HY_W // HY_CT
    nd = kspec.shape[0]
    return pl.pallas_call(
        functools.partial(_hyconv_kernel, nb=nb, p=p),
        grid=(nct, bsz // HY_NB),
        in_specs=[
            pl.BlockSpec((HY_NB, length, HY_CT), lambda c, b: (b, 0, zin_col + c)),
            pl.BlockSpec((HY_NB, length, HY_CT), lambda c, b: (b, 0, gate_col + c)),
            pl.BlockSpec((nd, 2 * p, HY_CT), lambda c, b: (0, 0, k_col + c), pipeline_mode=pl.Buffered(1)),
            pl.BlockSpec((1, HY_CT), lambda c, b: (0, c)),
            _const_spec(fwd.shape),
            _const_spec(inv.shape),
        ],
        out_specs=pl.BlockSpec((HY_NB, length, HY_CT), lambda c, b: (b, 0, c)),
        out_shape=jax.ShapeDtypeStruct((bsz, length, HY_W), F32),
        scratch_shapes=[
            pltpu.VMEM((nb, 2 * p, HY_NB * HY_CT), F32),
            pltpu.VMEM((2 * p, HY_NB * HY_CT), BF16),
            pltpu.VMEM((2 * p, HY_NB * HY_CT), BF16),
        ],
        compiler_params=_cparams(("arbitrary", "arbitrary")),
        name="hyconv",
    )(zin, gate, kspec, bias, fwd, inv)


def _ssd_dir(xbc_ref, dta_ref, dtat_ref, spread_ref, h_ref, y_ref, *, q, rev, r0):
    rows = slice(r0, r0 + q)
    n, hp, hd = SSD_STATE, SSD_HEADS // SSD_GROUPS, SSD_HEAD_DIM
    c0 = SSD_HEADS if rev else 0
    end = 0 if rev else q - 1
    li = lax.broadcasted_iota(jnp.int32, (q, q), 0)
    si = lax.broadcasted_iota(jnp.int32, (q, q), 1)
    keep = (li <= si) if rev else (li >= si)
    low = lax.broadcasted_iota(jnp.int32, (q, LANE), 1) < hd
    pairs = hp * hd // LANE
    dt_hl = jnp.concatenate(_split2(dta_ref[0, rows, :N_DT]), axis=1)
    dt_lanes = _dot(dt_hl, spread_ref[:, c0 * hd:(c0 + SSD_HEADS) * hd])
    for g in range(SSD_GROUPS):
        bg = xbc_ref[0, rows, SSD_W + g * n:SSD_W + (g + 1) * n]
        cg = xbc_ref[0, rows, SSD_W + SSD_GROUPS * n + g * n:SSD_W + SSD_GROUPS * n + (g + 1) * n]
        scores = lax.dot_general(cg, bg, (((1,), (1,)), ((), ())), preferred_element_type=F32)
        hg = h_ref[g]
        y_off = _dot(cg, hg.astype(BF16))
        ys, xws, tots = [], [], []
        for t in range(pairs):
            cols = slice(g * hp * hd + t * LANE, g * hp * hd + (t + 1) * LANE)
            acs_b = []
            for r in (2 * t, 2 * t + 1):
                c = c0 + g * hp + r
                acs_b.append(jnp.broadcast_to(dta_ref[0, rows, N_DT + c:N_DT + c + 1], (q, LANE)))
            acs2 = jnp.where(low, acs_b[0], acs_b[1])
            xdt = xbc_ref[0, rows, cols].astype(F32) * dt_lanes[:, cols]
            yd = None
            for i, r in enumerate((2 * t, 2 * t + 1)):
                c = c0 + g * hp + r
                seg = jnp.concatenate([acs_b[i]] * (q // LANE), axis=1) - dtat_ref[0, N_DT + c:N_DT + c + 1, rows]
                m = (jnp.exp2(jnp.where(keep, seg, -1e30)) * scores).astype(BF16)
                xh = jnp.where(low if i == 0 else jnp.logical_not(low), xdt, 0.0).astype(BF16)
                part = _dot(m, xh)
                yd = part if yd is None else yd + part
            ys.append(yd + jnp.exp2(acs2) * y_off[:, t * LANE:(t + 1) * LANE])
            tot = acs2[end:end + 1, :]
            xws.append((xdt * jnp.exp2(tot - acs2)).astype(BF16))
            tots.append(tot)
        y_ref[0, rows, g * hp * hd:(g + 1) * hp * hd] = jnp.concatenate(ys, axis=1).astype(y_ref.dtype)
        upd = lax.dot_general(bg, jnp.concatenate(xws, axis=1), (((0,), (0,)), ((), ())),
                              preferred_element_type=F32)
        h_ref[g] = hg * jnp.exp2(jnp.concatenate(tots, axis=1)) + upd


def _dt_spread():
    k = np.arange(2 * N_DT)[:, None] % N_DT
    lane_head = np.arange(2 * SSD_W)[None, :] // SSD_HEAD_DIM
    return jnp.asarray(k == lane_head, BF16)


def _ssd_kernel(xf_ref, dtaf_ref, dtatf_ref, xb_ref, dtab_ref, dtatb_ref, spread_ref, h0f_ref, h0b_ref,
                yf_ref, yb_ref, hf_out, hb_out, hf_ref, hb_ref, *, q):
    k = pl.program_id(1)

    @pl.when(k == 0)
    def _():
        hf_ref[...] = h0f_ref[0]
        hb_ref[...] = h0b_ref[0]

    cps = xf_ref.shape[1] // q
    for j in range(cps):
        _ssd_dir(xf_ref, dtaf_ref, dtatf_ref, spread_ref, hf_ref, yf_ref, q=q, rev=False, r0=j * q)
        _ssd_dir(xb_ref, dtab_ref, dtatb_ref, spread_ref, hb_ref, yb_ref, q=q, rev=True, r0=(cps - 1 - j) * q)

    @pl.when(k == pl.num_programs(1) - 1)
    def _():
        hf_out[0] = hf_ref[...]
        hb_out[0] = hb_ref[...]


def _ssd(xbc, dta, dtat, h0f, h0b, q, cps):
    bsz, length, _ = xbc.shape
    qs = q * cps
    nc = length // qs
    hshape = (SSD_GROUPS, SSD_STATE, SSD_W // SSD_GROUPS)
    fw = lambda b, k: (b, k, 0)
    bw = lambda b, k: (b, nc - 1 - k, 0)
    st = pl.BlockSpec((1,) + hshape, lambda b, k: (b, 0, 0, 0))
    return pl.pallas_call(
        functools.partial(_ssd_kernel, q=q),
        grid=(bsz, nc),
        in_specs=[
            pl.BlockSpec((1, qs, SSD_XBC), fw),
            pl.BlockSpec((1, qs, 2 * N_DT), fw),
            pl.BlockSpec((1, 2 * N_DT, qs), lambda b, k: (b, 0, k)),
            pl.BlockSpec((1, qs, SSD_XBC), bw),
            pl.BlockSpec((1, qs, 2 * N_DT), bw),
            pl.BlockSpec((1, 2 * N_DT, qs), lambda b, k: (b, 0, nc - 1 - k)),
            pl.BlockSpec((2 * N_DT, 2 * SSD_W), lambda b, k: (0, 0)),
            st, st,
        ],
        out_specs=[pl.BlockSpec((1, qs, SSD_W), fw), pl.BlockSpec((1, qs, SSD_W), bw), st, st],
        out_shape=[
            jax.ShapeDtypeStruct((bsz, length, SSD_W), BF16),
            jax.ShapeDtypeStruct((bsz, length, SSD_W), BF16),
            jax.ShapeDtypeStruct((bsz,) + hshape, F32),
            jax.ShapeDtypeStruct((bsz,) + hshape, F32),
        ],
        scratch_shapes=[pltpu.VMEM(hshape, F32), pltpu.VMEM(hshape, F32)],
        compiler_params=_cparams(("parallel", "arbitrary")),
        name="ssd",
    )(xbc, dta, dtat, xbc, dta, dtat, _dt_spread(), h0f, h0b)


def _outproj_kernel(x_ref, mod_ref, g_ref, hy_ref, yf_ref, yb_ref, xs_ref, z_ref, sc_ref, dexp_ref, mg_ref,
                    why_ref, wssd_ref, wsc_ref, wg_ref, wu_ref, wo_ref, o_ref, *, bounds):
    x = x_ref[0]
    mod = mod_ref[0]
    mg = mg_ref[...]
    o = _dot(sc_ref[0], wsc_ref[...])
    o = o + _dot(_rms(hy_ref[0], mg[:, :HY_W]).astype(BF16), why_ref[...])
    y = yf_ref[0].astype(F32) + yb_ref[0].astype(F32) + dexp_ref[...] * xs_ref[0].astype(F32)
    y_ssd = _rms(y * _silu(z_ref[0].astype(F32)), mg[:, HY_W:HY_W + SSD_W]).astype(BF16)
    o = o + _dot(y_ssd, wssd_ref[...])
    x = x + _rms(o, mod[5:6] * g_ref[3:4])
    o_ref[0] = _ffn_rows(x, mod, g_ref[...], wg_ref, wu_ref, wo_ref, 2, bounds)


def _outproj(x, mod, g, hy, yf, yb, xbc, z, sc, lw, ffn_w, tm):
    bsz, length, d = x.shape
    per_batch = mod.shape[0] != 1
    tok = lambda w: pl.BlockSpec((1, tm, w), lambda b, t: (b, t, 0))
    consts = (lw["d_exp"], lw["mix_gain"], lw["w_out_hy"], lw["w_out_ssd"], lw["w_out_sc"])
    w_in_all, w_out_all = ffn_w[:2]
    return pl.pallas_call(
        functools.partial(_outproj_kernel, bounds=_ffn_bounds(w_out_all.shape[2], FFN_CHUNK)),
        grid=(bsz, length // tm),
        in_specs=[
            tok(d),
            pl.BlockSpec((1, N_MOD, d), (lambda b, t: (b, 0, 0)) if per_batch else (lambda b, t: (0, 0, 0))),
            _const_spec(g.shape),
            tok(HY_W), tok(SSD_W), tok(SSD_W), tok(SSD_W), tok(SSD_W), tok(SC_W),
        ] + [_const_spec(a.shape) for a in consts] + _ffn_weight_specs(*ffn_w),
        out_specs=tok(d),
        out_shape=jax.ShapeDtypeStruct(x.shape, F32),
        compiler_params=_cparams(("parallel", "parallel")),
        name="outproj",
    )(x, mod, g, hy, yf, yb, xbc, z, sc, *consts, w_in_all, w_in_all, w_out_all)


def _mixer(xs, mod, g, lw, ffn_w, kspec, dft, seg, tm, p, q, h0f, h0b, batch_shape, want_out):
    bsz, length = batch_shape
    tm_in = _pick(xs.shape[1], (INPROJ_ROWS, tm))
    hy, z, xbc, sc, dta, dtat = _inproj(xs, mod, g, lw, seg, tm_in, q)
    if xs.shape[0] != bsz:
        dtat = dtat.reshape(2 * N_DT, bsz, length).transpose(1, 0, 2)
    seq = lambda a: a.reshape(bsz, length, a.shape[-1])
    yf, yb, hf, hb = _ssd(seq(xbc), seq(dta), dtat, h0f, h0b, q, _pick(length // q, (SSD_CPS, 4, 2, 1)))
    if not want_out:
        return None, hf, hb
    fwd, inv = dft
    nct = HY_W // HY_CT
    hy_s = seq(hy)
    fwd_b, inv_b = fwd.astype(BF16), inv.astype(BF16)
    z1 = _hyconv(hy_s, 0, hy_s, nct, kspec, 0, lw["hy_bias"][0:1], fwd_b, inv_b, p)
    z2 = _hyconv(z1, 0, hy_s, 2 * nct, kspec, nct, lw["hy_bias"][1:2], fwd_b, inv_b, p)
    tok = lambda a: a.reshape(xs.shape[0], xs.shape[1], a.shape[-1])
    out = _outproj(xs, mod, g, tok(z2), tok(yf), tok(yb), xbc, z, sc, lw, ffn_w, tm)
    return out, hf, hb


def _layer_weights(l, w_in, w_out, hy_conv_w, hy_conv_b, hy_bias, ssd_conv_w, ssd_conv_b, ssd_a_log, ssd_dt_bias,
                   ssd_d, sc_conv_w, mix_gain):
    o_dt = HY_IN + SSD_W + SSD_XBC
    wl = w_in[l]
    return {
        "w_in": jnp.concatenate([wl[:, :o_dt], wl[:, o_dt + N_DT:]], axis=1).astype(BF16),
        "w_dtt": wl[:, o_dt:o_dt + N_DT].astype(BF16).T,
        "hy_conv_w": hy_conv_w[l],
        "hy_conv_b": hy_conv_b[l][None],
        "ssd_conv_w": ssd_conv_w[l],
        "ssd_conv_b": ssd_conv_b[l][None],
        "sc_conv_w": sc_conv_w[l],
        "sc_gain": mix_gain[l][None, HY_W + SSD_W:],
        "dt_bias_t": ssd_dt_bias[l].reshape(N_DT, 1),
        "hy_bias": hy_bias[l],
        "a2_col": (-jnp.exp(ssd_a_log[l]) * math.log2(math.e)).reshape(N_DT, 1),
        "d_exp": jnp.repeat(ssd_d[l], SSD_HEAD_DIM)[None],
        "mix_gain": mix_gain[l][None],
        "w_out_hy": w_out[l, :HY_W].astype(BF16),
        "w_out_ssd": w_out[l, HY_W:HY_W + SSD_W].astype(BF16),
        "w_out_sc": w_out[l, HY_W + SSD_W:].astype(BF16),
    }


def _pick(n, prefs):
    for v in prefs:
        if n % v == 0:
            return v
    return n


def kernel(x, c, ctx, c_ctx, w_mod, b_mod, norm_g, ffn_w_in, ffn_w_out, w_in, w_out, hy_conv_w, hy_conv_b, hy_fw1,
           hy_fb1, hy_fw2, hy_fb2, hy_fw3, hy_fb3, hy_fw4, hy_freq, hy_bias, ssd_conv_w, ssd_conv_b, ssd_a_log,
           ssd_dt_bias, ssd_d, sc_conv_w, mix_gain):
    bsz, n_lat, d = x.shape
    _, ctx_len, _ = ctx.shape
    depth = w_mod.shape[0]
    ffn_in_b, ffn_out_b = ffn_w_in.astype(BF16), ffn_w_out.astype(BF16)

    rows = -(-(bsz + 1) // SUBLANE) * SUBLANE
    cvec = jnp.zeros((rows, d), F32).at[:bsz].set(c).at[bsz].set(c_ctx)
    mod_all = _mod_table(cvec, w_mod, b_mod).reshape(depth, rows, N_MOD, d)

    tm = _pick(n_lat, (ROW_TILE, 256, 128, GRID_W))
    p_lat = _pick(n_lat, (HY_BLOCK, 256, 128))
    q_lat = _pick(n_lat, (SSD_CHUNK, 128))
    xc = ctx.reshape(1, bsz * ctx_len, d)
    tm_c = ctx_len * _pick(bsz, (2, 1)) if ctx_len < ROW_TILE else _pick(ctx_len, (ROW_TILE, 256))
    p_ctx = _pick(ctx_len, (HY_BLOCK, 256, 128))
    q_ctx = _pick(ctx_len, (SSD_CHUNK, 128))
    dft_lat = _dft_mats(p_lat)
    dft_ctx = _dft_mats(p_ctx)

    h_zero = jnp.zeros((bsz, SSD_GROUPS, SSD_STATE, SSD_W // SSD_GROUPS), F32)
    for l in range(depth):
        last = l == depth - 1
        mod_x = mod_all[l, :bsz]
        mod_c = mod_all[l, bsz:bsz + 1]
        g = norm_g[l]
        lw = _layer_weights(l, w_in, w_out, hy_conv_w, hy_conv_b, hy_bias, ssd_conv_w, ssd_conv_b, ssd_a_log,
                            ssd_dt_bias, ssd_d, sc_conv_w, mix_gain)
        fw = {"fw1": hy_fw1[l], "fb1": hy_fb1[l], "fw2": hy_fw2[l], "fb2": hy_fb2[l], "fw3": hy_fw3[l],
              "fb3": hy_fb3[l], "fw4": hy_fw4[l], "freq": hy_freq[l]}
        ffn_w = [(ffn_in_b, ffn_out_b, l, i) for i in range(2)]

        x = _ffn(x, mod_x, g, ffn_w[0], 0, tm)
        xc = _ffn(xc, mod_c, g, ffn_w[0], 0, tm_c)

        ks_ctx = None
        if not last:
            ks_ctx = _hyena_spectra(_hyena_filter(fw, ctx_len, p_ctx), dft_ctx[0], p_ctx)
        xc, h_f, h_b = _mixer(xc, mod_c, g, lw, ffn_w[1], ks_ctx, dft_ctx, ctx_len, tm_c, p_ctx, q_ctx, h_zero,
                              h_zero, (bsz, ctx_len), not last)
        ks_lat = _hyena_spectra(_hyena_filter(fw, n_lat, p_lat), dft_lat[0], p_lat)
        x, _, _ = _mixer(x, mod_x, g, lw, ffn_w[1], ks_lat, dft_lat, GRID_W, tm, p_lat, q_lat, h_f, h_b,
                         (bsz, n_lat), True)
    return x
```

```python
import functools
import math

import jax
import jax.numpy as jnp
import numpy as np
from jax import lax
from jax.experimental import pallas as pl
from jax.experimental.pallas import tpu as pltpu

GRID_W = 64
EPS = 1e-6
FFN_RES_W = 0.5
N_MOD = 9
HY_W = 256
SSD_HEADS = 8
SSD_HEAD_DIM = 64
SSD_GROUPS = 2
SSD_STATE = 128
SC_W = 256
HY_ORDER = 2
HY_EMB = 33
HY_FAST = 0.3
HY_SLOW = 1.5
HY_TARGET = 1e-2

SSD_W = SSD_HEADS * SSD_HEAD_DIM
HY_IN = 3 * HY_W
SSD_XBC = SSD_W + 2 * SSD_GROUPS * SSD_STATE
SC_IN = 3 * SC_W
N_DT = 2 * SSD_HEADS

LANE = 128
SUBLANE = 8
MXU_TILE = 256
FFN_CHUNK = 3 * MXU_TILE
INPROJ_CHUNK = 2 * MXU_TILE
SSD_CPS = 8
ROW_TILE = 512
INPROJ_ROWS = 1024
HY_BLOCK = 512
SSD_CHUNK = 256
VMEM_LIMIT = 56 * 1024 * 1024

F32 = jnp.float32
BF16 = jnp.bfloat16


def _cparams(sem):
    return pltpu.CompilerParams(dimension_semantics=sem, vmem_limit_bytes=VMEM_LIMIT)


def _const_spec(shape):
    nd = len(shape)
    return pl.BlockSpec(shape, lambda *_: (0,) * nd, pipeline_mode=pl.Buffered(1))


def _dot(a, b):
    return jnp.dot(a, b, preferred_element_type=F32)


def _split2(a):
    hi = a.astype(BF16)
    lo = (a - hi.astype(F32)).astype(BF16)
    return hi, lo


def _dot_hp(a, b):
    ah, al = _split2(a)
    bh, bl = _split2(b)
    return _dot(ah, bh) + (_dot(ah, bl) + _dot(al, bh))


def _dot_hp_t(a_t, b):
    dn = (((0,), (0,)), ((), ()))
    ah, al = _split2(a_t)
    bh, bl = _split2(b)
    dg = lambda u, v: lax.dot_general(u, v, dn, preferred_element_type=F32)
    return dg(ah, bh) + (dg(ah, bl) + dg(al, bh))


def _silu(x):
    hx = 0.5 * x
    return hx + hx * jnp.tanh(hx)


def _rms(x, g):
    ms = jnp.mean(x * x, axis=-1, keepdims=True)
    return x * (lax.rsqrt(ms + EPS) * g)


def _mod_kernel(c_ref, w_ref, b_ref, o_ref):
    o_ref[0] = _dot_hp(_silu(c_ref[...]), w_ref[0]) + b_ref[0]


def _mod_table(cvec, w_mod, b_mod):
    depth, d, nm = w_mod.shape
    r = cvec.shape[0]
    tn = 1024 if nm % 1024 == 0 else d
    return pl.pallas_call(
        _mod_kernel,
        grid=(depth, nm // tn),
        in_specs=[
            pl.BlockSpec((r, d), lambda l, n: (0, 0)),
            pl.BlockSpec((1, d, tn), lambda l, n: (l, 0, n)),
            pl.BlockSpec((1, 1, tn), lambda l, n: (l, 0, n)),
        ],
        out_specs=pl.BlockSpec((1, r, tn), lambda l, n: (l, 0, n)),
        out_shape=jax.ShapeDtypeStruct((depth, r, nm), F32),
        compiler_params=_cparams(("parallel", "parallel")),
        name="mod_table",
    )(cvec, w_mod, b_mod.reshape(depth, 1, nm))


def _ffn_rows(x, mod, g, wg_ref, wu_ref, wo_ref, sub, bounds):
    shift, scale, gate = mod[3 * sub:3 * sub + 1], mod[3 * sub + 1:3 * sub + 2], mod[3 * sub + 2:3 * sub + 3]
    g_pre, g_post = g[2 * sub:2 * sub + 1], g[2 * sub + 1:2 * sub + 2]
    h = (_rms(x, g_pre * (1.0 + scale)) + shift).astype(BF16)
    acc = None
    for lo, hi in zip(bounds[:-1], bounds[1:]):
        gg = _dot(h, wg_ref[:, lo:hi])
        uu = _dot(h, wu_ref[:, lo:hi])
        a = (_silu(gg) * uu).astype(BF16)
        part = _dot(a, wo_ref[lo:hi, :])
        acc = part if acc is None else acc + part
    return x + _rms(acc, FFN_RES_W * gate * g_post)


def _ffn_kernel(x_ref, mod_ref, g_ref, wg_ref, wu_ref, wo_ref, o_ref, *, sub, bounds):
    o_ref[0] = _ffn_rows(x_ref[0], mod_ref[0], g_ref[...], wg_ref, wu_ref, wo_ref, sub, bounds)


def _ffn_bounds(dff, width):
    return tuple(range(0, dff, width)) + (dff,)


def _ffn_weight_specs(w_in_all, w_out_all, l, i):
    d, dff = w_out_all.shape[3], w_out_all.shape[2]
    one = pl.Buffered(1)
    return [
        pl.BlockSpec((None, None, d, dff), lambda *_: (l, i, 0, 0), pipeline_mode=one),
        pl.BlockSpec((None, None, d, dff), lambda *_: (l, i, 0, 1), pipeline_mode=one),
        pl.BlockSpec((None, None, dff, d), lambda *_: (l, i, 0, 0), pipeline_mode=one),
    ]


def _ffn(x, mod, g, ffn_w, sub, tm):
    w_in_all, w_out_all, l, i = ffn_w
    bsz, length, d = x.shape
    per_batch = mod.shape[0] != 1
    return pl.pallas_call(
        functools.partial(_ffn_kernel, sub=sub, bounds=_ffn_bounds(w_out_all.shape[2], FFN_CHUNK)),
        grid=(bsz, length // tm),
        in_specs=[
            pl.BlockSpec((1, tm, d), lambda b, t: (b, t, 0)),
            pl.BlockSpec((1, N_MOD, d), (lambda b, t: (b, 0, 0)) if per_batch else (lambda b, t: (0, 0, 0))),
            _const_spec(g.shape),
        ] + _ffn_weight_specs(*ffn_w),
        out_specs=pl.BlockSpec((1, tm, d), lambda b, t: (b, t, 0)),
        out_shape=jax.ShapeDtypeStruct(x.shape, F32),
        compiler_params=_cparams(("parallel", "parallel")),
        name="ffn",
    )(x, mod, g, w_in_all, w_in_all, w_out_all)


def _conv3(a, w, seg, scr):
    n, c = a.shape
    scr[SUBLANE:SUBLANE + n, :c] = a
    row = lax.broadcasted_iota(jnp.int32, (n, c), 0) % seg
    prev = jnp.where(row == 0, 0.0, scr[SUBLANE - 1:SUBLANE - 1 + n, :c])
    nxt = jnp.where(row == seg - 1, 0.0, scr[SUBLANE + 1:SUBLANE + 1 + n, :c])
    return (prev * w[0:1] + a * w[1:2]) + nxt * w[2:3]


def _chunk_scan(da, q, axis):
    n = da.shape[axis]
    pos = lax.broadcasted_iota(jnp.int32, da.shape, axis) % q
    head = lax.broadcasted_iota(jnp.int32, da.shape, 1 - axis)
    up, down = da, da
    s = 1
    while s < q:
        up = up + jnp.where(pos >= s, pltpu.roll(up, s, axis), 0.0)
        down = down + jnp.where(pos < q - s, pltpu.roll(down, n - s, axis), 0.0)
        s *= 2
    return jnp.where(head < SSD_HEADS, up, down)


def _inproj_kernel(x_ref, mod_ref, g_ref, w_ref, wdtt_ref, hyw_ref, hyb_ref, sw_ref, sb_ref, scw_ref,
                   scg_ref, dtbt_ref, a2c_ref, hy_ref, z_ref, xbc_ref, sc_ref, dta_ref, dtat_ref, scr_ref, *, seg, q):
    x = x_ref[0]
    mod = mod_ref[0]
    pad = jnp.zeros((SUBLANE, scr_ref.shape[1]), F32)
    scr_ref[:SUBLANE, :] = pad
    scr_ref[scr_ref.shape[0] - SUBLANE:, :] = pad
    h = (_rms(x, g_ref[2:3] * (1.0 + mod[4:5])) + mod[3:4]).astype(BF16)
    dtt = lax.dot_general(wdtt_ref[...], h, (((1,), (1,)), ((), ())), preferred_element_type=F32)
    dtt = jax.nn.softplus(dtt + dtbt_ref[...])
    rowform = jnp.concatenate([dtt, _chunk_scan(dtt * a2c_ref[...], q, 1)], axis=0)
    dtat_ref[0] = rowform
    pad = jnp.zeros((LANE - 2 * N_DT, rowform.shape[1]), F32)
    dta_ref[0] = jnp.concatenate([rowform, pad], axis=0).T[:, :2 * N_DT]
    o_z, o_xbc, o_sc = HY_IN, HY_IN + SSD_W, HY_IN + SSD_W + SSD_XBC

    def post_hy(p, lo, hi):
        hy_ref[0, :, lo:hi] = _conv3(p, hyw_ref[:, lo:hi], seg, scr_ref) + hyb_ref[:, lo:hi]

    def post_z(p, lo, hi):
        z_ref[0, :, lo - o_z:hi - o_z] = p.astype(BF16)

    def post_xbc(p, lo, hi):
        lo, hi = lo - o_xbc, hi - o_xbc
        xbc_ref[0, :, lo:hi] = _silu(_conv3(p, sw_ref[:, lo:hi], seg, scr_ref) + sb_ref[:, lo:hi]).astype(BF16)

    def post_sc(p, lo, hi):
        gb, gc, hx = p[:, :SC_W], p[:, SC_W:2 * SC_W], p[:, 2 * SC_W:]
        sc_ref[0] = _rms(gb * _conv3(gc * hx, scw_ref[...], seg, scr_ref), scg_ref[...]).astype(BF16)

    stages = []
    for post, lo, hi in ((post_hy, 0, o_z), (post_z, o_z, o_xbc), (post_xbc, o_xbc, o_sc)):
        stages += [(post, c, min(c + INPROJ_CHUNK, hi)) for c in range(lo, hi, INPROJ_CHUNK)]
    stages.append((post_sc, o_sc, o_sc + SC_IN))
    pending = None
    for post, lo, hi in stages:
        p = _dot(h, w_ref[:, lo:hi])
        if pending is not None:
            pending[0](*pending[1:])
        pending = (post, p, lo, hi)
    pending[0](*pending[1:])


def _inproj(x, mod, g, lw, seg, tm, q):
    bsz, length, d = x.shape
    per_batch = mod.shape[0] != 1
    tok = lambda w: pl.BlockSpec((1, tm, w), lambda b, t: (b, t, 0))
    consts = (lw["w_in"], lw["w_dtt"], lw["hy_conv_w"], lw["hy_conv_b"], lw["ssd_conv_w"],
              lw["ssd_conv_b"], lw["sc_conv_w"], lw["sc_gain"], lw["dt_bias_t"], lw["a2_col"])
    return pl.pallas_call(
        functools.partial(_inproj_kernel, seg=seg, q=q),
        grid=(bsz, length // tm),
        in_specs=[
            tok(d),
            pl.BlockSpec((1, N_MOD, d), (lambda b, t: (b, 0, 0)) if per_batch else (lambda b, t: (0, 0, 0))),
            _const_spec(g.shape),
        ] + [_const_spec(a.shape) for a in consts],
        out_specs=[tok(HY_IN), tok(SSD_W), tok(SSD_XBC), tok(SC_W), tok(2 * N_DT),
                   pl.BlockSpec((1, 2 * N_DT, tm), lambda b, t: (b, 0, t))],
        out_shape=[
            jax.ShapeDtypeStruct((bsz, length, HY_IN), F32),
            jax.ShapeDtypeStruct((bsz, length, SSD_W), BF16),
            jax.ShapeDtypeStruct((bsz, length, SSD_XBC), BF16),
            jax.ShapeDtypeStruct((bsz, length, SC_W), BF16),
            jax.ShapeDtypeStruct((bsz, length, 2 * N_DT), F32),
            jax.ShapeDtypeStruct((bsz, 2 * N_DT, length), F32),
        ],
        scratch_shapes=[pltpu.VMEM((tm + 2 * SUBLANE, INPROJ_CHUNK), F32)],
        compiler_params=_cparams(("parallel", "parallel")),
        name="inproj",
    )(x, mod, g, *consts)


def _dft_mats(p):
    f = np.arange(p, dtype=np.int64)[:, None]
    t = np.arange(p, dtype=np.int64)[None, :]
    ang = (np.pi / (2 * p)) * (((2 * f + 1) * t) % (4 * p)).astype(np.float64)
    fwd = np.concatenate([np.cos(ang), -np.sin(ang)], axis=0)
    return jnp.asarray(fwd, F32), jnp.asarray(fwd.T / p, F32)


def _filter_kernel(fb_ref, dl_ref, w1t_ref, w1c_ref, w1s_ref, b1_ref, w2_ref, b2_ref, w3_ref, b3_ref, w4_ref,
                   fr_ref, k_ref, *, length, tr):
    base = pl.program_id(0) * tr - length
    n_row = jnp.abs(base + lax.broadcasted_iota(jnp.int32, (1, tr), 1)).astype(F32)
    ang = fb_ref[...] * (n_row * (2.0 * math.pi / length))
    fr = fr_ref[...]
    h = w1t_ref[...] * (n_row * (1.0 / (length - 1)))
    h = h + _dot_hp(w1c_ref[...], jnp.cos(ang)) - _dot_hp(w1s_ref[...], jnp.sin(ang))
    h = jnp.sin(fr * (h + b1_ref[...]))
    h = jnp.sin(fr * (_dot_hp(w2_ref[...], h) + b2_ref[...]))
    h = jnp.sin(fr * (_dot_hp(w3_ref[...], h) + b3_ref[...]))
    lag = base + lax.broadcasted_iota(jnp.int32, (tr, 1), 0)
    t_col = jnp.abs(lag).astype(F32) * (1.0 / (length - 1))
    decay = jnp.exp(-t_col * jnp.abs(dl_ref[...]))
    for o in range(HY_ORDER):
        lo = o * 2 * HY_W
        kf = _dot_hp_t(h, w4_ref[:, lo:lo + HY_W]) * decay
        kb = _dot_hp_t(h, w4_ref[:, lo + HY_W:lo + 2 * HY_W]) * decay
        k_ref[:, o * HY_W:(o + 1) * HY_W] = jnp.where(lag >= 0, kf, jnp.where(lag == -length, 0.0, kb))


def _hyena_filter(fw, length, tr):
    bands = (HY_EMB - 1) // 2
    fbands = jnp.linspace(1e-4, bands - 1, bands, dtype=F32)[:, None]
    deltas = jnp.linspace(math.log(HY_TARGET) / HY_SLOW, math.log(HY_TARGET) / HY_FAST, HY_W, dtype=F32)[None, :]
    fw1 = fw["fw1"]
    col = lambda v: v[:, None]
    consts = (fbands, deltas, fw1[0][:, None], fw1[1:1 + bands].T, fw1[1 + bands:].T, col(fw["fb1"]), fw["fw2"].T,
              col(fw["fb2"]), fw["fw3"].T, col(fw["fb3"]), fw["fw4"], col(fw["freq"]))
    return pl.pallas_call(
        functools.partial(_filter_kernel, length=length, tr=tr),
        grid=(2 * length // tr,),
        in_specs=[pl.BlockSpec(a.shape, lambda i: (0, 0)) for a in consts],
        out_specs=pl.BlockSpec((tr, HY_ORDER * HY_W), lambda i: (i, 0)),
        out_shape=jax.ShapeDtypeStruct((2 * length, HY_ORDER * HY_W), F32),
        compiler_params=_cparams(("parallel",)),
        name="hyena_filter",
    )(*consts)


def _kspec_kernel(f_ref, k0_ref, k1_ref, o_ref, prev_ref, *, p):
    fwd = f_ref[...]

    @pl.when(pl.program_id(0) == 0)
    def _():
        prev_ref[...] = _dot_hp(fwd, k0_ref[...])

    a = _dot_hp(fwd, k1_ref[...])
    a1 = prev_ref[...]
    prev_ref[...] = a
    a1_re = a1[:p] - k0_ref[0:1, :]
    a1_im = a1[p:]
    sign = 1.0 - 2.0 * (lax.broadcasted_iota(jnp.int32, (p, 1), 0) % 2).astype(F32)
    o_ref[0, :p, :] = a[:p] - sign * a1_im
    o_ref[0, p:, :] = a[p:] + sign * a1_re


def _hyena_spectra(kfull, fwd, p):
    nb = kfull.shape[0] // (2 * p)
    cols = kfull.shape[1]
    return pl.pallas_call(
        functools.partial(_kspec_kernel, p=p),
        grid=(2 * nb - 1,),
        in_specs=[
            pl.BlockSpec(fwd.shape, lambda i: (0, 0)),
            pl.BlockSpec((p, cols), lambda i: (i, 0)),
            pl.BlockSpec((p, cols), lambda i: (i + 1, 0)),
        ],
        out_specs=pl.BlockSpec((1, 2 * p, cols), lambda i: (i, 0, 0)),
        out_shape=jax.ShapeDtypeStruct((2 * nb - 1, 2 * p, cols), F32),
        scratch_shapes=[pltpu.VMEM((2 * p, cols), F32)],
        compiler_params=_cparams(("arbitrary",)),
        name="hyena_spectra",
    )(fwd, kfull, kfull)


HY_CT = 128
HY_NB = 2
HY_RT = 32


def _hyconv_kernel(zin_ref, gate_ref, k_ref, bias_ref, fwd_ref, inv_ref, o_ref, zf_ref, ya_ref, yb_ref, *, nb, p):
    fwd = fwd_ref[...]
    for j in range(nb):
        zb = jnp.concatenate([zin_ref[s, j * p:(j + 1) * p, :] for s in range(HY_NB)], axis=1).astype(BF16)
        zf_ref[j] = _dot(fwd, zb)

    def mac(i, y_ref):
        for rt in range(p // HY_RT):
            r0 = rt * HY_RT
            acc_re = [None] * HY_NB
            acc_im = [None] * HY_NB
            for j in range(nb):
                d = i - j + (nb - 1)
                kre = k_ref[d, r0:r0 + HY_RT, :]
                kim = k_ref[d, p + r0:p + r0 + HY_RT, :]
                zre = zf_ref[j, r0:r0 + HY_RT, :]
                zim = zf_ref[j, p + r0:p + r0 + HY_RT, :]
                for s in range(HY_NB):
                    zr = zre[:, s * HY_CT:(s + 1) * HY_CT]
                    zi = zim[:, s * HY_CT:(s + 1) * HY_CT]
                    re = kre * zr - kim * zi
                    im = kre * zi + kim * zr
                    acc_re[s] = re if j == 0 else acc_re[s] + re
                    acc_im[s] = im if j == 0 else acc_im[s] + im
            y_ref[r0:r0 + HY_RT, :] = jnp.concatenate(acc_re, axis=1).astype(BF16)
            y_ref[p + r0:p + r0 + HY_RT, :] = jnp.concatenate(acc_im, axis=1).astype(BF16)

    def finish(i, y_ref):
        y = _dot(inv_ref[...], y_ref[...])
        rows = pl.ds(pl.multiple_of(i * p, p), p)
        for s in range(HY_NB):
            zs = zin_ref[s, rows, :]
            o_ref[s, rows, :] = gate_ref[s, rows, :] * (y[:, s * HY_CT:(s + 1) * HY_CT] + bias_ref[...] * zs)

    mac(0, ya_ref)
    for i in range(1, nb):
        mac(i, yb_ref if i % 2 else ya_ref)
        finish(i - 1, ya_ref if i % 2 else yb_ref)
    finish(nb - 1, ya_ref if nb % 2 else yb_ref)


def _hyconv(zin, zin_col, gate, gate_col, kspec, k_col, bias, fwd, inv, p):
    bsz, length, _ = zin.shape
    nb = length // p
    nct = HY_W // HY_CT
    nd = kspec.shape[0]
    return pl.pallas_call(
        functools.partial(_hyconv_kernel, nb=nb, p=p),
        grid=(nct, bsz // HY_NB),
        in_specs=[
            pl.BlockSpec((HY_NB, length, HY_CT), lambda c, b: (b, 0, zin_col + c)),
            pl.BlockSpec((HY_NB, length, HY_CT), lambda c, b: (b, 0, gate_col + c)),
            pl.BlockSpec((nd, 2 * p, HY_CT), lambda c, b: (0, 0, k_col + c), pipeline_mode=pl.Buffered(1)),
            pl.BlockSpec((1, HY_CT), lambda c, b: (0, c)),
            _const_spec(fwd.shape),
            _const_spec(inv.shape),
        ],
        out_specs=pl.BlockSpec((HY_NB, length, HY_CT), lambda c, b: (b, 0, c)),
        out_shape=jax.ShapeDtypeStruct((bsz, length, HY_W), F32),
        scratch_shapes=[
            pltpu.VMEM((nb, 2 * p, HY_NB * HY_CT), F32),
            pltpu.VMEM((2 * p, HY_NB * HY_CT), BF16),
            pltpu.VMEM((2 * p, HY_NB * HY_CT), BF16),
        ],
        compiler_params=_cparams(("arbitrary", "arbitrary")),
        name="hyconv",
    )(zin, gate, kspec, bias, fwd, inv)


def _ssd_dir(xbc_ref, dta_ref, dtat_ref, spread_ref, h_ref, y_ref, *, q, rev, r0):
    rows = slice(r0, r0 + q)
    n, hp, hd = SSD_STATE, SSD_HEADS // SSD_GROUPS, SSD_HEAD_DIM
    c0 = SSD_HEADS if rev else 0
    end = 0 if rev else q - 1
    li = lax.broadcasted_iota(jnp.int32, (q, q), 0)
    si = lax.broadcasted_iota(jnp.int32, (q, q), 1)
    keep = (li <= si) if rev else (li >= si)
    low = lax.broadcasted_iota(jnp.int32, (q, LANE), 1) < hd
    pairs = hp * hd // LANE
    dt_hl = jnp.concatenate(_split2(dta_ref[0, rows, :N_DT]), axis=1)
    dt_lanes = _dot(dt_hl, spread_ref[:, c0 * hd:(c0 + SSD_HEADS) * hd])
    for g in range(SSD_GROUPS):
        bg = xbc_ref[0, rows, SSD_W + g * n:SSD_W + (g + 1) * n]
        cg = xbc_ref[0, rows, SSD_W + SSD_GROUPS * n + g * n:SSD_W + SSD_GROUPS * n + (g + 1) * n]
        scores = lax.dot_general(cg, bg, (((1,), (1,)), ((), ())), preferred_element_type=F32)
        hg = h_ref[g]
        y_off = _dot(cg, hg.astype(BF16))
        ys, xws, tots = [], [], []
        for t in range(pairs):
            cols = slice(g * hp * hd + t * LANE, g * hp * hd + (t + 1) * LANE)
            acs_b = []
            for r in (2 * t, 2 * t + 1):
                c = c0 + g * hp + r
                acs_b.append(jnp.broadcast_to(dta_ref[0, rows, N_DT + c:N_DT + c + 1], (q, LANE)))
            acs2 = jnp.where(low, acs_b[0], acs_b[1])
            xdt = xbc_ref[0, rows, cols].astype(F32) * dt_lanes[:, cols]
            yd = None
            for i, r in enumerate((2 * t, 2 * t + 1)):
                c = c0 + g * hp + r
                seg = jnp.concatenate([acs_b[i]] * (q // LANE), axis=1) - dtat_ref[0, N_DT + c:N_DT + c + 1, rows]
                m = (jnp.exp2(jnp.where(keep, seg, -1e30)) * scores).astype(BF16)
                xh = jnp.where(low if i == 0 else jnp.logical_not(low), xdt, 0.0).astype(BF16)
                part = _dot(m, xh)
                yd = part if yd is None else yd + part
            ys.append(yd + jnp.exp2(acs2) * y_off[:, t * LANE:(t + 1) * LANE])
            tot = acs2[end:end + 1, :]
            xws.append((xdt * jnp.exp2(tot - acs2)).astype(BF16))
            tots.append(tot)
        y_ref[0, rows, g * hp * hd:(g + 1) * hp * hd] = jnp.concatenate(ys, axis=1).astype(y_ref.dtype)
        upd = lax.dot_general(bg, jnp.concatenate(xws, axis=1), (((0,), (0,)), ((), ())),
                              preferred_element_type=F32)
        h_ref[g] = hg * jnp.exp2(jnp.concatenate(tots, axis=1)) + upd


def _dt_spread():
    k = np.arange(2 * N_DT)[:, None] % N_DT
    lane_head = np.arange(2 * SSD_W)[None, :] // SSD_HEAD_DIM
    return jnp.asarray(k == lane_head, BF16)


def _ssd_kernel(xf_ref, dtaf_ref, dtatf_ref, xb_ref, dtab_ref, dtatb_ref, spread_ref, h0f_ref, h0b_ref,
                yf_ref, yb_ref, hf_out, hb_out, hf_ref, hb_ref, *, q):
    k = pl.program_id(1)

    @pl.when(k == 0)
    def _():
        hf_ref[...] = h0f_ref[0]
        hb_ref[...] = h0b_ref[0]

    cps = xf_ref.shape[1] // q
    for j in range(cps):
        _ssd_dir(xf_ref, dtaf_ref, dtatf_ref, spread_ref, hf_ref, yf_ref, q=q, rev=False, r0=j * q)
        _ssd_dir(xb_ref, dtab_ref, dtatb_ref, spread_ref, hb_ref, yb_ref, q=q, rev=True, r0=(cps - 1 - j) * q)

    @pl.when(k == pl.num_programs(1) - 1)
    def _():
        hf_out[0] = hf_ref[...]
        hb_out[0] = hb_ref[...]


def _ssd(xbc, dta, dtat, h0f, h0b, q, cps):
    bsz, length, _ = xbc.shape
    qs = q * cps
    nc = length // qs
    hshape = (SSD_GROUPS, SSD_STATE, SSD_W // SSD_GROUPS)
    fw = lambda b, k: (b, k, 0)
    bw = lambda b, k: (b, nc - 1 - k, 0)
    st = pl.BlockSpec((1,) + hshape, lambda b, k: (b, 0, 0, 0))
    return pl.pallas_call(
        functools.partial(_ssd_kernel, q=q),
        grid=(bsz, nc),
        in_specs=[
            pl.BlockSpec((1, qs, SSD_XBC), fw),
            pl.BlockSpec((1, qs, 2 * N_DT), fw),
            pl.BlockSpec((1, 2 * N_DT, qs), lambda b, k: (b, 0, k)),
            pl.BlockSpec((1, qs, SSD_XBC), bw),
            pl.BlockSpec((1, qs, 2 * N_DT), bw),
            pl.BlockSpec((1, 2 * N_DT, qs), lambda b, k: (b, 0, nc - 1 - k)),
            pl.BlockSpec((2 * N_DT, 2 * SSD_W), lambda b, k: (0, 0)),
            st, st,
        ],
        out_specs=[pl.BlockSpec((1, qs, SSD_W), fw), pl.BlockSpec((1, qs, SSD_W), bw), st, st],
        out_shape=[
            jax.ShapeDtypeStruct((bsz, length, SSD_W), F32),
            jax.ShapeDtypeStruct((bsz, length, SSD_W), F32),
            jax.ShapeDtypeStruct((bsz,) + hshape, F32),
            jax.ShapeDtypeStruct((bsz,) + hshape, F32),
        ],
        scratch_shapes=[pltpu.VMEM(hshape, F32), pltpu.VMEM(hshape, F32)],
        compiler_params=_cparams(("parallel", "arbitrary")),
        name="ssd",
    )(xbc, dta, dtat, xbc, dta, dtat, _dt_spread(), h0f, h0b)


def _outproj_kernel(x_ref, mod_ref, g_ref, hy_ref, yf_ref, yb_ref, xs_ref, z_ref, sc_ref, dexp_ref, mg_ref,
                    why_ref, wssd_ref, wsc_ref, wg_ref, wu_ref, wo_ref, o_ref, *, bounds):
    x = x_ref[0]
    mod = mod_ref[0]
    mg = mg_ref[...]
    o = _dot(sc_ref[0], wsc_ref[...])
    o = o + _dot(_rms(hy_ref[0], mg[:, :HY_W]).astype(BF16), why_ref[...])
    y = yf_ref[0].astype(F32) + yb_ref[0].astype(F32) + dexp_ref[...] * xs_ref[0].astype(F32)
    y_ssd = _rms(y * _silu(z_ref[0].astype(F32)), mg[:, HY_W:HY_W + SSD_W]).astype(BF16)
    o = o + _dot(y_ssd, wssd_ref[...])
    x = x + _rms(o, mod[5:6] * g_ref[3:4])
    o_ref[0] = _ffn_rows(x, mod, g_ref[...], wg_ref, wu_ref, wo_ref, 2, bounds)


def _outproj(x, mod, g, hy, yf, yb, xbc, z, sc, lw, ffn_w, tm):
    bsz, length, d = x.shape
    per_batch = mod.shape[0] != 1
    tok = lambda w: pl.BlockSpec((1, tm, w), lambda b, t: (b, t, 0))
    consts = (lw["d_exp"], lw["mix_gain"], lw["w_out_hy"], lw["w_out_ssd"], lw["w_out_sc"])
    w_in_all, w_out_all = ffn_w[:2]
    return pl.pallas_call(
        functools.partial(_outproj_kernel, bounds=_ffn_bounds(w_out_all.shape[2], FFN_CHUNK)),
        grid=(bsz, length // tm),
        in_specs=[
            tok(d),
            pl.BlockSpec((1, N_MOD, d), (lambda b, t: (b, 0, 0)) if per_batch else (lambda b, t: (0, 0, 0))),
            _const_spec(g.shape),
            tok(HY_W), tok(SSD_W), tok(SSD_W), tok(SSD_W), tok(SSD_W), tok(SC_W),
        ] + [_const_spec(a.shape) for a in consts] + _ffn_weight_specs(*ffn_w),
        out_specs=tok(d),
        out_shape=jax.ShapeDtypeStruct(x.shape, F32),
        compiler_params=_cparams(("parallel", "parallel")),
        name="outproj",
    )(x, mod, g, hy, yf, yb, xbc, z, sc, *consts, w_in_all, w_in_all, w_out_all)


def _mixer(xs, mod, g, lw, ffn_w, kspec, dft, seg, tm, p, q, h0f, h0b, batch_shape, want_out):
    bsz, length = batch_shape
    tm_in = _pick(xs.shape[1], (INPROJ_ROWS, tm))
    hy, z, xbc, sc, dta, dtat = _inproj(xs, mod, g, lw, seg, tm_in, q)
    if xs.shape[0] != bsz:
        dtat = dtat.reshape(2 * N_DT, bsz, length).transpose(1, 0, 2)
    seq = lambda a: a.reshape(bsz, length, a.shape[-1])
    yf, yb, hf, hb = _ssd(seq(xbc), seq(dta), dtat, h0f, h0b, q, _pick(length // q, (SSD_CPS, 4, 2, 1)))
    if not want_out:
        return None, hf, hb
    fwd, inv = dft
    nct = HY_W // HY_CT
    hy_s = seq(hy)
    fwd_b, inv_b = fwd.astype(BF16), inv.astype(BF16)
    z1 = _hyconv(hy_s, 0, hy_s, nct, kspec, 0, lw["hy_bias"][0:1], fwd_b, inv_b, p)
    z2 = _hyconv(z1, 0, hy_s, 2 * nct, kspec, nct, lw["hy_bias"][1:2], fwd_b, inv_b, p)
    tok = lambda a: a.reshape(xs.shape[0], xs.shape[1], a.shape[-1])
    out = _outproj(xs, mod, g, tok(z2), tok(yf), tok(yb), xbc, z, sc, lw, ffn_w, tm)
    return out, hf, hb


def _layer_weights(l, w_in, w_out, hy_conv_w, hy_conv_b, hy_bias, ssd_conv_w, ssd_conv_b, ssd_a_log, ssd_dt_bias,
                   ssd_d, sc_conv_w, mix_gain):
    o_dt = HY_IN + SSD_W + SSD_XBC
    wl = w_in[l]
    return {
        "w_in": jnp.concatenate([wl[:, :o_dt], wl[:, o_dt + N_DT:]], axis=1).astype(BF16),
        "w_dtt": wl[:, o_dt:o_dt + N_DT].astype(BF16).T,
        "hy_conv_w": hy_conv_w[l],
        "hy_conv_b": hy_conv_b[l][None],
        "ssd_conv_w": ssd_conv_w[l],
        "ssd_conv_b": ssd_conv_b[l][None],
        "sc_conv_w": sc_conv_w[l],
        "sc_gain": mix_gain[l][None, HY_W + SSD_W:],
        "dt_bias_t": ssd_dt_bias[l].reshape(N_DT, 1),
        "hy_bias": hy_bias[l],
        "a2_col": (-jnp.exp(ssd_a_log[l]) * math.log2(math.e)).reshape(N_DT, 1),
        "d_exp": jnp.repeat(ssd_d[l], SSD_HEAD_DIM)[None],
        "mix_gain": mix_gain[l][None],
        "w_out_hy": w_out[l, :HY_W].astype(BF16),
        "w_out_ssd": w_out[l, HY_W:HY_W + SSD_W].astype(BF16),
        "w_out_sc": w_out[l, HY_W + SSD_W:].astype(BF16),
    }


def _pick(n, prefs):
    for v in prefs:
        if n % v == 0:
            return v
    return n


def kernel(x, c, ctx, c_ctx, w_mod, b_mod, norm_g, ffn_w_in, ffn_w_out, w_in, w_out, hy_conv_w, hy_conv_b, hy_fw1,
           hy_fb1, hy_fw2, hy_fb2, hy_fw3, hy_fb3, hy_fw4, hy_freq, hy_bias, ssd_conv_w, ssd_conv_b, ssd_a_log,
           ssd_dt_bias, ssd_d, sc_conv_w, mix_gain):
    bsz, n_lat, d = x.shape
    _, ctx_len, _ = ctx.shape
    depth = w_mod.shape[0]
    ffn_in_b, ffn_out_b = ffn_w_in.astype(BF16), ffn_w_out.astype(BF16)

    rows = -(-(bsz + 1) // SUBLANE) * SUBLANE
    cvec = jnp.zeros((rows, d), F32).at[:bsz].set(c).at[bsz].set(c_ctx)
    mod_all = _mod_table(cvec, w_mod, b_mod).reshape(depth, rows, N_MOD, d)

    tm = _pick(n_lat, (ROW_TILE, 256, 128, GRID_W))
    p_lat = _pick(n_lat, (HY_BLOCK, 256, 128))
    q_lat = _pick(n_lat, (SSD_CHUNK, 128))
    xc = ctx.reshape(1, bsz * ctx_len, d)
    tm_c = ctx_len * _pick(bsz, (2, 1)) if ctx_len < ROW_TILE else _pick(ctx_len, (ROW_TILE, 256))
    p_ctx = _pick(ctx_len, (HY_BLOCK, 256, 128))
    q_ctx = _pick(ctx_len, (SSD_CHUNK, 128))
    dft_lat = _dft_mats(p_lat)
    dft_ctx = _dft_mats(p_ctx)

    h_zero = jnp.zeros((bsz, SSD_GROUPS, SSD_STATE, SSD_W // SSD_GROUPS), F32)
    for l in range(depth):
        last = l == depth - 1
        mod_x = mod_all[l, :bsz]
        mod_c = mod_all[l, bsz:bsz + 1]
        g = norm_g[l]
        lw = _layer_weights(l, w_in, w_out, hy_conv_w, hy_conv_b, hy_bias, ssd_conv_w, ssd_conv_b, ssd_a_log,
                            ssd_dt_bias, ssd_d, sc_conv_w, mix_gain)
        fw = {"fw1": hy_fw1[l], "fb1": hy_fb1[l], "fw2": hy_fw2[l], "fb2": hy_fb2[l], "fw3": hy_fw3[l],
              "fb3": hy_fb3[l], "fw4": hy_fw4[l], "freq": hy_freq[l]}
        ffn_w = [(ffn_in_b, ffn_out_b, l, i) for i in range(2)]

        x = _ffn(x, mod_x, g, ffn_w[0], 0, tm)
        xc = _ffn(xc, mod_c, g, ffn_w[0], 0, tm_c)

        ks_ctx = None
        if not last:
            ks_ctx = _hyena_spectra(_hyena_filter(fw, ctx_len, p_ctx), dft_ctx[0], p_ctx)
        xc, h_f, h_b = _mixer(xc, mod_c, g, lw, ffn_w[1], ks_ctx, dft_ctx, ctx_len, tm_c, p_ctx, q_ctx, h_zero,
                              h_zero, (bsz, ctx_len), not last)
        ks_lat = _hyena_spectra(_hyena_filter(fw, n_lat, p_lat), dft_lat[0], p_lat)
        x, _, _ = _mixer(x, mod_x, g, lw, ffn_w[1], ks_lat, dft_lat, GRID_W, tm, p_lat, q_lat, h_f, h_b,
                         (bsz, n_lat), True)
    return x
```
